```python
import functools
import jax, jax.numpy as jnp
from jax import lax
import numpy as np

D_MODEL = 1024
BATCH = 8
SEQ = 2048
DEPTH = 1
DEC_BATCH = 32
DEC_SEQ = 8
PAST_LEN = 16384
PAGE_SIZE = 128

MIX_DIM = D_MODEL
HEAD_DIM = 64
CONV_DIM = MIX_DIM // 4
CONV_W = 3
NSA_DIM = MIX_DIM - CONV_DIM
N_Q_HEADS = NSA_DIM // HEAD_DIM
N_KV = 2
Q_PER_KV = N_Q_HEADS // N_KV
CMP_BLOCK = 32
CMP_STRIDE = 16
CMP_HID = 2 * HEAD_DIM
SEL_BLOCK = 64
CMP_PER_SEL = SEL_BLOCK // CMP_STRIDE
N_SELECT = 16
WINDOW = 512
Q_BLOCK = 128
D_FF = -(-8 * D_MODEL // (3 * 256)) * 256
ALPHA = (2 * DEPTH) ** 0.25
BETA = (8 * DEPTH) ** -0.25
EPS = 1e-5
SCALE = HEAD_DIM ** -0.5
KV_COLS = 2 * N_KV * HEAD_DIM
_C3 = 3 * CONV_DIM
SPLITS = (CONV_DIM, 2 * CONV_DIM, _C3, _C3 + NSA_DIM, _C3 + NSA_DIM + KV_COLS,
          _C3 + NSA_DIM + 2 * KV_COLS, _C3 + NSA_DIM + 3 * KV_COLS)
PROJ_DIM = _C3 + NSA_DIM + 3 * KV_COLS + 3 * N_Q_HEADS

kernel_name = 'hybrid_conv_nsa_deepnorm_adaln_step'


def _layernorm(x, g, b):
    xf = x.astype(jnp.float32)
    mu = jnp.mean(xf, axis=-1, keepdims=True)
    var = jnp.mean(jnp.square(xf - mu), axis=-1, keepdims=True)
    return ((xf - mu) * lax.rsqrt(var + EPS) * g.astype(jnp.float32) + b.astype(jnp.float32)).astype(x.dtype)


def _rmsnorm(x, g):
    xf = x.astype(jnp.float32)
    return (xf * lax.rsqrt(jnp.mean(jnp.square(xf), axis=-1, keepdims=True) + EPS) * g.astype(jnp.float32)).astype(x.dtype)


def _modulation(c, w_ada, b_ada):
    mod = jax.nn.silu(c) @ w_ada + b_ada
    return [m[:, None, :] for m in jnp.split(mod, 6, axis=-1)]


def _pad_rows(a, n):
    return jnp.pad(a, [(0, 0), (0, n)] + [(0, 0)] * (a.ndim - 2))


def _in_proj(h, w_in):
    bn, t = h.shape[:2]
    hc, gb, gc, q, kvc, kvs, kvw, gl = jnp.split(h @ w_in, SPLITS, axis=-1)
    kv_shape = (bn, t, 2, N_KV, HEAD_DIM)
    gates = jax.nn.sigmoid(gl.astype(jnp.float32)).astype(h.dtype).reshape(bn, t, 3, N_KV, Q_PER_KV)
    return (gc * hc, gb, q.reshape(bn, t, N_KV, Q_PER_KV, HEAD_DIM),
            kvc.reshape(kv_shape), kvs.reshape(kv_shape), kvw.reshape(kv_shape), gates)


def _short_conv(u_ext, w):
    t = u_ext.shape[1] - (CONV_W - 1)
    y = w[0] * u_ext[:, 0:t]
    for j in range(1, CONV_W):
        y = y + w[j] * u_ext[:, j:j + t]
    return y


def _chunk_partials(rows, w1):
    bn, n = rows.shape[:2]
    ch = rows.reshape(bn, n // CMP_STRIDE, CMP_STRIDE, 2, N_KV, HEAD_DIM)
    top = jnp.einsum('bcjegd,ejdh->bcegh', ch, w1[:, :CMP_STRIDE])
    bot = jnp.einsum('bcjegd,ejdh->bcegh', ch, w1[:, CMP_STRIDE:])
    return top, bot


def _compress(top, bot, pe, w1, b1, w2):
    bias = jnp.einsum('ejd,ejdh->eh', pe, w1) + b1
    hid = jax.nn.gelu(top[:, :-1] + bot[:, 1:] + bias[:, None, :])
    return jnp.einsum('bcegh,ehd->bcegd', hid, w2)


def _masked_softmax(s, mask):
    s = jnp.where(mask, s, -jnp.inf)
    m = jnp.max(s, axis=-1, keepdims=True)
    m = jnp.where(jnp.isfinite(m), m, 0.0)
    e = jnp.exp(s - m)
    return e / jnp.maximum(jnp.sum(e, axis=-1, keepdims=True), 1e-30)


def _attend_shared(q, k, v, mask):
    s = jnp.einsum('btgrd,bsgd->bgrts', q, k).astype(jnp.float32) * SCALE
    p = _masked_softmax(s, mask)
    return jnp.einsum('bgrts,bsgd->btgrd', p.astype(v.dtype), v), p


def _attend_gathered(q, k, v, mask):
    s = jnp.einsum('btgrd,btgsd->btgrs', q, k).astype(jnp.float32) * SCALE
    p = _masked_softmax(s, mask[:, :, :, None, :])
    return jnp.einsum('btgrs,btgsd->btgrd', p.astype(v.dtype), v)


def _block_importance(imp, n_blocks):
    nc = imp.shape[-1]
    lead = [(0, 0)] * (imp.ndim - 1)
    pp = jnp.pad(imp, lead + [(0, CMP_PER_SEL * n_blocks - nc)])
    pp = pp.reshape(imp.shape[:-1] + (n_blocks, CMP_PER_SEL))
    tail = pp[..., -1]
    prev = jnp.pad(tail[..., :-1], lead + [(1, 0)])
    return pp.sum(axis=-1) + prev


def _flatten_sel(kv, idx):
    bn, t, g, k = idx.shape
    pos = idx[..., None] * SEL_BLOCK + jnp.arange(SEL_BLOCK)
    return (kv.reshape(bn, t, g, k * SEL_BLOCK, 2, HEAD_DIM), pos.reshape(bn, t, g, k * SEL_BLOCK))


def _gather_local(blocks, idx):
    b = jnp.arange(idx.shape[0])[:, None, None, None]
    g = jnp.arange(N_KV)[None, None, :, None]
    return _flatten_sel(blocks[b, idx, :, :, g, :], idx)


def _gather_paged(pool_blocks, new_blocks, page_table, blocks_per_page, idx):
    n_past = page_table.shape[1] * blocks_per_page
    n_new = new_blocks.shape[1]
    b = jnp.arange(idx.shape[0])[:, None, None, None]
    g = jnp.arange(N_KV)[None, None, :, None]
    jp = jnp.minimum(idx, n_past - 1)
    phys = page_table[b, jp // blocks_per_page] * blocks_per_page + jp % blocks_per_page
    kv_past = pool_blocks[phys, :, :, g, :]
    kv_new = new_blocks[b, jnp.clip(idx - n_past, 0, n_new - 1), :, :, g, :]
    kv = jnp.where((idx < n_past)[..., None, None, None], kv_past, kv_new)
    return _flatten_sel(kv, idx)


def _nsa_core(q, q_pos, gates, kv_cmpr, cmp_end, n_blocks, gather_sel, kvw, w_pos):
    cmask = cmp_end[None, :] <= q_pos[:, None]
    o_cmp, p_cmp = _attend_shared(q, kv_cmpr[:, :, 0], kv_cmpr[:, :, 1], cmask)
    imp = jnp.transpose(p_cmp.sum(axis=2), (0, 2, 1, 3))
    score = _block_importance(imp, n_blocks)
    j = jnp.arange(n_blocks)[None, :]
    cur = (q_pos // SEL_BLOCK)[:, None]
    valid = (j * SEL_BLOCK <= q_pos[:, None])[None, :, None, :]
    forced = ((j == 0) | (j == cur) | (j == cur - 1))[None, :, None, :]
    score = jnp.where(forced, jnp.inf, jnp.where(valid, score, -jnp.inf))
    _, idx = lax.top_k(score, min(N_SELECT, n_blocks))
    kvs, spos = gather_sel(idx)
    smask = spos <= q_pos[None, :, None, None]
    o_slc = _attend_gathered(q, kvs[..., 0, :], kvs[..., 1, :], smask)
    wmask = (w_pos[None, :] <= q_pos[:, None]) & (w_pos[None, :] > q_pos[:, None] - WINDOW) & (w_pos[None, :] >= 0)
    o_win, _ = _attend_shared(q, kvw[:, :, 0], kvw[:, :, 1], wmask)
    g = gates[..., None]
    return g[:, :, 0] * o_cmp + g[:, :, 1] * o_slc + g[:, :, 2] * o_win


def _prompt_mixer(h, w_in, conv_w, cmp_params):
    bn, s = h.shape[:2]
    u, gb, q, kvc, kvs, kvw, gates = _in_proj(h, w_in)
    y_conv = gb * _short_conv(jnp.pad(u, ((0, 0), (CONV_W - 1, 0), (0, 0))), conv_w)
    top, bot = _chunk_partials(_pad_rows(kvc, (-s) % CMP_STRIDE), cmp_params[1])
    kv_cmpr = _compress(top, bot, *cmp_params)
    cmp_end = CMP_STRIDE * jnp.arange(kv_cmpr.shape[1]) + (CMP_BLOCK - 1)
    ns = -(-s // SEL_BLOCK)
    slc_blocks = _pad_rows(kvs, ns * SEL_BLOCK - s).reshape(bn, ns, SEL_BLOCK, 2, N_KV, HEAD_DIM)
    kvw_pad = jnp.pad(kvw, ((0, 0), (WINDOW, 0), (0, 0), (0, 0), (0, 0)))
    qb = min(Q_BLOCK, s)
    nqb = s // qb

    def one_block(item):
        b = item // nqb
        start = (item % nqb) * qb
        pick = lambda a: lax.dynamic_slice_in_dim(a, b, 1, axis=0)
        q_i = lax.dynamic_slice_in_dim(pick(q), start, qb, axis=1)
        g_i = lax.dynamic_slice_in_dim(pick(gates), start, qb, axis=1)
        kvw_i = lax.dynamic_slice_in_dim(pick(kvw_pad), start, qb + WINDOW, axis=1)
        q_pos = start + jnp.arange(qb)
        w_pos = start - WINDOW + jnp.arange(qb + WINDOW)
        gather = functools.partial(_gather_local, pick(slc_blocks))
        return _nsa_core(q_i, q_pos, g_i, pick(kv_cmpr), cmp_end, ns, gather, kvw_i, w_pos)

    o = lax.map(one_block, jnp.arange(bn * nqb)).reshape(bn, s, NSA_DIM)
    w_keep = min(WINDOW, s)
    return y_conv, o, (kvc, kvs, kvw[:, s - w_keep:], u[:, s - (CONV_W - 1):])


def _sample_mixer(h, cmp_pool, slc_pool, win_buf, conv_buf, page_table, w_in, conv_w, cmp_params):
    bn, t = h.shape[:2]
    n_pages = page_table.shape[1]
    page = cmp_pool.shape[1]
    past = n_pages * page
    u, gb, q, kvc, kvs, kvw, gates = _in_proj(h, w_in)
    u_ext = jnp.concatenate([conv_buf.astype(u.dtype), u], axis=1)
    y_conv = gb * _short_conv(u_ext, conv_w)
    past_cmp = cmp_pool[page_table].reshape(bn, past, 2, N_KV, HEAD_DIM)
    top_p, bot_p = _chunk_partials(past_cmp, cmp_params[1])
    top_n, bot_n = _chunk_partials(_pad_rows(kvc, (-t) % CMP_STRIDE), cmp_params[1])
    kv_cmpr = _compress(jnp.concatenate([top_p, top_n], axis=1), jnp.concatenate([bot_p, bot_n], axis=1), *cmp_params)
    cmp_end = CMP_STRIDE * jnp.arange(kv_cmpr.shape[1]) + (CMP_BLOCK - 1)
    bpp = page // SEL_BLOCK
    nnb = -(-t // SEL_BLOCK)
    ns = n_pages * bpp + nnb
    new_blocks = _pad_rows(kvs, nnb * SEL_BLOCK - t).reshape(bn, nnb, SEL_BLOCK, 2, N_KV, HEAD_DIM)
    pool_blocks = slc_pool.reshape(-1, SEL_BLOCK, 2, N_KV, HEAD_DIM)
    gather = functools.partial(_gather_paged, pool_blocks, new_blocks, page_table, bpp)
    kvw_all = jnp.concatenate([win_buf, kvw], axis=1)
    w_buf = win_buf.shape[1]
    q_pos = past + jnp.arange(t)
    w_pos = past - w_buf + jnp.arange(w_buf + t)
    o = _nsa_core(q, q_pos, gates, kv_cmpr, cmp_end, ns, gather, kvw_all, w_pos)
    return y_conv, o.reshape(bn, t, NSA_DIM), (kvc, kvs, kvw_all[:, t:], u_ext[:, t:])


def _merge_groups(y_conv, o_nsa, g_conv, g_nsa, w_o):
    return jnp.concatenate([_rmsnorm(y_conv, g_conv), _rmsnorm(o_nsa, g_nsa)], axis=-1) @ w_o


def _post_block(x, mix, mods, ln1_g, ln1_b, ln2_g, ln2_b, w_gate, w_up, w_down):
    gate1, shift2, scale2, gate2 = mods[2], mods[3], mods[4], mods[5]
    x = _layernorm(ALPHA * x + gate1 * mix, ln1_g, ln1_b)
    h = x * (1 + scale2) + shift2
    f = (jax.nn.silu(h @ w_gate) * (h @ w_up)) @ w_down
    return _layernorm(ALPHA * x + gate2 * f, ln2_g, ln2_b)


def setup_inputs(seed: int = 0) -> dict:
    key = jax.random.key(seed)
    ks = iter(jax.random.split(key, 32))

    def nrm(shape, scale=1.0):
        return jax.random.normal(next(ks), shape, jnp.float32) * scale

    n_pages = PAST_LEN // PAGE_SIZE
    n_used = DEC_BATCH * n_pages
    n_phys = n_used + max(1, n_used // 4)
    w_buf = min(WINDOW, PAST_LEN)
    L = DEPTH
    page_table = jax.random.permutation(next(ks), n_phys)[:n_used].reshape(DEC_BATCH, n_pages).astype(jnp.int32)
    return {
        'x_prompt': nrm((BATCH, SEQ, D_MODEL)),
        'x_sample': nrm((DEC_BATCH, DEC_SEQ, D_MODEL)),
        'cache_cmp_kv': nrm((L, n_phys, PAGE_SIZE, 2, N_KV, HEAD_DIM)),
        'cache_slc_kv': nrm((L, n_phys, PAGE_SIZE, 2, N_KV, HEAD_DIM)),
        'cache_win_kv': nrm((L, DEC_BATCH, w_buf, 2, N_KV, HEAD_DIM)),
        'state_conv': nrm((L, DEC_BATCH, CONV_W - 1, CONV_DIM)),
        'page_table': page_table,
        'c_prompt': nrm((BATCH, D_MODEL)),
        'c_sample': nrm((DEC_BATCH, D_MODEL)),
        'w_ada': nrm((L, D_MODEL, 6 * D_MODEL), D_MODEL ** -0.5),
        'b_ada': nrm((L, 6 * D_MODEL), 0.02),
        'w_in': nrm((L, D_MODEL, PROJ_DIM), D_MODEL ** -0.5),
        'conv_w': nrm((L, CONV_W, CONV_DIM), CONV_W ** -0.5),
        'cmp_pe': nrm((L, 2, CMP_BLOCK, HEAD_DIM), 0.1),
        'cmp_w1': nrm((L, 2, CMP_BLOCK, HEAD_DIM, CMP_HID), (CMP_BLOCK * HEAD_DIM) ** -0.5),
        'cmp_b1': nrm((L, 2, CMP_HID), 0.02),
        'cmp_w2': nrm((L, 2, CMP_HID, HEAD_DIM), CMP_HID ** -0.5),
        'g_conv_out': 1.0 + nrm((L, CONV_DIM), 0.02),
        'g_nsa_out': 1.0 + nrm((L, NSA_DIM), 0.02),
        'w_o': nrm((L, MIX_DIM, D_MODEL), BETA * MIX_DIM ** -0.5),
        'ln1_g': 1.0 + nrm((L, D_MODEL), 0.02),
        'ln1_b': nrm((L, D_MODEL), 0.02),
        'ln2_g': 1.0 + nrm((L, D_MODEL), 0.02),
        'ln2_b': nrm((L, D_MODEL), 0.02),
        'w_ffn_gate': nrm((L, D_MODEL, D_FF), D_MODEL ** -0.5),
        'w_ffn_up': nrm((L, D_MODEL, D_FF), D_MODEL ** -0.5),
        'w_ffn_down': nrm((L, D_FF, D_MODEL), BETA * D_FF ** -0.5),
    }


def reference(x_prompt, x_sample, cache_cmp_kv, cache_slc_kv, cache_win_kv, state_conv, page_table,
              c_prompt, c_sample, w_ada, b_ada, w_in, conv_w, cmp_pe, cmp_w1, cmp_b1, cmp_w2,
              g_conv_out, g_nsa_out, w_o, ln1_g, ln1_b, ln2_g, ln2_b, w_ffn_gate, w_ffn_up, w_ffn_down):
    xp, xs = x_prompt, x_sample
    acc_p = [[], [], [], []]
    acc_s = [[], [], [], []]
    for l in range(DEPTH):
        cmp_params = (cmp_pe[l], cmp_w1[l], cmp_b1[l], cmp_w2[l])
        ffn = (ln1_g[l], ln1_b[l], ln2_g[l], ln2_b[l], w_ffn_gate[l], w_ffn_up[l], w_ffn_down[l])
        mods = _modulation(c_prompt, w_ada[l], b_ada[l])
        y_c, o_n, new = _prompt_mixer(xp * (1 + mods[1]) + mods[0], w_in[l], conv_w[l], cmp_params)
        xp = _post_block(xp, _merge_groups(y_c, o_n, g_conv_out[l], g_nsa_out[l], w_o[l]), mods, *ffn)
        for a, v in zip(acc_p, new):
            a.append(v)
        mods = _modulation(c_sample, w_ada[l], b_ada[l])
        y_c, o_n, new = _sample_mixer(xs * (1 + mods[1]) + mods[0], cache_cmp_kv[l], cache_slc_kv[l],
                                      cache_win_kv[l], state_conv[l], page_table, w_in[l], conv_w[l], cmp_params)
        xs = _post_block(xs, _merge_groups(y_c, o_n, g_conv_out[l], g_nsa_out[l], w_o[l]), mods, *ffn)
        for a, v in zip(acc_s, new):
            a.append(v)
    cmp_p, slc_p, win_p, conv_p = [jnp.stack(a) for a in acc_p]
    cmp_s, slc_s, win_s, conv_s = [jnp.stack(a) for a in acc_s]
    return (xp, xs, cmp_p, slc_p, win_p, conv_p, cmp_s, slc_s, win_s, conv_s)
```

```python
import functools

import jax
import jax.numpy as jnp
from jax import lax
from jax.experimental import pallas as pl
from jax.experimental.pallas import tpu as pltpu

HEAD_DIM = 64
N_KV = 2
Q_PER_KV = 6
N_HEADS = N_KV * Q_PER_KV
CONV_DIM = 256
CONV_W = 3
NSA_DIM = N_HEADS * HEAD_DIM
KV_COLS = 2 * N_KV * HEAD_DIM
CMP_BLOCK = 32
CMP_STRIDE = 16
CMP_HID = 2 * HEAD_DIM
SEL_BLOCK = 64
CMP_PER_SEL = SEL_BLOCK // CMP_STRIDE
N_SELECT = 16
WINDOW = 512
Q_BLOCK = 128
DEPTH = 1
ALPHA = (2 * DEPTH) ** 0.25
EPS = 1e-5
SCALE = HEAD_DIM ** -0.5

LANES = 128
QZ_DIM = N_HEADS * LANES
CHUNK_COLS = CMP_STRIDE * KV_COLS
TB_COLS = 2 * 2 * N_KV * CMP_HID
NEG = -1e30
REMOVED = -3e38
VMEM_LIMIT = 56 * 1024 * 1024

F32 = jnp.float32
BF16 = jnp.bfloat16


def _cparams(n_grid):
    return pltpu.CompilerParams(dimension_semantics=("arbitrary",) * n_grid, vmem_limit_bytes=VMEM_LIMIT)


def _const_spec(shape):
    nd = len(shape)
    return pl.BlockSpec(shape, lambda *_: (0,) * nd, pipeline_mode=pl.Buffered(1))


def _dot(a, b):
    return jnp.dot(a, b, preferred_element_type=F32)


def _dot_nt(a, b):
    return lax.dot_general(a, b, (((1,), (1,)), ((), ())), preferred_element_type=F32)


def _iota(shape, dim):
    return lax.broadcasted_iota(jnp.int32, shape, dim)


def _mod_kernel(c_ref, w_ref, b_ref, o_ref):
    c = c_ref[...]
    a = (c * jax.nn.sigmoid(c)).astype(BF16)
    o_ref[...] = _dot(a, w_ref[...].astype(BF16)) + b_ref[...]


def _modulation(c_all, w_ada, b_ada):
    m, d = c_all.shape
    n = w_ada.shape[1]
    tn = 512
    return pl.pallas_call(
        _mod_kernel,
        grid=(n // tn,),
        in_specs=[pl.BlockSpec((m, d), lambda j: (0, 0)),
                  pl.BlockSpec((d, tn), lambda j: (0, j)),
                  pl.BlockSpec((1, tn), lambda j: (0, j))],
        out_specs=pl.BlockSpec((m, tn), lambda j: (0, j)),
        out_shape=jax.ShapeDtypeStruct((m, n), F32),
        compiler_params=_cparams(1),
        name="modulation",
    )(c_all, w_ada, b_ada.reshape(1, n))


_C_HC, _C_GB, _C_GC, _C_Q = 0, CONV_DIM, 2 * CONV_DIM, 3 * CONV_DIM
_C_KVC = _C_Q + QZ_DIM
_C_KVS = _C_KVC + KV_COLS
_C_KVW = _C_KVS + KV_COLS
_C_GL = _C_KVW + KV_COLS
W_IN_COLS = _C_GL + LANES


def _inproj_kernel(x_ref, sc_ref, sh_ref, w_ref, u_ref, gb_ref, qz_ref, kvc_ref, kvs_ref, kvw_ref,
                   kvsb_ref, kvwb_ref, gates_ref):
    h = (x_ref[0] * (1.0 + sc_ref[0]) + sh_ref[0]).astype(BF16)
    hc = _dot(h, w_ref[:, _C_HC:_C_GB])
    gb_ref[0] = _dot(h, w_ref[:, _C_GB:_C_GC])
    gc = _dot(h, w_ref[:, _C_GC:_C_Q])
    u_ref[0] = gc * hc
    qz_ref[0] = _dot(h, w_ref[:, _C_Q:_C_KVC]).astype(qz_ref.dtype)
    kvc_ref[0] = _dot(h, w_ref[:, _C_KVC:_C_KVS])
    kvs = _dot(h, w_ref[:, _C_KVS:_C_KVW])
    kvs_ref[0] = kvs
    kvsb_ref[0] = kvs.astype(BF16)
    kvw = _dot(h, w_ref[:, _C_KVW:_C_GL])
    kvw_ref[0] = kvw
    kvwb_ref[0] = kvw.astype(BF16)
    gates_ref[0] = jax.nn.sigmoid(_dot(h, w_ref[:, _C_GL:W_IN_COLS]))


def _in_proj(x, scale, shift, w_cat, tm, qz_dtype):
    b, t, d = x.shape
    rm = scale.shape[1]
    mod_block = (1, tm, d) if rm == t else (1, 1, d)
    mod_map = (lambda i, j: (i, j, 0)) if rm == t else (lambda i, j: (i, 0, 0))
    row = lambda cols: pl.BlockSpec((1, tm, cols), lambda i, j: (i, j, 0))
    sds = lambda cols, dt: jax.ShapeDtypeStruct((b, t, cols), dt)
    return pl.pallas_call(
        _inproj_kernel,
        grid=(b, t // tm),
        in_specs=[row(d), pl.BlockSpec(mod_block, mod_map), pl.BlockSpec(mod_block, mod_map),
                  _const_spec((d, W_IN_COLS))],
        out_specs=[row(CONV_DIM), row(CONV_DIM), row(QZ_DIM), row(KV_COLS), row(KV_COLS), row(KV_COLS),
                   row(KV_COLS), row(KV_COLS), row(LANES)],
        out_shape=[sds(CONV_DIM, F32), sds(CONV_DIM, F32), sds(QZ_DIM, qz_dtype), sds(KV_COLS, F32),
                   sds(KV_COLS, F32), sds(KV_COLS, F32), sds(KV_COLS, BF16), sds(KV_COLS, BF16),
                   sds(LANES, F32)],
        compiler_params=_cparams(2),
        name="in_proj",
    )(x, scale, shift, w_cat)


def _gelu_tanh(x):
    return jax.nn.gelu(x, approximate=True)


def _cmp_bias(pe_ref, w1f_ref, b1_ref):
    parts = []
    for e in range(2):
        pe = jnp.broadcast_to(pe_ref[e:e + 1, :], (8, pe_ref.shape[1])).astype(BF16)
        be = _dot(pe, w1f_ref[e])[0:1, :] + b1_ref[e:e + 1, :]
        parts += [be] * N_KV
    return jnp.concatenate(parts, axis=1)


def _compress_dense_kernel(x_ref, wbig_ref, pe_ref, w1f_ref, b1_ref, w2_ref, o_ref):
    half = TB_COLS // 2
    tb = _dot(x_ref[0].astype(BF16), wbig_ref[...])
    n = tb.shape[0]
    top = tb[:, :half]
    bot_next = pltpu.roll(tb[:, half:], n - 1, 0)
    hid = _gelu_tanh(top + bot_next + _cmp_bias(pe_ref, w1f_ref, b1_ref))
    o_ref[0] = _dot(hid.astype(BF16), w2_ref[...]).astype(o_ref.dtype)


def _compress_dense(kvc, wbig, pe2, w1f, b1, w2big):
    b, s, _ = kvc.shape
    nch = s // CMP_STRIDE
    x = kvc.reshape(b, nch, CHUNK_COLS)
    return pl.pallas_call(
        _compress_dense_kernel,
        grid=(b,),
        in_specs=[pl.BlockSpec((1, nch, CHUNK_COLS), lambda i: (i, 0, 0)),
                  _const_spec(wbig.shape), _const_spec(pe2.shape), _const_spec(w1f.shape),
                  _const_spec(b1.shape), _const_spec(w2big.shape)],
        out_specs=pl.BlockSpec((1, nch, KV_COLS), lambda i: (i, 0, 0)),
        out_shape=jax.ShapeDtypeStruct((b, nch, KV_COLS), BF16),
        compiler_params=_cparams(1),
        name="compress_prompt",
    )(x, wbig, pe2, w1f, b1, w2big)


PAGES_PER_STEP = 16


def _chunk_partials_paged_kernel(pt_ref, *refs):
    page_refs, wbig_ref, o_ref = refs[:PAGES_PER_STEP], refs[PAGES_PER_STEP], refs[PAGES_PER_STEP + 1]
    x = jnp.concatenate([r[0] for r in page_refs], axis=0).astype(BF16)
    o_ref[0] = _dot(x, wbig_ref[...])


def _chunk_partials_paged(pool, page_table, wbig):
    bn, n_pages = page_table.shape
    cpp = pool.shape[1]
    g = PAGES_PER_STEP

    def page_spec(k):
        return pl.BlockSpec((1, cpp, CHUNK_COLS), lambda i, j, pt: (pt[i, j * g + k], 0, 0))

    grid_spec = pltpu.PrefetchScalarGridSpec(
        num_scalar_prefetch=1,
        grid=(bn, n_pages // g),
        in_specs=[page_spec(k) for k in range(g)]
        + [pl.BlockSpec(wbig.shape, lambda i, j, pt: (0, 0), pipeline_mode=pl.Buffered(1))],
        out_specs=pl.BlockSpec((1, g * cpp, TB_COLS), lambda i, j, pt: (i, j, 0)),
    )
    return pl.pallas_call(
        _chunk_partials_paged_kernel,
        grid_spec=grid_spec,
        out_shape=jax.ShapeDtypeStruct((bn, n_pages * cpp, TB_COLS), F32),
        compiler_params=_cparams(2),
        name="chunk_partials_paged",
    )(page_table, *([pool] * g), wbig)


def _compress_paged_kernel(tb_ref, xnew_ref, wbig_ref, pe_ref, w1f_ref, b1_ref, w2_ref, o_ref):
    half = TB_COLS // 2
    n = tb_ref.shape[1]
    xn = jnp.broadcast_to(xnew_ref[0], (8, CHUNK_COLS)).astype(BF16)
    bot_new = _dot(xn, wbig_ref[:, half:])[0:1, :]
    top = tb_ref[0, :, :half]
    bot_next = jnp.concatenate([tb_ref[0, 1:n, half:], bot_new], axis=0)
    hid = _gelu_tanh(top + bot_next + _cmp_bias(pe_ref, w1f_ref, b1_ref))
    o_ref[0] = _dot(hid.astype(BF16), w2_ref[...]).astype(o_ref.dtype)


def _compress_paged(tb, xnew, wbig, pe2, w1f, b1, w2big):
    bn, n, _ = tb.shape
    return pl.pallas_call(
        _compress_paged_kernel,
        grid=(bn,),
        in_specs=[pl.BlockSpec((1, n, TB_COLS), lambda i: (i, 0, 0)),
                  pl.BlockSpec((1, 1, CHUNK_COLS), lambda i: (i, 0, 0)),
                  _const_spec(wbig.shape), _const_spec(pe2.shape), _const_spec(w1f.shape),
                  _const_spec(b1.shape), _const_spec(w2big.shape)],
        out_specs=pl.BlockSpec((1, n, KV_COLS), lambda i: (i, 0, 0)),
        out_shape=jax.ShapeDtypeStruct((bn, n, KV_COLS), BF16),
        compiler_params=_cparams(1),
        name="compress_sample",
    )(tb, xnew, wbig, pe2, w1f, b1, w2big)


def _split3_dot(x, m01):
    h1 = x.astype(BF16)
    r1 = x - h1.astype(F32)
    h2 = r1.astype(BF16)
    h3 = (r1 - h2.astype(F32)).astype(BF16)
    return _dot(h1, m01) + _dot(h2, m01) + _dot(h3, m01)


def _block_scores(imp, n_lanes):
    ncp = imp.shape[1]
    c = _iota((ncp, n_lanes), 0)
    j = _iota((ncp, n_lanes), 1)
    a = jnp.where((c >= CMP_PER_SEL * j - 1) & (c <= CMP_PER_SEL * j + CMP_PER_SEL - 1), 1.0, 0.0).astype(BF16)
    return _split3_dot(imp, a)


def _select_blocks(score, qpos, n_blocks, n_select):
    r, l = score.shape
    j = _iota((r, l), 1)
    cur = qpos >> 6
    valid = (j * SEL_BLOCK <= qpos) & (j < n_blocks)
    forced = ((j == 0) | (j == cur) | (j == cur - 1)) & (j < n_blocks)
    sc = jnp.where(forced, -NEG, jnp.where(valid, score, NEG))
    sc = jnp.where(j < n_blocks, sc, REMOVED)
    jf = j.astype(F32)

    def body(_, carry):
        sc, sel = carry
        mx = jnp.max(sc, axis=-1, keepdims=True)
        first = jnp.min(jnp.where(sc == mx, jf, 1e9), axis=-1, keepdims=True)
        pick = jf == first
        return jnp.where(pick, REMOVED, sc), jnp.where(pick, 1.0, sel)

    _, sel = lax.fori_loop(0, n_select, body, (sc, jnp.zeros((r, l), F32)))
    return sel


def _softmax_parts(s, mask):
    s = jnp.where(mask, s, NEG)
    m = jnp.max(s, axis=-1, keepdims=True)
    e = jnp.where(mask, jnp.exp(s - m), 0.0)
    return e, jnp.sum(e, axis=-1, keepdims=True)


SEL_TILE = 512


def _attn_prompt_kernel(qz_ref, kvs_ref, kvw_ref, kvc_ref, gates_ref, o_ref, m_sc, l_sc, acc_sc, *, seq):
    qb = Q_BLOCK
    start = pl.program_id(1) * qb
    qpos = start + _iota((qb, 1), 0)
    ncp = kvc_ref.shape[1]
    n_blocks = -(-seq // SEL_BLOCK)
    rows = Q_PER_KV * qb
    gates = gates_ref[0]

    for g in range(N_KV):
        q_g = jnp.concatenate(
            [qz_ref[0, :, (g * Q_PER_KV + r) * LANES:(g * Q_PER_KV + r + 1) * LANES] for r in range(Q_PER_KV)],
            axis=0)

        s = _dot_nt(q_g, kvc_ref[0, :, 0:LANES]).reshape(Q_PER_KV, qb, ncp)
        c = _iota((qb, ncp), 1)
        cmask = (CMP_STRIDE * c + (CMP_BLOCK - 1) <= qpos) & (c < ncp - 1)
        e, l = _softmax_parts(s, cmask)
        p = e / jnp.maximum(l, 1e-30)
        o_cmp = _dot(p.reshape(rows, ncp).astype(BF16), kvc_ref[0, :, LANES:2 * LANES])
        imp = jnp.sum(p, axis=0)
        sel = _select_blocks(_block_scores(imp, LANES), qpos, n_blocks, min(N_SELECT, n_blocks)).astype(BF16)

        m_sc[...] = jnp.full(m_sc.shape, NEG, F32)
        l_sc[...] = jnp.zeros(l_sc.shape, F32)
        acc_sc[...] = jnp.zeros(acc_sc.shape, F32)

        def sel_tile(kt, _):
            k0 = pl.multiple_of(kt * SEL_TILE, SEL_TILE)
            kk = kvs_ref[0, pl.ds(k0, SEL_TILE), 0:LANES]
            vv = kvs_ref[0, pl.ds(k0, SEL_TILE), LANES:2 * LANES]
            kpos = k0 + _iota((1, SEL_TILE), 1)
            expand = jnp.where(_iota((LANES, SEL_TILE), 0) == ((k0 + _iota((LANES, SEL_TILE), 1)) >> 6),
                               1.0, 0.0).astype(BF16)
            mask = (_dot(sel, expand) > 0.5) & (kpos <= qpos)
            s = _dot_nt(q_g, kk).reshape(Q_PER_KV, qb, SEL_TILE)
            s = jnp.where(mask, s, NEG)
            m_prev = m_sc[...]
            m_new = jnp.maximum(m_prev, jnp.max(s, axis=-1, keepdims=True))
            alpha = jnp.exp(m_prev - m_new)
            e = jnp.where(mask, jnp.exp(s - m_new), 0.0)
            l_sc[...] = alpha * l_sc[...] + jnp.sum(e, axis=-1, keepdims=True)
            pv = _dot(e.reshape(rows, SEL_TILE).astype(BF16), vv).reshape(Q_PER_KV, qb, LANES)
            acc_sc[...] = alpha * acc_sc[...] + pv
            m_sc[...] = m_new
            return 0

        lax.fori_loop(0, (start + qb + SEL_TILE - 1) // SEL_TILE, sel_tile, 0)
        o_slc = acc_sc[...] / jnp.maximum(l_sc[...], 1e-30)

        wlen = WINDOW + qb
        w0 = pl.multiple_of(jnp.maximum(start - WINDOW, 0), qb)
        kk = kvw_ref[0, pl.ds(w0, wlen), 0:LANES]
        vv = kvw_ref[0, pl.ds(w0, wlen), LANES:2 * LANES]
        kpos = w0 + _iota((1, wlen), 1)
        wmask = (kpos <= qpos) & (kpos > qpos - WINDOW)
        e, l = _softmax_parts(_dot_nt(q_g, kk).reshape(Q_PER_KV, qb, wlen), wmask)
        o_win = _dot(e.reshape(rows, wlen).astype(BF16), vv).reshape(Q_PER_KV, qb, LANES) / jnp.maximum(l, 1e-30)

        o_cmp = o_cmp.reshape(Q_PER_KV, qb, LANES)
        own = (_iota((qb, LANES), 1) >> 6) == g
        for r in range(Q_PER_KV):
            h = g * Q_PER_KV + r
            o = (gates[:, h:h + 1] * o_cmp[r] + gates[:, N_HEADS + h:N_HEADS + h + 1] * o_slc[r]
                 + gates[:, 2 * N_HEADS + h:2 * N_HEADS + h + 1] * o_win[r])
            o_ref[0, :, h * LANES:(h + 1) * LANES] = jnp.where(own, o, 0.0)


def _attn_prompt(qz, kvs_b, kvw_b, kv_cmpr, gates):
    b, s, _ = qz.shape
    assert s % Q_BLOCK == 0 and s % SEL_TILE == 0 and s >= WINDOW + Q_BLOCK
    ncp = kv_cmpr.shape[1]
    full = lambda n: pl.BlockSpec((1, n, KV_COLS), lambda i, j: (i, 0, 0))
    return pl.pallas_call(
        functools.partial(_attn_prompt_kernel, seq=s),
        grid=(b, s // Q_BLOCK),
        in_specs=[pl.BlockSpec((1, Q_BLOCK, QZ_DIM), lambda i, j: (i, j, 0)), full(s), full(s), full(ncp),
                  pl.BlockSpec((1, Q_BLOCK, LANES), lambda i, j: (i, j, 0))],
        out_specs=pl.BlockSpec((1, Q_BLOCK, QZ_DIM), lambda i, j: (i, j, 0)),
        out_shape=jax.ShapeDtypeStruct((b, s, QZ_DIM), F32),
        scratch_shapes=[pltpu.VMEM((Q_PER_KV, Q_BLOCK, 1), F32), pltpu.VMEM((Q_PER_KV, Q_BLOCK, 1), F32),
                        pltpu.VMEM((Q_PER_KV, Q_BLOCK, LANES), F32)],
        compiler_params=_cparams(2),
        name="attn_prompt",
    )(qz, kvs_b, kvw_b, kv_cmpr, gates)


def _attn_sample_kernel(pt_ref, *refs, past, n_steps):
    g_pages = PAGES_PER_STEP
    page_refs = refs[:g_pages]
    (qz_ref, kvc_ref, gates_ref, kvs_new_ref, win_ref, kvw_new_ref,
     o_ref, q_sc, sel_sc, m_sc, l_sc, acc_sc, ocw_sc) = refs[g_pages:]
    kt = pl.program_id(1)
    t = qz_ref.shape[1]
    rows = N_HEADS * t
    page = page_refs[0].shape[1]
    tile = g_pages * page
    blocks_per_tile = tile // SEL_BLOCK
    n_past_blocks = past // SEL_BLOCK
    n_blocks = n_past_blocks + -(-t // SEL_BLOCK)
    sel_lanes = sel_sc.shape[0] * LANES
    tpos = _iota((t, 1), 0)
    qpos = past + tpos
    qpos_gt = jnp.concatenate([qpos] * N_KV, axis=0)
    own_rows = _iota((N_KV, Q_PER_KV, t, LANES), 0) == (_iota((N_KV, Q_PER_KV, t, LANES), 3) >> 6)

    def online_update(s, mask, vv):
        k = s.shape[1]
        s = jnp.where(mask, s.reshape(N_KV, Q_PER_KV, t, k), NEG)
        m_prev = m_sc[...]
        m_new = jnp.maximum(m_prev, jnp.max(s, axis=-1, keepdims=True))
        alpha = jnp.exp(m_prev - m_new)
        e = jnp.where(mask, jnp.exp(s - m_new), 0.0)
        l_sc[...] = alpha * l_sc[...] + jnp.sum(e, axis=-1, keepdims=True)
        pv = _dot(e.reshape(rows, k).astype(BF16), vv).reshape(N_KV, Q_PER_KV, t, LANES)
        acc_sc[...] = alpha * acc_sc[...] + pv
        m_sc[...] = m_new

    @pl.when(kt == 0)
    def _():
        q = jnp.concatenate([qz_ref[0, :, h * LANES:(h + 1) * LANES] for h in range(N_HEADS)], axis=0)
        q_sc[...] = q.astype(BF16)
        qb = q_sc[...]
        ncp = kvc_ref.shape[1]
        s = _dot_nt(qb, kvc_ref[0, :, 0:LANES]).reshape(N_KV * Q_PER_KV, t, ncp)
        c = _iota((t, ncp), 1)
        cmask = CMP_STRIDE * c + (CMP_BLOCK - 1) <= qpos
        e, l = _softmax_parts(s, cmask)
        p = e / jnp.maximum(l, 1e-30)
        o_cmp = _dot(p.reshape(rows, ncp).astype(BF16), kvc_ref[0, :, LANES:2 * LANES])
        imp = jnp.sum(p.reshape(N_KV, Q_PER_KV, t, ncp), axis=1).reshape(N_KV * t, ncp)
        sel = _select_blocks(_block_scores(imp, sel_lanes), qpos_gt, n_blocks, min(N_SELECT, n_blocks))
        for w in range(sel_sc.shape[0]):
            sel_sc[w] = sel[:, w * LANES:(w + 1) * LANES]
        wb = win_ref.shape[1]
        kvw = jnp.concatenate([win_ref[0], kvw_new_ref[0]], axis=0).astype(BF16)
        kpos = past - wb + _iota((1, wb + t), 1)
        wmask = (kpos <= qpos) & (kpos > qpos - WINDOW) & (kpos >= 0)
        e, l = _softmax_parts(_dot_nt(qb, kvw[:, 0:LANES]).reshape(N_KV * Q_PER_KV, t, wb + t), wmask)
        o_win = _dot(e.reshape(rows, wb + t).astype(BF16), kvw[:, LANES:2 * LANES])
        o_win = o_win.reshape(N_KV * Q_PER_KV, t, LANES) / jnp.maximum(l, 1e-30)
        gts = gates_ref[0]
        gc = jnp.stack([jnp.broadcast_to(gts[:, h:h + 1], (t, LANES)) for h in range(N_HEADS)])
        gw = jnp.stack([jnp.broadcast_to(gts[:, 2 * N_HEADS + h:2 * N_HEADS + h + 1], (t, LANES))
                        for h in range(N_HEADS)])
        ocw_sc[...] = gc * o_cmp.reshape(N_KV * Q_PER_KV, t, LANES) + gw * o_win
        m_sc[...] = jnp.full(m_sc.shape, NEG, F32)
        l_sc[...] = jnp.zeros(l_sc.shape, F32)
        acc_sc[...] = jnp.zeros(acc_sc.shape, F32)
        kvn = kvs_new_ref[0].astype(BF16)
        npos = past + _iota((1, t), 1)
        jn = npos >> 6
        seln = jnp.zeros((N_KV * t, t), F32)
        for jb in range(n_past_blocks, n_blocks):
            col = sel[:, jb:jb + 1]
            seln = jnp.where(jn == jb, col, seln)
        nmask = ((seln > 0.5) & (npos <= qpos_gt)).reshape(N_KV, 1, t, t)
        online_update(_dot_nt(qb, kvn[:, 0:LANES]), nmask, kvn[:, LANES:2 * LANES])

    kv = jnp.concatenate([r[0] for r in page_refs], axis=0).astype(BF16)
    b0 = kt * blocks_per_tile
    selw = sel_sc[b0 // LANES].astype(BF16)
    off = b0 % LANES
    expand = jnp.where(_iota((LANES, tile), 0) == off + (_iota((LANES, tile), 1) >> 6), 1.0, 0.0).astype(BF16)
    kpos = kt * tile + _iota((1, tile), 1)
    mask = ((_dot(selw, expand) > 0.5) & (kpos <= qpos_gt)).reshape(N_KV, 1, t, tile)
    online_update(_dot_nt(q_sc[...], kv[:, 0:LANES]), mask, kv[:, LANES:2 * LANES])

    @pl.when(kt == n_steps - 1)
    def _():
        gts = gates_ref[0]
        gs = jnp.stack([jnp.broadcast_to(gts[:, N_HEADS + h:N_HEADS + h + 1], (t, LANES))
                        for h in range(N_HEADS)])
        o_slc = (acc_sc[...] / jnp.maximum(l_sc[...], 1e-30)).reshape(N_HEADS, t, LANES)
        o = jnp.where(own_rows.reshape(N_HEADS, t, LANES), ocw_sc[...] + gs * o_slc, 0.0)
        for h in range(N_HEADS):
            o_ref[0, :, h * LANES:(h + 1) * LANES] = o[h]


def _attn_sample(qz, kv_cmpr, gates, kvs_new, slc_pool, win_buf, kvw_new, page_table):
    bn, t, _ = qz.shape
    n_pages = page_table.shape[1]
    page = slc_pool.shape[1]
    past = n_pages * page
    g = PAGES_PER_STEP
    n_steps = n_pages // g
    assert n_pages % g == 0 and page % SEL_BLOCK == 0 and (g * page // SEL_BLOCK) <= LANES
    assert LANES % (g * page // SEL_BLOCK) == 0 and t % 8 == 0
    n_blocks = past // SEL_BLOCK + -(-t // SEL_BLOCK)
    sel_groups = -(-n_blocks // LANES)
    ncp = kv_cmpr.shape[1]
    wb = win_buf.shape[1]

    def page_spec(k):
        return pl.BlockSpec((1, page, KV_COLS), lambda i, j, pt: (pt[i, j * g + k], 0, 0))

    per_b = lambda n, cols: pl.BlockSpec((1, n, cols), lambda i, j, pt: (i, 0, 0))
    grid_spec = pltpu.PrefetchScalarGridSpec(
        num_scalar_prefetch=1,
        grid=(bn, n_steps),
        in_specs=[page_spec(k) for k in range(g)]
        + [per_b(t, QZ_DIM), per_b(ncp, KV_COLS), per_b(t, LANES), per_b(t, KV_COLS), per_b(wb, KV_COLS),
           per_b(t, KV_COLS)],
        out_specs=per_b(t, QZ_DIM),
        scratch_shapes=[pltpu.VMEM((N_HEADS * t, LANES), BF16),
                        pltpu.VMEM((sel_groups, N_KV * t, LANES), F32),
                        pltpu.VMEM((N_KV, Q_PER_KV, t, 1), F32), pltpu.VMEM((N_KV, Q_PER_KV, t, 1), F32),
                        pltpu.VMEM((N_KV, Q_PER_KV, t, LANES), F32),
                        pltpu.VMEM((N_HEADS, t, LANES), F32)],
    )
    return pl.pallas_call(
        functools.partial(_attn_sample_kernel, past=past, n_steps=n_steps),
        grid_spec=grid_spec,
        out_shape=jax.ShapeDtypeStruct((bn, t, QZ_DIM), F32),
        compiler_params=_cparams(2),
        name="attn_sample",
    )(page_table, *([slc_pool] * g), qz, kv_cmpr, gates, kvs_new, win_buf, kvw_new)


def _layernorm(x, g, b):
    mu = jnp.mean(x, axis=-1, keepdims=True)
    xc = x - mu
    var = jnp.mean(xc * xc, axis=-1, keepdims=True)
    return xc * lax.rsqrt(var + EPS) * g + b


def _post_kernel(x_ref, o_ref, u_ref, uprev_ref, gb_ref, st1_ref, st2_ref, g1_ref, sh2_ref, sc2_ref, g2_ref,
                 convw_ref, gconv_ref, gnsa_ref, woc_ref, won_ref, ln1g_ref, ln1b_ref, ln2g_ref, ln2b_ref,
                 wg_ref, wu_ref, wd_ref, y_ref, *, seq):
    tm = x_ref.shape[1]
    u = u_ref[0]
    ext = jnp.concatenate([uprev_ref[0], u], axis=0)
    if tm <= seq:
        pos = (pl.program_id(1) * tm) % seq + _iota((tm, 1), 0)
    else:
        pos = lax.rem(_iota((tm, 1), 0), seq)
    p1 = jnp.where(pos >= 1, ext[7:7 + tm], st1_ref[0])
    p2 = jnp.where(pos >= 2, ext[6:6 + tm], st2_ref[0])
    cw = convw_ref[...]
    y_c = gb_ref[0] * (cw[0:1] * p2 + cw[1:2] * p1 + cw[2:3] * u)
    yn = y_c * lax.rsqrt(jnp.mean(y_c * y_c, axis=-1, keepdims=True) + EPS) * gconv_ref[...]
    o = o_ref[0]
    on = o * lax.rsqrt(jnp.sum(o * o, axis=-1, keepdims=True) * (1.0 / NSA_DIM) + EPS) * gnsa_ref[...]
    mix = _dot(yn.astype(BF16), woc_ref[...]) + _dot(on.astype(BF16), won_ref[...])
    x1 = _layernorm(ALPHA * x_ref[0] + g1_ref[0] * mix, ln1g_ref[...], ln1b_ref[...])
    h = (x1 * (1.0 + sc2_ref[0]) + sh2_ref[0]).astype(BF16)
    a = _dot(h, wg_ref[...])
    f = (a * jax.nn.sigmoid(a)) * _dot(h, wu_ref[...])
    f = _dot(f.astype(BF16), wd_ref[...])
    y_ref[0] = _layernorm(ALPHA * x1 + g2_ref[0] * f, ln2g_ref[...], ln2b_ref[...])


def _post(x, o, u, gb, st1, st2, mods, w, tm, seq):
    b, t, d = x.shape
    row = lambda cols: pl.BlockSpec((1, tm, cols), lambda i, j: (i, j, 0))

    def bcast(a):
        if a.shape[1] == t:
            return pl.BlockSpec((1, tm, a.shape[2]), lambda i, j: (i, j, 0))
        return pl.BlockSpec((1, 1, a.shape[2]), lambda i, j: (i, 0, 0))

    prev = pl.BlockSpec((1, 8, CONV_DIM), lambda i, j: (i, jnp.maximum(j * (tm // 8) - 1, 0), 0))
    consts = [w["conv_w"], w["g_conv"], w["g_nsa"], w["wo_c"], w["wo_n"], w["ln1_g"], w["ln1_b"], w["ln2_g"],
              w["ln2_b"], w["w_gate"], w["w_up"], w["w_down"]]
    return pl.pallas_call(
        functools.partial(_post_kernel, seq=seq),
        grid=(b, t // tm),
        in_specs=[row(d), row(QZ_DIM), row(CONV_DIM), prev, row(CONV_DIM), bcast(st1), bcast(st2)]
        + [bcast(m) for m in mods] + [_const_spec(c.shape) for c in consts],
        out_specs=row(d),
        out_shape=jax.ShapeDtypeStruct((b, t, d), F32),
        compiler_params=_cparams(2),
        name="post_block",
    )(x, o, u, u, gb, st1, st2, *mods, *consts)


def _prep_w_in(w_in):
    d = w_in.shape[0]
    c3 = 3 * CONV_DIM
    splits = (CONV_DIM, 2 * CONV_DIM, c3, c3 + NSA_DIM, c3 + NSA_DIM + KV_COLS, c3 + NSA_DIM + 2 * KV_COLS,
              c3 + NSA_DIM + 3 * KV_COLS)
    hc, gb, gc, q, kvc, kvs, kvw, gl = jnp.split(w_in, splits, axis=1)
    q4 = q.reshape(d, N_KV, Q_PER_KV, HEAD_DIM) * SCALE
    qz = jnp.einsum("dgrh,gk->dgrkh", q4, jnp.eye(N_KV, dtype=w_in.dtype)).reshape(d, QZ_DIM)
    glp = jnp.pad(gl, ((0, 0), (0, LANES - gl.shape[1])))
    return jnp.concatenate([hc, gb, gc, qz, kvc, kvs, kvw, glp], axis=1).astype(BF16)


def _prep_cmp(cmp_pe, cmp_w1, cmp_b1, cmp_w2):
    eye = jnp.eye(N_KV, dtype=cmp_w1.dtype)
    eye2 = jnp.eye(2, dtype=cmp_w1.dtype)
    w1r = cmp_w1.reshape(2, 2, CMP_STRIDE, HEAD_DIM, CMP_HID)
    wbig = jnp.einsum("etjdh,ef,gk->jegdtfkh", w1r, eye2, eye).reshape(CHUNK_COLS, TB_COLS).astype(BF16)
    w2big = jnp.einsum("ehd,ef,gk->eghfkd", cmp_w2, eye2, eye).reshape(TB_COLS // 2, KV_COLS).astype(BF16)
    pe2 = cmp_pe.reshape(2, CMP_BLOCK * HEAD_DIM)
    w1f = cmp_w1.reshape(2, CMP_BLOCK * HEAD_DIM, CMP_HID).astype(BF16)
    return wbig, pe2, w1f, cmp_b1, w2big


def _prep_post(conv_w, g_conv, g_nsa, w_o, ln1_g, ln1_b, ln2_g, ln2_b, w_gate, w_up, w_down):
    d = w_o.shape[1]
    own = jnp.eye(N_KV, dtype=w_o.dtype)
    g_nsa_z = jnp.einsum("grh,gk->grkh", g_nsa.reshape(N_KV, Q_PER_KV, HEAD_DIM), own).reshape(1, QZ_DIM)
    wo_n = jnp.einsum("grhd,gk->grkhd", w_o[CONV_DIM:].reshape(N_KV, Q_PER_KV, HEAD_DIM, d), own)
    row = lambda a: a.reshape(1, -1)
    return dict(conv_w=conv_w, g_conv=row(g_conv), g_nsa=g_nsa_z, wo_c=w_o[:CONV_DIM].astype(BF16),
                wo_n=wo_n.reshape(QZ_DIM, d).astype(BF16), ln1_g=row(ln1_g), ln1_b=row(ln1_b),
                ln2_g=row(ln2_g), ln2_b=row(ln2_b), w_gate=w_gate.astype(BF16), w_up=w_up.astype(BF16),
                w_down=w_down.astype(BF16))


def kernel(x_prompt, x_sample, cache_cmp_kv, cache_slc_kv, cache_win_kv, state_conv, page_table, c_prompt,
           c_sample, w_ada, b_ada, w_in, conv_w, cmp_pe, cmp_w1, cmp_b1, cmp_w2, g_conv_out, g_nsa_out, w_o,
           ln1_g, ln1_b, ln2_g, ln2_b, w_ffn_gate, w_ffn_up, w_ffn_down):
    assert w_ada.shape[0] == DEPTH
    bp, s, d = x_prompt.shape
    bs, t, _ = x_sample.shape
    n_phys, page = cache_cmp_kv.shape[1:3]
    kv_shape = (2, N_KV, HEAD_DIM)

    w_cat = _prep_w_in(w_in[0])
    cmp_w = _prep_cmp(cmp_pe[0], cmp_w1[0], cmp_b1[0], cmp_w2[0])
    post_w = _prep_post(conv_w[0], g_conv_out[0], g_nsa_out[0], w_o[0], ln1_g[0], ln1_b[0], ln2_g[0], ln2_b[0],
                        w_ffn_gate[0], w_ffn_up[0], w_ffn_down[0])

    mod = _modulation(jnp.concatenate([c_prompt, c_sample], axis=0), w_ada[0], b_ada[0])
    mods_p = [m[:, None, :] for m in jnp.split(mod[:bp], 6, axis=-1)]
    mods_s = [jnp.repeat(m, t, axis=0)[None] for m in jnp.split(mod[bp:], 6, axis=-1)]

    u, gb, qz, kvc, kvs, kvw, kvs_b, kvw_b, gates = _in_proj(x_prompt, mods_p[1], mods_p[0], w_cat, 512, BF16)
    kv_cmpr = _compress_dense(kvc, *cmp_w)
    o = _attn_prompt(qz, kvs_b, kvw_b, kv_cmpr, gates)
    zero_state = jnp.zeros((bp, 1, CONV_DIM), F32)
    y_prompt = _post(x_prompt, o, u, gb, zero_state, zero_state, mods_p[2:], post_w, 256, s)
    w_keep = min(WINDOW, s)
    cmp_p = kvc.reshape(1, bp, s, *kv_shape)
    slc_p = kvs.reshape(1, bp, s, *kv_shape)
    win_p = kvw[:, s - w_keep:].reshape(1, bp, w_keep, *kv_shape)
    conv_p = u[:, s - (CONV_W - 1):][None]

    rows = bs * t
    us, gbs, qzs, kvcs, kvss, kvws, _, _, gates_s = _in_proj(
        x_sample.reshape(1, rows, d), mods_s[1], mods_s[0], w_cat, rows, F32)
    per_b = lambda a: a.reshape(bs, t, a.shape[-1])
    kvcs, kvss, kvws = per_b(kvcs), per_b(kvss), per_b(kvws)
    cmp_pool = cache_cmp_kv[0].reshape(n_phys, page // CMP_STRIDE, CHUNK_COLS)
    tb = _chunk_partials_paged(cmp_pool, page_table, cmp_w[0])
    xnew = jnp.pad(kvcs, ((0, 0), (0, (-t) % CMP_STRIDE), (0, 0))).reshape(bs, 1, CHUNK_COLS)
    kv_cmpr_s = _compress_paged(tb, xnew, *cmp_w)
    slc_pool = cache_slc_kv[0].reshape(n_phys, page, KV_COLS)
    win_buf = cache_win_kv[0].reshape(bs, -1, KV_COLS)
    o_s = _attn_sample(per_b(qzs), kv_cmpr_s, per_b(gates_s), kvss, slc_pool, win_buf, kvws, page_table)
    st = state_conv[0]
    tpos = jnp.arange(t)[None, :, None]
    st1 = jnp.where(tpos == 0, st[:, 1:2], 0.0).reshape(1, rows, CONV_DIM)
    st2 = jnp.where(tpos == 0, st[:, 0:1], jnp.where(tpos == 1, st[:, 1:2], 0.0)).reshape(1, rows, CONV_DIM)
    y_sample = _post(x_sample.reshape(1, rows, d), o_s.reshape(1, rows, QZ_DIM), us, gbs, st1, st2, mods_s[2:],
                     post_w, rows, t).reshape(bs, t, d)
    cmp_s = kvcs.reshape(1, bs, t, *kv_shape)
    slc_s = kvss.reshape(1, bs, t, *kv_shape)
    win_s = jnp.concatenate([win_buf, kvws], axis=1)[:, t:].reshape(1, bs, -1, *kv_shape)
    conv_s = jnp.concatenate([st, per_b(us)], axis=1)[:, t:][None]
    return (y_prompt, y_sample, cmp_p, slc_p, win_p, conv_p, cmp_s, slc_s, win_s, conv_s)
```

```python
import functools

import jax
import jax.numpy as jnp
from jax import lax
from jax.experimental import pallas as pl
from jax.experimental.pallas import tpu as pltpu

HEAD_DIM = 64
N_KV = 2
Q_PER_KV = 6
N_HEADS = N_KV * Q_PER_KV
CONV_DIM = 256
CONV_W = 3
NSA_DIM = N_HEADS * HEAD_DIM
KV_COLS = 2 * N_KV * HEAD_DIM
CMP_BLOCK = 32
CMP_STRIDE = 16
CMP_HID = 2 * HEAD_DIM
SEL_BLOCK = 64
CMP_PER_SEL = SEL_BLOCK // CMP_STRIDE
N_SELECT = 16
WINDOW = 512
Q_BLOCK = 128
DEPTH = 1
ALPHA = (2 * DEPTH) ** 0.25
EPS = 1e-5
SCALE = HEAD_DIM ** -0.5

LANES = 128
QZ_DIM = N_HEADS * LANES
CHUNK_COLS = CMP_STRIDE * KV_COLS
TB_COLS = 2 * 2 * N_KV * CMP_HID
NEG = -1e30
VMEM_LIMIT = 56 * 1024 * 1024

F32 = jnp.float32
BF16 = jnp.bfloat16


def _cparams(n_grid):
    return pltpu.CompilerParams(dimension_semantics=("arbitrary",) * n_grid, vmem_limit_bytes=VMEM_LIMIT)


def _const_spec(shape):
    nd = len(shape)
    return pl.BlockSpec(shape, lambda *_: (0,) * nd, pipeline_mode=pl.Buffered(1))


def _dot(a, b):
    return jnp.dot(a, b, preferred_element_type=F32)


def _dot_nt(a, b):
    return lax.dot_general(a, b, (((1,), (1,)), ((), ())), preferred_element_type=F32)


def _iota(shape, dim):
    return lax.broadcasted_iota(jnp.int32, shape, dim)


def _lane_tiles(x):
    return [x[:, w * LANES:(w + 1) * LANES] for w in range(x.shape[1] // LANES)]


def _mod_kernel(c_ref, w_ref, b_ref, o_ref):
    c = c_ref[...]
    a = (c * jax.nn.sigmoid(c)).astype(BF16)
    o_ref[...] = _dot(a, w_ref[...].astype(BF16)) + b_ref[...]


def _modulation(c_all, w_ada, b_ada):
    m, d = c_all.shape
    n = w_ada.shape[1]
    tn = 512
    return pl.pallas_call(
        _mod_kernel,
        grid=(n // tn,),
        in_specs=[pl.BlockSpec((m, d), lambda j: (0, 0)),
                  pl.BlockSpec((d, tn), lambda j: (0, j)),
                  pl.BlockSpec((1, tn), lambda j: (0, j))],
        out_specs=pl.BlockSpec((m, tn), lambda j: (0, j)),
        out_shape=jax.ShapeDtypeStruct((m, n), F32),
        compiler_params=_cparams(1),
        name="modulation",
    )(c_all, w_ada, b_ada.reshape(1, n))


_C_HC, _C_GB, _C_GC, _C_Q = 0, CONV_DIM, 2 * CONV_DIM, 3 * CONV_DIM
_C_KVC = _C_Q + QZ_DIM
_C_KVS = _C_KVC + KV_COLS
_C_KVW = _C_KVS + KV_COLS
_C_GL = _C_KVW + KV_COLS
W_IN_COLS = _C_GL + LANES


def _inproj_kernel(x_ref, sc_ref, sh_ref, w_ref, u_ref, gb_ref, qz_ref, kvc_ref, kvs_ref, kvw_ref,
                   kvsb_ref, kvwb_ref, gates_ref):
    h = (x_ref[0] * (1.0 + sc_ref[0]) + sh_ref[0]).astype(BF16)
    hc = _dot(h, w_ref[:, _C_HC:_C_GB])
    gb_ref[0] = _dot(h, w_ref[:, _C_GB:_C_GC])
    gc = _dot(h, w_ref[:, _C_GC:_C_Q])
    u_ref[0] = gc * hc
    qz_ref[0] = _dot(h, w_ref[:, _C_Q:_C_KVC]).astype(qz_ref.dtype)
    kvc_ref[0] = _dot(h, w_ref[:, _C_KVC:_C_KVS])
    kvs = _dot(h, w_ref[:, _C_KVS:_C_KVW])
    kvs_ref[0] = kvs
    kvsb_ref[0] = kvs.astype(BF16)
    kvw = _dot(h, w_ref[:, _C_KVW:_C_GL])
    kvw_ref[0] = kvw
    kvwb_ref[0] = kvw.astype(BF16)
    gates_ref[0] = jax.nn.sigmoid(_dot(h, w_ref[:, _C_GL:W_IN_COLS]))


def _in_proj(x, scale, shift, w_cat, tm, qz_dtype):
    b, t, d = x.shape
    rm = scale.shape[1]
    mod_block = (1, tm, d) if rm == t else (1, 1, d)
    mod_map = (lambda i, j: (i, j, 0)) if rm == t else (lambda i, j: (i, 0, 0))
    row = lambda cols: pl.BlockSpec((1, tm, cols), lambda i, j: (i, j, 0))
    sds = lambda cols, dt: jax.ShapeDtypeStruct((b, t, cols), dt)
    return pl.pallas_call(
        _inproj_kernel,
        grid=(b, t // tm),
        in_specs=[row(d), pl.BlockSpec(mod_block, mod_map), pl.BlockSpec(mod_block, mod_map),
                  _const_spec((d, W_IN_COLS))],
        out_specs=[row(CONV_DIM), row(CONV_DIM), row(QZ_DIM), row(KV_COLS), row(KV_COLS), row(KV_COLS),
                   row(KV_COLS), row(KV_COLS), row(LANES)],
        out_shape=[sds(CONV_DIM, F32), sds(CONV_DIM, F32), sds(QZ_DIM, qz_dtype), sds(KV_COLS, F32),
                   sds(KV_COLS, F32), sds(KV_COLS, F32), sds(KV_COLS, BF16), sds(KV_COLS, BF16),
                   sds(LANES, F32)],
        compiler_params=_cparams(2),
        name="in_proj",
    )(x, scale, shift, w_cat)


def _gelu_tanh(x):
    return jax.nn.gelu(x, approximate=True)


def _cmp_bias_e(pe_ref, w1f_ref, b1_ref, e):
    pe = jnp.broadcast_to(pe_ref[e:e + 1, :], (8, pe_ref.shape[1])).astype(BF16)
    return _dot(pe, w1f_ref[e])[0:1, :] + b1_ref[e:e + 1, :]


def _compress_dense_kernel(x_ref, wbig_ref, pe_ref, w1f_ref, b1_ref, w2_ref, o_ref):
    half = TB_COLS // 2
    tb = _dot(x_ref[0].astype(BF16), wbig_ref[...])
    n = tb.shape[0]
    top = tb[:, :half]
    bot_next = pltpu.roll(tb[:, half:], n - 1, 0)
    bias = jnp.concatenate([_cmp_bias_e(pe_ref, w1f_ref, b1_ref, e) for e in (0, 0, 1, 1)], axis=1)
    hid = _gelu_tanh(top + bot_next + bias)
    o_ref[0] = _dot(hid.astype(BF16), w2_ref[...]).astype(o_ref.dtype)


def _compress_dense(kvc, wbig, pe2, w1f, b1, w2big):
    b, s, _ = kvc.shape
    nch = s // CMP_STRIDE
    x = kvc.reshape(b, nch, CHUNK_COLS)
    return pl.pallas_call(
        _compress_dense_kernel,
        grid=(b,),
        in_specs=[pl.BlockSpec((1, nch, CHUNK_COLS), lambda i: (i, 0, 0)),
                  _const_spec(wbig.shape), _const_spec(pe2.shape), _const_spec(w1f.shape),
                  _const_spec(b1.shape), _const_spec(w2big.shape)],
        out_specs=pl.BlockSpec((1, nch, KV_COLS), lambda i: (i, 0, 0)),
        out_shape=jax.ShapeDtypeStruct((b, nch, KV_COLS), BF16),
        compiler_params=_cparams(1),
        name="compress_prompt",
    )(x, wbig, pe2, w1f, b1, w2big)


PAGES_PER_STEP = 32


def _compress_paged_kernel(pt_ref, *refs, n_steps):
    g_pages = PAGES_PER_STEP
    page_refs = refs[:g_pages]
    xnew_ref, wpair_ref, wbot_ref, pe_ref, w1f_ref, b1_ref, w2_ref, o_ref, r_sc, x_sc = refs[g_pages:]
    kt = pl.program_id(1)
    page = page_refs[0].shape[2]
    rows_step = g_pages * (page // CMP_STRIDE)
    half_cols = N_KV * HEAD_DIM

    row0 = pl.multiple_of(kt * rows_step, rows_step)
    for e in range(2):
        for p in range(g_pages):
            r_sc[p * page:(p + 1) * page, :] = page_refs[p][0, e * half_cols:(e + 1) * half_cols, :].T
        for j in range(CMP_STRIDE):
            xj = r_sc[pl.ds(j, rows_step, stride=CMP_STRIDE), :]
            x_sc[e, j, pl.ds(row0, rows_step), :] = xj.astype(BF16)

    @pl.when(kt == n_steps - 1)
    def _():
        n = x_sc.shape[2]
        hid_cols = N_KV * CMP_HID
        xn = jnp.broadcast_to(xnew_ref[0], (8, CHUNK_COLS)).astype(BF16)
        bot_new = _dot(xn, wbot_ref[...])[0:1, :]
        last = _iota((n, 1), 0) == n - 1
        outs = []
        for e in range(2):
            acc = jnp.zeros((n, 2 * hid_cols), F32)
            for jp in range(CMP_STRIDE // 2):
                xs = jnp.concatenate([x_sc[e, 2 * jp], x_sc[e, 2 * jp + 1]], axis=1)
                acc = acc + _dot(xs, wpair_ref[e, jp])
            top, bot = acc[:, :hid_cols], acc[:, hid_cols:]
            bot_next = jnp.where(last, bot_new[:, e * hid_cols:(e + 1) * hid_cols], pltpu.roll(bot, n - 1, 0))
            bias = jnp.concatenate([_cmp_bias_e(pe_ref, w1f_ref, b1_ref, e)] * N_KV, axis=1)
            hid = _gelu_tanh(top + bot_next + bias)
            outs.append(_dot(hid.astype(BF16), w2_ref[e]))
        o_ref[0] = jnp.concatenate(outs, axis=1).astype(o_ref.dtype)


def _compress_paged(pool_t, page_table, xnew, wpair, wbot, pe2, w1f, b1, w2pair):
    bn, n_pages = page_table.shape
    page = pool_t.shape[2]
    g = PAGES_PER_STEP
    n_steps = n_pages // g
    assert n_pages % g == 0 and page % CMP_STRIDE == 0 and page == LANES
    n = n_pages * page // CMP_STRIDE

    def page_spec(k):
        return pl.BlockSpec((1, KV_COLS, page), lambda i, j, pt: (pt[i, j * g + k], 0, 0))

    def const(shape):
        nd = len(shape)
        return pl.BlockSpec(shape, lambda i, j, pt: (0,) * nd, pipeline_mode=pl.Buffered(1))

    grid_spec = pltpu.PrefetchScalarGridSpec(
        num_scalar_prefetch=1,
        grid=(bn, n_steps),
        in_specs=[page_spec(k) for k in range(g)]
        + [pl.BlockSpec((1, 1, CHUNK_COLS), lambda i, j, pt: (i, 0, 0)), const(wpair.shape), const(wbot.shape),
           const(pe2.shape), const(w1f.shape), const(b1.shape), const(w2pair.shape)],
        out_specs=pl.BlockSpec((1, n, KV_COLS), lambda i, j, pt: (i, 0, 0)),
        scratch_shapes=[pltpu.VMEM((g * page, LANES), F32),
                        pltpu.VMEM((2, CMP_STRIDE, n, LANES), BF16)],
    )
    return pl.pallas_call(
        functools.partial(_compress_paged_kernel, n_steps=n_steps),
        grid_spec=grid_spec,
        out_shape=jax.ShapeDtypeStruct((bn, n, KV_COLS), BF16),
        compiler_params=_cparams(2),
        name="compress_sample",
    )(page_table, *([pool_t] * g), xnew, wpair, wbot, pe2, w1f, b1, w2pair)


def _split3_dot(x, m01):
    h1 = x.astype(BF16)
    r1 = x - h1.astype(F32)
    h2 = r1.astype(BF16)
    h3 = (r1 - h2.astype(F32)).astype(BF16)
    return _dot(h1, m01) + _dot(h2, m01) + _dot(h3, m01)


def _block_scores(imp, n_lanes):
    ncp = imp.shape[1]
    c = _iota((ncp, n_lanes), 0)
    j = _iota((ncp, n_lanes), 1)
    a = jnp.where((c >= CMP_PER_SEL * j - 1) & (c <= CMP_PER_SEL * j + CMP_PER_SEL - 1), 1.0, 0.0).astype(BF16)
    return _split3_dot(imp, a)


def _ranked_scores(score, qpos, n_blocks):
    j = _iota(score.shape, 1)
    cur = qpos >> 6
    valid = j * SEL_BLOCK <= qpos
    forced = (j == 0) | (j == cur) | (j == cur - 1)
    return jnp.where(forced, -NEG, jnp.where(valid, score, NEG))


def _select_blocks_iter(score, qpos, n_blocks, n_select):
    r, l = score.shape
    removed = 3.0 * NEG
    jf = _iota((r, l), 1).astype(F32)
    sc = jnp.where(jf < n_blocks, _ranked_scores(score, qpos, n_blocks), removed)

    def body(_, carry):
        sc, sel = carry
        mx = jnp.max(sc, axis=-1, keepdims=True)
        first = jnp.min(jnp.where(sc == mx, jf, 1e9), axis=-1, keepdims=True)
        pick = jf == first
        return jnp.where(pick, removed, sc), jnp.where(pick, 1.0, sel)

    _, sel = lax.fori_loop(0, n_select, body, (sc, jnp.zeros((r, l), F32)))
    return sel


def _select_blocks_rank(score, qpos, n_blocks, n_select):
    r, l = score.shape
    nb8 = -(-n_blocks // 8) * 8
    st = _ranked_scores(score, qpos, n_blocks).T[:nb8]
    jrow = _iota((nb8, r), 0)
    rank = jnp.zeros((nb8, r), F32)
    for jp in range(n_blocks):
        row = st[jp:jp + 1, :]
        beats = (row > st) | ((row == st) & (jrow > jp))
        rank = rank + jnp.where(beats, 1.0, 0.0)
    sel_t = jnp.where((rank < n_select) & (jrow < n_blocks), 1.0, 0.0)
    if nb8 < l:
        sel_t = jnp.concatenate([sel_t, jnp.zeros((l - nb8, r), F32)], axis=0)
    return sel_t.T


def _softmax_parts(s, mask):
    s = jnp.where(mask, s, NEG)
    m = jnp.max(s, axis=-1, keepdims=True)
    e = jnp.where(mask, jnp.exp(s - m), 0.0)
    return e, jnp.sum(e, axis=-1, keepdims=True)


SEL_TILE = 512


def _attn_prompt_kernel(qz_ref, kvs_ref, kvw_ref, kvc_ref, gates_ref, exp_ref, o_ref,
                        s_sc, p_sc, pw_sc, m_sc, l_sc, acc_sc, *, seq):
    qb = Q_BLOCK
    tk = SEL_TILE
    start = pl.program_id(1) * qb
    qpos = start + _iota((qb, 1), 0)
    ncp = kvc_ref.shape[1]
    n_blocks = -(-seq // SEL_BLOCK)
    rows = Q_PER_KV * qb
    gates = gates_ref[0]
    n_tiles = (start + qb + tk - 1) // tk
    head_rows = [slice(r * qb, (r + 1) * qb) for r in range(Q_PER_KV)]

    for g in range(N_KV):
        q_g = jnp.concatenate(
            [qz_ref[0, :, (g * Q_PER_KV + r) * LANES:(g * Q_PER_KV + r + 1) * LANES] for r in range(Q_PER_KV)],
            axis=0)

        s = _dot_nt(q_g, kvc_ref[0, :, 0:LANES]).reshape(Q_PER_KV, qb, ncp)
        c = _iota((qb, ncp), 1)
        cmask = (CMP_STRIDE * c + (CMP_BLOCK - 1) <= qpos) & (c < ncp - 1)
        e, l = _softmax_parts(s, cmask)
        p = e / jnp.maximum(l, 1e-30)
        o_cmp = _dot(p.reshape(rows, ncp).astype(BF16), kvc_ref[0, :, LANES:2 * LANES])
        imp = jnp.sum(p, axis=0)
        sel = _select_blocks_rank(_block_scores(imp, LANES), qpos, n_blocks, min(N_SELECT, n_blocks)).astype(BF16)

        m_sc[...] = jnp.full(m_sc.shape, NEG, F32)

        def pass1(kt, _):
            k0 = pl.multiple_of(kt * tk, tk)
            s = _dot_nt(q_g, kvs_ref[0, pl.ds(k0, tk), 0:LANES])
            kpos = k0 + _iota((1, tk), 1)
            mask = (_dot(sel, exp_ref[kt]) > 0.5) & (kpos <= qpos)
            for rs in head_rows:
                sr = jnp.where(mask, s[rs], NEG)
                s_sc[kt, rs, :] = sr
                m_sc[rs, :] = jnp.maximum(m_sc[rs, :], functools.reduce(jnp.maximum, _lane_tiles(sr)))
            return 0

        lax.fori_loop(0, n_tiles, pass1, 0)
        m_sc[...] = jnp.broadcast_to(jnp.max(m_sc[...], axis=-1, keepdims=True), m_sc.shape)
        l_sc[...] = jnp.zeros(l_sc.shape, F32)
        acc_sc[...] = jnp.zeros(acc_sc.shape, F32)

        def pass2(kt, _):
            k0 = pl.multiple_of(kt * tk, tk)
            for rs in head_rows:
                e = jnp.exp(s_sc[kt, rs, :] - jnp.concatenate([m_sc[rs, :]] * (tk // LANES), axis=1))
                l_sc[rs, :] = l_sc[rs, :] + functools.reduce(jnp.add, _lane_tiles(e))
                p_sc[rs, :] = e.astype(BF16)
            acc_sc[...] = acc_sc[...] + _dot(p_sc[...], kvs_ref[0, pl.ds(k0, tk), LANES:2 * LANES])
            return 0

        lax.fori_loop(0, n_tiles, pass2, 0)
        o_slc = acc_sc[...] / jnp.maximum(jnp.sum(l_sc[...], axis=-1, keepdims=True), 1e-30)

        wlen = WINDOW + qb
        w0 = pl.multiple_of(jnp.maximum(start - WINDOW, 0), qb)
        s = _dot_nt(q_g, kvw_ref[0, pl.ds(w0, wlen), 0:LANES])
        kpos = w0 + _iota((1, wlen), 1)
        wmask = (kpos <= qpos) & (kpos > qpos - WINDOW)
        lw = []
        for rs in head_rows:
            sr = jnp.where(wmask, s[rs], NEG)
            m = jnp.max(functools.reduce(jnp.maximum, _lane_tiles(sr)), axis=-1, keepdims=True)
            e = jnp.exp(sr - m)
            lw.append(jnp.sum(functools.reduce(jnp.add, _lane_tiles(e)), axis=-1, keepdims=True))
            pw_sc[rs, :] = e.astype(BF16)
        o_win = _dot(pw_sc[...], kvw_ref[0, pl.ds(w0, wlen), LANES:2 * LANES])

        own = (_iota((qb, LANES), 1) >> 6) == g
        for r, rs in enumerate(head_rows):
            h = g * Q_PER_KV + r
            o = (gates[:, h:h + 1] * o_cmp[rs] + gates[:, N_HEADS + h:N_HEADS + h + 1] * o_slc[rs]
                 + gates[:, 2 * N_HEADS + h:2 * N_HEADS + h + 1] * (o_win[rs] / jnp.maximum(lw[r], 1e-30)))
            o_ref[0, :, h * LANES:(h + 1) * LANES] = jnp.where(own, o, 0.0)


def _expand_matrix(n_tiles, tile):
    t = jnp.arange(n_tiles)[:, None, None]
    j = jnp.arange(LANES)[None, :, None]
    k = jnp.arange(tile)[None, None, :]
    return (j == (t * tile + k) // SEL_BLOCK).astype(BF16)


def _attn_prompt(qz, kvs_b, kvw_b, kv_cmpr, gates):
    b, s, _ = qz.shape
    assert s % Q_BLOCK == 0 and s % SEL_TILE == 0 and s >= WINDOW + Q_BLOCK and s // SEL_BLOCK <= LANES
    ncp = kv_cmpr.shape[1]
    n_tiles = s // SEL_TILE
    expand = _expand_matrix(n_tiles, SEL_TILE)
    rows = Q_PER_KV * Q_BLOCK
    full = lambda n: pl.BlockSpec((1, n, KV_COLS), lambda i, j: (i, 0, 0))
    return pl.pallas_call(
        functools.partial(_attn_prompt_kernel, seq=s),
        grid=(b, s // Q_BLOCK),
        in_specs=[pl.BlockSpec((1, Q_BLOCK, QZ_DIM), lambda i, j: (i, j, 0)), full(s), full(s), full(ncp),
                  pl.BlockSpec((1, Q_BLOCK, LANES), lambda i, j: (i, j, 0)), _const_spec(expand.shape)],
        out_specs=pl.BlockSpec((1, Q_BLOCK, QZ_DIM), lambda i, j: (i, j, 0)),
        out_shape=jax.ShapeDtypeStruct((b, s, QZ_DIM), F32),
        scratch_shapes=[pltpu.VMEM((n_tiles, rows, SEL_TILE), F32), pltpu.VMEM((rows, SEL_TILE), BF16),
                        pltpu.VMEM((rows, WINDOW + Q_BLOCK), BF16), pltpu.VMEM((rows, LANES), F32),
                        pltpu.VMEM((rows, LANES), F32), pltpu.VMEM((rows, LANES), F32)],
        compiler_params=_cparams(2),
        name="attn_prompt",
    )(qz, kvs_b, kvw_b, kv_cmpr, gates, expand)


def _attn_sample_kernel(pt_ref, *refs, past, n_steps):
    g_pages = PAGES_PER_STEP
    page_refs = refs[:g_pages]
    (qz_ref, kvc_ref, gates_ref, kvs_new_ref, win_ref, kvw_new_ref, exp_ref,
     o_ref, q_sc, sel_sc, m_sc, l_sc, acc_sc, ocw_sc) = refs[g_pages:]
    kt = pl.program_id(1)
    t = qz_ref.shape[1]
    rows = N_HEADS * t
    page = page_refs[0].shape[2]
    tile = g_pages * page
    blocks_per_tile = tile // SEL_BLOCK
    n_past_blocks = past // SEL_BLOCK
    n_blocks = n_past_blocks + -(-t // SEL_BLOCK)
    sel_lanes = sel_sc.shape[0] * LANES
    half_cols = N_KV * HEAD_DIM
    qpos = past + _iota((t, 1), 0)
    qpos_gt = jnp.concatenate([qpos] * N_KV, axis=0)
    own_rows = _iota((N_KV, Q_PER_KV, t, LANES), 0) == (_iota((N_KV, Q_PER_KV, t, LANES), 3) >> 6)

    def online_update(s, mask, pv_fn):
        k = s.shape[1]
        s = jnp.where(mask, s.reshape(N_KV, Q_PER_KV, t, k), NEG)
        m_prev = m_sc[...]
        m_new = jnp.maximum(m_prev, jnp.max(s, axis=-1, keepdims=True))
        alpha = jnp.exp(m_prev - m_new)
        e = jnp.where(mask, jnp.exp(s - m_new), 0.0)
        l_sc[...] = alpha * l_sc[...] + jnp.sum(e, axis=-1, keepdims=True)
        pv = pv_fn(e.reshape(rows, k).astype(BF16)).reshape(N_KV, Q_PER_KV, t, LANES)
        acc_sc[...] = alpha * acc_sc[...] + pv
        m_sc[...] = m_new

    @pl.when(kt == 0)
    def _():
        q = jnp.concatenate([qz_ref[0, :, h * LANES:(h + 1) * LANES] for h in range(N_HEADS)], axis=0)
        q_sc[...] = q.astype(BF16)
        qb = q_sc[...]
        ncp = kvc_ref.shape[1]
        s = _dot_nt(qb, kvc_ref[0, :, 0:LANES]).reshape(N_HEADS, t, ncp)
        c = _iota((t, ncp), 1)
        cmask = CMP_STRIDE * c + (CMP_BLOCK - 1) <= qpos
        e, l = _softmax_parts(s, cmask)
        p = e / jnp.maximum(l, 1e-30)
        o_cmp = _dot(p.reshape(rows, ncp).astype(BF16), kvc_ref[0, :, LANES:2 * LANES])
        imp = jnp.sum(p.reshape(N_KV, Q_PER_KV, t, ncp), axis=1).reshape(N_KV * t, ncp)
        sel = _select_blocks_iter(_block_scores(imp, sel_lanes), qpos_gt, n_blocks, min(N_SELECT, n_blocks))
        for w in range(sel_sc.shape[0]):
            sel_sc[w] = sel[:, w * LANES:(w + 1) * LANES]
        wb = win_ref.shape[2]
        win = win_ref[0].astype(BF16)
        kvw_new = kvw_new_ref[0].astype(BF16)
        s_old = _dot(qb, win[0:half_cols]).reshape(N_HEADS, t, wb)
        s_new = _dot_nt(qb, kvw_new[:, 0:half_cols]).reshape(N_HEADS, t, t)
        kpos_old = past - wb + _iota((1, wb), 1)
        kpos_new = past + _iota((1, t), 1)
        mask_old = (kpos_old <= qpos) & (kpos_old > qpos - WINDOW) & (kpos_old >= 0)
        mask_new = (kpos_new <= qpos) & (kpos_new > qpos - WINDOW)
        s_old = jnp.where(mask_old, s_old, NEG)
        s_new = jnp.where(mask_new, s_new, NEG)
        m = jnp.maximum(jnp.max(s_old, axis=-1, keepdims=True), jnp.max(s_new, axis=-1, keepdims=True))
        e_old = jnp.where(mask_old, jnp.exp(s_old - m), 0.0)
        e_new = jnp.where(mask_new, jnp.exp(s_new - m), 0.0)
        l = jnp.sum(e_old, axis=-1, keepdims=True) + jnp.sum(e_new, axis=-1, keepdims=True)
        o_win = (_dot_nt(e_old.reshape(rows, wb).astype(BF16), win[half_cols:])
                 + _dot(e_new.reshape(rows, t).astype(BF16), kvw_new[:, half_cols:]))
        o_win = o_win.reshape(N_HEADS, t, LANES) / jnp.maximum(l, 1e-30)
        gts = gates_ref[0]
        gc = jnp.stack([jnp.broadcast_to(gts[:, h:h + 1], (t, LANES)) for h in range(N_HEADS)])
        gw = jnp.stack([jnp.broadcast_to(gts[:, 2 * N_HEADS + h:2 * N_HEADS + h + 1], (t, LANES))
                        for h in range(N_HEADS)])
        ocw_sc[...] = gc * o_cmp.reshape(N_HEADS, t, LANES) + gw * o_win
        m_sc[...] = jnp.full(m_sc.shape, NEG, F32)
        l_sc[...] = jnp.zeros(l_sc.shape, F32)
        acc_sc[...] = jnp.zeros(acc_sc.shape, F32)
        kvn = kvs_new_ref[0].astype(BF16)
        jn = kpos_new >> 6
        seln = jnp.zeros((N_KV * t, t), F32)
        for jb in range(n_past_blocks, n_blocks):
            seln = jnp.where(jn == jb, sel[:, jb:jb + 1], seln)
        nmask = ((seln > 0.5) & (kpos_new <= qpos_gt)).reshape(N_KV, 1, t, t)
        online_update(_dot_nt(qb, kvn[:, 0:half_cols]), nmask, lambda pb: _dot(pb, kvn[:, half_cols:]))

    k_t = jnp.concatenate([r[0, 0:half_cols, :] for r in page_refs], axis=1).astype(BF16)
    v_t = jnp.concatenate([r[0, half_cols:, :] for r in page_refs], axis=1).astype(BF16)
    b0 = kt * blocks_per_tile
    selw = sel_sc[b0 // LANES].astype(BF16)
    kpos = kt * tile + _iota((1, tile), 1)
    mask = ((_dot(selw, exp_ref[0]) > 0.5) & (kpos <= qpos_gt)).reshape(N_KV, 1, t, tile)
    online_update(_dot(q_sc[...], k_t), mask, lambda pb: _dot_nt(pb, v_t))

    @pl.when(kt == n_steps - 1)
    def _():
        gts = gates_ref[0]
        gs = jnp.stack([jnp.broadcast_to(gts[:, N_HEADS + h:N_HEADS + h + 1], (t, LANES))
                        for h in range(N_HEADS)])
        o_slc = (acc_sc[...] / jnp.maximum(l_sc[...], 1e-30)).reshape(N_HEADS, t, LANES)
        o = jnp.where(own_rows.reshape(N_HEADS, t, LANES), ocw_sc[...] + gs * o_slc, 0.0)
        for h in range(N_HEADS):
            o_ref[0, :, h * LANES:(h + 1) * LANES] = o[h]


def _attn_sample(qz, kv_cmpr, gates, kvs_new, slc_pool_t, win_t, kvw_new, page_table):
    bn, t, _ = qz.shape
    n_pages = page_table.shape[1]
    page = slc_pool_t.shape[2]
    past = n_pages * page
    g = PAGES_PER_STEP
    n_steps = n_pages // g
    tile = g * page
    blocks_per_tile = tile // SEL_BLOCK
    assert n_pages % g == 0 and page % SEL_BLOCK == 0 and LANES % blocks_per_tile == 0 and t % 8 == 0
    n_blocks = past // SEL_BLOCK + -(-t // SEL_BLOCK)
    sel_groups = -(-n_blocks // LANES)
    ncp = kv_cmpr.shape[1]
    wb = win_t.shape[2]
    n_var = LANES // blocks_per_tile
    expand = _expand_matrix(n_var, tile)

    def page_spec(k):
        return pl.BlockSpec((1, KV_COLS, page), lambda i, j, pt: (pt[i, j * g + k], 0, 0))

    per_b = lambda n, cols: pl.BlockSpec((1, n, cols), lambda i, j, pt: (i, 0, 0))
    grid_spec = pltpu.PrefetchScalarGridSpec(
        num_scalar_prefetch=1,
        grid=(bn, n_steps),
        in_specs=[page_spec(k) for k in range(g)]
        + [per_b(t, QZ_DIM), per_b(ncp, KV_COLS), per_b(t, LANES), per_b(t, KV_COLS), per_b(KV_COLS, wb),
           per_b(t, KV_COLS), pl.BlockSpec((1, LANES, tile), lambda i, j, pt: (j % n_var, 0, 0))],
        out_specs=per_b(t, QZ_DIM),
        scratch_shapes=[pltpu.VMEM((N_HEADS * t, LANES), BF16),
                        pltpu.VMEM((sel_groups, N_KV * t, LANES), F32),
                        pltpu.VMEM((N_KV, Q_PER_KV, t, 1), F32), pltpu.VMEM((N_KV, Q_PER_KV, t, 1), F32),
                        pltpu.VMEM((N_KV, Q_PER_KV, t, LANES), F32),
                        pltpu.VMEM((N_HEADS, t, LANES), F32)],
    )
    return pl.pallas_call(
        functools.partial(_attn_sample_kernel, past=past, n_steps=n_steps),
        grid_spec=grid_spec,
        out_shape=jax.ShapeDtypeStruct((bn, t, QZ_DIM), F32),
        compiler_params=_cparams(2),
        name="attn_sample",
    )(page_table, *([slc_pool_t] * g), qz, kv_cmpr, gates, kvs_new, win_t, kvw_new, expand)


def _layernorm(x, g, b):
    mu = jnp.mean(x, axis=-1, keepdims=True)
    xc = x - mu
    var = jnp.mean(xc * xc, axis=-1, keepdims=True)
    return xc * lax.rsqrt(var + EPS) * g + b


def _post_kernel(x_ref, o_ref, u_ref, uprev_ref, gb_ref, st1_ref, st2_ref, g1_ref, sh2_ref, sc2_ref, g2_ref,
                 convw_ref, gconv_ref, gnsa_ref, woc_ref, won_ref, ln1g_ref, ln1b_ref, ln2g_ref, ln2b_ref,
                 wg_ref, wu_ref, wd_ref, y_ref, *, seq):
    tm = x_ref.shape[1]
    u = u_ref[0]
    ext = jnp.concatenate([uprev_ref[0], u], axis=0)
    if tm <= seq:
        pos = (pl.program_id(1) * tm) % seq + _iota((tm, 1), 0)
    else:
        pos = lax.rem(_iota((tm, 1), 0), seq)
    p1 = jnp.where(pos >= 1, ext[7:7 + tm], st1_ref[0])
    p2 = jnp.where(pos >= 2, ext[6:6 + tm], st2_ref[0])
    cw = convw_ref[...]
    y_c = gb_ref[0] * (cw[0:1] * p2 + cw[1:2] * p1 + cw[2:3] * u)
    yn = y_c * lax.rsqrt(jnp.mean(y_c * y_c, axis=-1, keepdims=True) + EPS) * gconv_ref[...]
    o = o_ref[0]
    on = o * lax.rsqrt(jnp.sum(o * o, axis=-1, keepdims=True) * (1.0 / NSA_DIM) + EPS) * gnsa_ref[...]
    mix = _dot(yn.astype(BF16), woc_ref[...]) + _dot(on.astype(BF16), won_ref[...])
    x1 = _layernorm(ALPHA * x_ref[0] + g1_ref[0] * mix, ln1g_ref[...], ln1b_ref[...])
    h = (x1 * (1.0 + sc2_ref[0]) + sh2_ref[0]).astype(BF16)
    a = _dot(h, wg_ref[...])
    f = (a * jax.nn.sigmoid(a)) * _dot(h, wu_ref[...])
    f = _dot(f.astype(BF16), wd_ref[...])
    y_ref[0] = _layernorm(ALPHA * x1 + g2_ref[0] * f, ln2g_ref[...], ln2b_ref[...])


def _post(x, o, u, gb, st1, st2, mods, w, tm, seq):
    b, t, d = x.shape
    row = lambda cols: pl.BlockSpec((1, tm, cols), lambda i, j: (i, j, 0))

    def bcast(a):
        if a.shape[1] == t:
            return pl.BlockSpec((1, tm, a.shape[2]), lambda i, j: (i, j, 0))
        return pl.BlockSpec((1, 1, a.shape[2]), lambda i, j: (i, 0, 0))

    prev = pl.BlockSpec((1, 8, CONV_DIM), lambda i, j: (i, jnp.maximum(j * (tm // 8) - 1, 0), 0))
    consts = [w["conv_w"], w["g_conv"], w["g_nsa"], w["wo_c"], w["wo_n"], w["ln1_g"], w["ln1_b"], w["ln2_g"],
              w["ln2_b"], w["w_gate"], w["w_up"], w["w_down"]]
    return pl.pallas_call(
        functools.partial(_post_kernel, seq=seq),
        grid=(b, t // tm),
        in_specs=[row(d), row(QZ_DIM), row(CONV_DIM), prev, row(CONV_DIM), bcast(st1), bcast(st2)]
        + [bcast(m) for m in mods] + [_const_spec(c.shape) for c in consts],
        out_specs=row(d),
        out_shape=jax.ShapeDtypeStruct((b, t, d), F32),
        compiler_params=_cparams(2),
        name="post_block",
    )(x, o, u, u, gb, st1, st2, *mods, *consts)


def _prep_w_in(w_in):
    d = w_in.shape[0]
    c3 = 3 * CONV_DIM
    splits = (CONV_DIM, 2 * CONV_DIM, c3, c3 + NSA_DIM, c3 + NSA_DIM + KV_COLS, c3 + NSA_DIM + 2 * KV_COLS,
              c3 + NSA_DIM + 3 * KV_COLS)
    hc, gb, gc, q, kvc, kvs, kvw, gl = jnp.split(w_in, splits, axis=1)
    q4 = q.reshape(d, N_KV, Q_PER_KV, HEAD_DIM) * SCALE
    qz = jnp.einsum("dgrh,gk->dgrkh", q4, jnp.eye(N_KV, dtype=w_in.dtype)).reshape(d, QZ_DIM)
    glp = jnp.pad(gl, ((0, 0), (0, LANES - gl.shape[1])))
    return jnp.concatenate([hc, gb, gc, qz, kvc, kvs, kvw, glp], axis=1).astype(BF16)


def _prep_cmp(cmp_pe, cmp_w1, cmp_b1, cmp_w2):
    eye = jnp.eye(N_KV, dtype=cmp_w1.dtype)
    eye2 = jnp.eye(2, dtype=cmp_w1.dtype)
    w1r = cmp_w1.reshape(2, 2, CMP_STRIDE, HEAD_DIM, CMP_HID)
    wbig = jnp.einsum("etjdh,ef,gk->jegdtfkh", w1r, eye2, eye).reshape(CHUNK_COLS, TB_COLS).astype(BF16)
    w2big = jnp.einsum("ehd,ef,gk->eghfkd", cmp_w2, eye2, eye).reshape(TB_COLS // 2, KV_COLS).astype(BF16)
    w1p = cmp_w1.reshape(2, 2, CMP_STRIDE // 2, 2, HEAD_DIM, CMP_HID)
    wpair = jnp.einsum("etpjdh,gk->epjgdtkh", w1p, eye).reshape(
        2, CMP_STRIDE // 2, 2 * N_KV * HEAD_DIM, 2 * N_KV * CMP_HID).astype(BF16)
    w2pair = jnp.einsum("ehd,gk->eghkd", cmp_w2, eye).reshape(2, N_KV * CMP_HID, N_KV * HEAD_DIM).astype(BF16)
    pe2 = cmp_pe.reshape(2, CMP_BLOCK * HEAD_DIM)
    w1f = cmp_w1.reshape(2, CMP_BLOCK * HEAD_DIM, CMP_HID).astype(BF16)
    dense = (wbig, pe2, w1f, cmp_b1, w2big)
    paged = (wpair, wbig[:, TB_COLS // 2:], pe2, w1f, cmp_b1, w2pair)
    return dense, paged


def _prep_post(conv_w, g_conv, g_nsa, w_o, ln1_g, ln1_b, ln2_g, ln2_b, w_gate, w_up, w_down):
    d = w_o.shape[1]
    own = jnp.eye(N_KV, dtype=w_o.dtype)
    g_nsa_z = jnp.einsum("grh,gk->grkh", g_nsa.reshape(N_KV, Q_PER_KV, HEAD_DIM), own).reshape(1, QZ_DIM)
    wo_n = jnp.einsum("grhd,gk->grkhd", w_o[CONV_DIM:].reshape(N_KV, Q_PER_KV, HEAD_DIM, d), own)
    row = lambda a: a.reshape(1, -1)
    return dict(conv_w=conv_w, g_conv=row(g_conv), g_nsa=g_nsa_z, wo_c=w_o[:CONV_DIM].astype(BF16),
                wo_n=wo_n.reshape(QZ_DIM, d).astype(BF16), ln1_g=row(ln1_g), ln1_b=row(ln1_b),
                ln2_g=row(ln2_g), ln2_b=row(ln2_b), w_gate=w_gate.astype(BF16), w_up=w_up.astype(BF16),
                w_down=w_down.astype(BF16))


def _feature_major(cache):
    n, rows = cache.shape[:2]
    return jnp.transpose(cache, (0, 2, 3, 4, 1)).reshape(n, KV_COLS, rows)


def kernel(x_prompt, x_sample, cache_cmp_kv, cache_slc_kv, cache_win_kv, state_conv, page_table, c_prompt,
           c_sample, w_ada, b_ada, w_in, conv_w, cmp_pe, cmp_w1, cmp_b1, cmp_w2, g_conv_out, g_nsa_out, w_o,
           ln1_g, ln1_b, ln2_g, ln2_b, w_ffn_gate, w_ffn_up, w_ffn_down):
    assert w_ada.shape[0] == DEPTH
    bp, s, d = x_prompt.shape
    bs, t, _ = x_sample.shape
    kv_shape = (2, N_KV, HEAD_DIM)

    w_cat = _prep_w_in(w_in[0])
    cmp_dense_w, cmp_paged_w = _prep_cmp(cmp_pe[0], cmp_w1[0], cmp_b1[0], cmp_w2[0])
    post_w = _prep_post(conv_w[0], g_conv_out[0], g_nsa_out[0], w_o[0], ln1_g[0], ln1_b[0], ln2_g[0], ln2_b[0],
                        w_ffn_gate[0], w_ffn_up[0], w_ffn_down[0])

    mod = _modulation(jnp.concatenate([c_prompt, c_sample], axis=0), w_ada[0], b_ada[0])
    mods_p = [m[:, None, :] for m in jnp.split(mod[:bp], 6, axis=-1)]
    mods_s = [jnp.repeat(m, t, axis=0)[None] for m in jnp.split(mod[bp:], 6, axis=-1)]

    u, gb, qz, kvc, kvs, kvw, kvs_b, kvw_b, gates = _in_proj(x_prompt, mods_p[1], mods_p[0], w_cat, 512, BF16)
    kv_cmpr = _compress_dense(kvc, *cmp_dense_w)
    o = _attn_prompt(qz, kvs_b, kvw_b, kv_cmpr, gates)
    zero_state = jnp.zeros((bp, 1, CONV_DIM), F32)
    y_prompt = _post(x_prompt, o, u, gb, zero_state, zero_state, mods_p[2:], post_w, 256, s)
    w_keep = min(WINDOW, s)
    cmp_p = kvc.reshape(1, bp, s, *kv_shape)
    slc_p = kvs.reshape(1, bp, s, *kv_shape)
    win_p = kvw[:, s - w_keep:].reshape(1, bp, w_keep, *kv_shape)
    conv_p = u[:, s - (CONV_W - 1):][None]

    rows = bs * t
    us, gbs, qzs, kvcs, kvss, kvws, _, _, gates_s = _in_proj(
        x_sample.reshape(1, rows, d), mods_s[1], mods_s[0], w_cat, rows, F32)
    per_b = lambda a: a.reshape(bs, t, a.shape[-1])
    kvcs, kvss, kvws = per_b(kvcs), per_b(kvss), per_b(kvws)
    xnew = jnp.pad(kvcs, ((0, 0), (0, (-t) % CMP_STRIDE), (0, 0))).reshape(bs, 1, CHUNK_COLS)
    kv_cmpr_s = _compress_paged(_feature_major(cache_cmp_kv[0]), page_table, xnew, *cmp_paged_w)
    o_s = _attn_sample(per_b(qzs), kv_cmpr_s, per_b(gates_s), kvss, _feature_major(cache_slc_kv[0]),
                       _feature_major(cache_win_kv[0]), kvws, page_table)
    st = state_conv[0]
    tpos = jnp.arange(t)[None, :, None]
    st1 = jnp.where(tpos == 0, st[:, 1:2], 0.0).reshape(1, rows, CONV_DIM)
    st2 = jnp.where(tpos == 0, st[:, 0:1], jnp.where(tpos == 1, st[:, 1:2], 0.0)).reshape(1, rows, CONV_DIM)
    y_sample = _post(x_sample.reshape(1, rows, d), o_s.reshape(1, rows, QZ_DIM), us, gbs, st1, st2, mods_s[2:],
                     post_w, rows, t).reshape(bs, t, d)
    cmp_s = kvcs.reshape(1, bs, t, *kv_shape)
    slc_s = kvss.reshape(1, bs, t, *kv_shape)
    win_all = jnp.concatenate([cache_win_kv[0], kvws.reshape(bs, t, *kv_shape)], axis=1)
    win_s = win_all[:, t:][None]
    conv_s = jnp.concatenate([st, per_b(us)], axis=1)[:, t:][None]
    return (y_prompt, y_sample, cmp_p, slc_p, win_p, conv_p, cmp_s, slc_s, win_s, conv_s)
```

```python
import functools

import jax
import jax.numpy as jnp
from jax import lax
from jax.experimental import pallas as pl
from jax.experimental.pallas import tpu as pltpu

HEAD_DIM = 64
N_KV = 2
Q_PER_KV = 6
N_HEADS = N_KV * Q_PER_KV
CONV_DIM = 256
CONV_W = 3
NSA_DIM = N_HEADS * HEAD_DIM
KV_COLS = 2 * N_KV * HEAD_DIM
CMP_BLOCK = 32
CMP_STRIDE = 16
CMP_HID = 2 * HEAD_DIM
SEL_BLOCK = 64
CMP_PER_SEL = SEL_BLOCK // CMP_STRIDE
N_SELECT = 16
WINDOW = 512
Q_BLOCK = 128
DEPTH = 1
ALPHA = (2 * DEPTH) ** 0.25
EPS = 1e-5
SCALE = HEAD_DIM ** -0.5

LANES = 128
QZ_DIM = N_HEADS * LANES
CHUNK_COLS = CMP_STRIDE * KV_COLS
TB_COLS = 2 * 2 * N_KV * CMP_HID
NEG = -1e30
VMEM_LIMIT = 56 * 1024 * 1024

F32 = jnp.float32
BF16 = jnp.bfloat16


def _cparams(n_grid):
    return pltpu.CompilerParams(dimension_semantics=("arbitrary",) * n_grid, vmem_limit_bytes=VMEM_LIMIT)


def _const_spec(shape):
    nd = len(shape)
    return pl.BlockSpec(shape, lambda *_: (0,) * nd, pipeline_mode=pl.Buffered(1))


def _dot(a, b):
    return jnp.dot(a, b, preferred_element_type=F32)


def _dot_nt(a, b):
    return lax.dot_general(a, b, (((1,), (1,)), ((), ())), preferred_element_type=F32)


def _iota(shape, dim):
    return lax.broadcasted_iota(jnp.int32, shape, dim)


def _lane_tiles(x):
    return [x[:, w * LANES:(w + 1) * LANES] for w in range(x.shape[1] // LANES)]


def _mod_kernel(c_ref, w_ref, b_ref, o_ref):
    c = c_ref[...]
    a = (c * jax.nn.sigmoid(c)).astype(BF16)
    o_ref[...] = _dot(a, w_ref[...].astype(BF16)) + b_ref[...]


def _modulation(c_all, w_ada, b_ada):
    m, d = c_all.shape
    n = w_ada.shape[1]
    tn = 512
    return pl.pallas_call(
        _mod_kernel,
        grid=(n // tn,),
        in_specs=[pl.BlockSpec((m, d), lambda j: (0, 0)),
                  pl.BlockSpec((d, tn), lambda j: (0, j)),
                  pl.BlockSpec((1, tn), lambda j: (0, j))],
        out_specs=pl.BlockSpec((m, tn), lambda j: (0, j)),
        out_shape=jax.ShapeDtypeStruct((m, n), F32),
        compiler_params=_cparams(1),
        name="modulation",
    )(c_all, w_ada, b_ada.reshape(1, n))


_C_HC, _C_GB, _C_GC, _C_Q = 0, CONV_DIM, 2 * CONV_DIM, 3 * CONV_DIM
_C_KVC = _C_Q + QZ_DIM
_C_KVS = _C_KVC + KV_COLS
_C_KVW = _C_KVS + KV_COLS
_C_GL = _C_KVW + KV_COLS
W_IN_COLS = _C_GL + LANES


def _inproj_kernel(x_ref, sc_ref, sh_ref, w_ref, u_ref, gb_ref, qz_ref, kvc_ref, kvs_ref, kvw_ref,
                   kvsb_ref, kvwb_ref, gates_ref):
    h = (x_ref[0] * (1.0 + sc_ref[0]) + sh_ref[0]).astype(BF16)
    hc = _dot(h, w_ref[:, _C_HC:_C_GB])
    gb_ref[0] = _dot(h, w_ref[:, _C_GB:_C_GC])
    gc = _dot(h, w_ref[:, _C_GC:_C_Q])
    u_ref[0] = gc * hc
    qz_ref[0] = _dot(h, w_ref[:, _C_Q:_C_KVC]).astype(qz_ref.dtype)
    kvc_ref[0] = _dot(h, w_ref[:, _C_KVC:_C_KVS])
    kvs = _dot(h, w_ref[:, _C_KVS:_C_KVW])
    kvs_ref[0] = kvs
    kvsb_ref[0] = kvs.astype(BF16)
    kvw = _dot(h, w_ref[:, _C_KVW:_C_GL])
    kvw_ref[0] = kvw
    kvwb_ref[0] = kvw.astype(BF16)
    gates_ref[0] = jax.nn.sigmoid(_dot(h, w_ref[:, _C_GL:W_IN_COLS]))


def _in_proj(x, scale, shift, w_cat, tm, qz_dtype):
    b, t, d = x.shape
    rm = scale.shape[1]
    mod_block = (1, tm, d) if rm == t else (1, 1, d)
    mod_map = (lambda i, j: (i, j, 0)) if rm == t else (lambda i, j: (i, 0, 0))
    row = lambda cols: pl.BlockSpec((1, tm, cols), lambda i, j: (i, j, 0))
    sds = lambda cols, dt: jax.ShapeDtypeStruct((b, t, cols), dt)
    return pl.pallas_call(
        _inproj_kernel,
        grid=(b, t // tm),
        in_specs=[row(d), pl.BlockSpec(mod_block, mod_map), pl.BlockSpec(mod_block, mod_map),
                  _const_spec((d, W_IN_COLS))],
        out_specs=[row(CONV_DIM), row(CONV_DIM), row(QZ_DIM), row(KV_COLS), row(KV_COLS), row(KV_COLS),
                   row(KV_COLS), row(KV_COLS), row(LANES)],
        out_shape=[sds(CONV_DIM, F32), sds(CONV_DIM, F32), sds(QZ_DIM, qz_dtype), sds(KV_COLS, F32),
                   sds(KV_COLS, F32), sds(KV_COLS, F32), sds(KV_COLS, BF16), sds(KV_COLS, BF16),
                   sds(LANES, F32)],
        compiler_params=_cparams(2),
        name="in_proj",
    )(x, scale, shift, w_cat)


def _gelu_tanh(x):
    return jax.nn.gelu(x, approximate=True)


def _cmp_bias_e(pe_ref, w1f_ref, b1_ref, e):
    pe = jnp.broadcast_to(pe_ref[e:e + 1, :], (8, pe_ref.shape[1])).astype(BF16)
    return _dot(pe, w1f_ref[e])[0:1, :] + b1_ref[e:e + 1, :]


def _compress_dense_kernel(x_ref, wbig_ref, pe_ref, w1f_ref, b1_ref, w2_ref, o_ref):
    half = TB_COLS // 2
    tb = _dot(x_ref[0].astype(BF16), wbig_ref[...])
    n = tb.shape[0]
    top = tb[:, :half]
    bot_next = pltpu.roll(tb[:, half:], n - 1, 0)
    bias = jnp.concatenate([_cmp_bias_e(pe_ref, w1f_ref, b1_ref, e) for e in (0, 0, 1, 1)], axis=1)
    hid = _gelu_tanh(top + bot_next + bias)
    o_ref[0] = _dot(hid.astype(BF16), w2_ref[...]).astype(o_ref.dtype)


def _compress_dense(kvc, wbig, pe2, w1f, b1, w2big):
    b, s, _ = kvc.shape
    nch = s // CMP_STRIDE
    x = kvc.reshape(b, nch, CHUNK_COLS)
    return pl.pallas_call(
        _compress_dense_kernel,
        grid=(b,),
        in_specs=[pl.BlockSpec((1, nch, CHUNK_COLS), lambda i: (i, 0, 0)),
                  _const_spec(wbig.shape), _const_spec(pe2.shape), _const_spec(w1f.shape),
                  _const_spec(b1.shape), _const_spec(w2big.shape)],
        out_specs=pl.BlockSpec((1, nch, KV_COLS), lambda i: (i, 0, 0)),
        out_shape=jax.ShapeDtypeStruct((b, nch, KV_COLS), BF16),
        compiler_params=_cparams(1),
        name="compress_prompt",
    )(x, wbig, pe2, w1f, b1, w2big)


PAGES_PER_STEP = 32


def _compress_paged_kernel(pt_ref, *refs, n_steps):
    g_pages = PAGES_PER_STEP
    page_refs = refs[:g_pages]
    xnew_ref, wpair_ref, wbot_ref, pe_ref, w1f_ref, b1_ref, w2_ref, o_ref, r_sc, x_sc = refs[g_pages:]
    kt = pl.program_id(1)
    page = page_refs[0].shape[2]
    rows_step = g_pages * (page // CMP_STRIDE)
    half_cols = N_KV * HEAD_DIM

    row0 = pl.multiple_of(kt * rows_step, rows_step)
    for e in range(2):
        for p in range(g_pages):
            r_sc[p * page:(p + 1) * page, :] = page_refs[p][0, e * half_cols:(e + 1) * half_cols, :].T
        for j in range(CMP_STRIDE):
            xj = r_sc[pl.ds(j, rows_step, stride=CMP_STRIDE), :]
            x_sc[e, j, pl.ds(row0, rows_step), :] = xj.astype(BF16)

    @pl.when(kt == n_steps - 1)
    def _():
        n = x_sc.shape[2]
        hid_cols = N_KV * CMP_HID
        xn = jnp.broadcast_to(xnew_ref[0], (8, CHUNK_COLS)).astype(BF16)
        bot_new = _dot(xn, wbot_ref[...])[0:1, :]
        last = _iota((n, 1), 0) == n - 1
        outs = []
        for e in range(2):
            acc = jnp.zeros((n, 2 * hid_cols), F32)
            for jp in range(CMP_STRIDE // 2):
                xs = jnp.concatenate([x_sc[e, 2 * jp], x_sc[e, 2 * jp + 1]], axis=1)
                acc = acc + _dot(xs, wpair_ref[e, jp])
            top, bot = acc[:, :hid_cols], acc[:, hid_cols:]
            bot_next = jnp.where(last, bot_new[:, e * hid_cols:(e + 1) * hid_cols], pltpu.roll(bot, n - 1, 0))
            bias = jnp.concatenate([_cmp_bias_e(pe_ref, w1f_ref, b1_ref, e)] * N_KV, axis=1)
            hid = _gelu_tanh(top + bot_next + bias)
            outs.append(_dot(hid.astype(BF16), w2_ref[e]))
        o_ref[0] = jnp.concatenate(outs, axis=1).astype(o_ref.dtype)


def _compress_paged(pool_t, page_table, xnew, wpair, wbot, pe2, w1f, b1, w2pair):
    bn, n_pages = page_table.shape
    page = pool_t.shape[2]
    g = PAGES_PER_STEP
    n_steps = n_pages // g
    assert n_pages % g == 0 and page % CMP_STRIDE == 0 and page == LANES
    n = n_pages * page // CMP_STRIDE

    def page_spec(k):
        return pl.BlockSpec((1, KV_COLS, page), lambda i, j, pt: (pt[i, j * g + k], 0, 0))

    def const(shape):
        nd = len(shape)
        return pl.BlockSpec(shape, lambda i, j, pt: (0,) * nd, pipeline_mode=pl.Buffered(1))

    grid_spec = pltpu.PrefetchScalarGridSpec(
        num_scalar_prefetch=1,
        grid=(bn, n_steps),
        in_specs=[page_spec(k) for k in range(g)]
        + [pl.BlockSpec((1, 1, CHUNK_COLS), lambda i, j, pt: (i, 0, 0)), const(wpair.shape), const(wbot.shape),
           const(pe2.shape), const(w1f.shape), const(b1.shape), const(w2pair.shape)],
        out_specs=pl.BlockSpec((1, n, KV_COLS), lambda i, j, pt: (i, 0, 0)),
        scratch_shapes=[pltpu.VMEM((g * page, LANES), F32),
                        pltpu.VMEM((2, CMP_STRIDE, n, LANES), BF16)],
    )
    return pl.pallas_call(
        functools.partial(_compress_paged_kernel, n_steps=n_steps),
        grid_spec=grid_spec,
        out_shape=jax.ShapeDtypeStruct((bn, n, KV_COLS), BF16),
        compiler_params=_cparams(2),
        name="compress_sample",
    )(page_table, *([pool_t] * g), xnew, wpair, wbot, pe2, w1f, b1, w2pair)


def _split3_dot(x, m01):
    h1 = x.astype(BF16)
    r1 = x - h1.astype(F32)
    h2 = r1.astype(BF16)
    h3 = (r1 - h2.astype(F32)).astype(BF16)
    return _dot(h1, m01) + _dot(h2, m01) + _dot(h3, m01)


def _block_scores(imp, n_lanes):
    ncp = imp.shape[1]
    c = _iota((ncp, n_lanes), 0)
    j = _iota((ncp, n_lanes), 1)
    a = jnp.where((c >= CMP_PER_SEL * j - 1) & (c <= CMP_PER_SEL * j + CMP_PER_SEL - 1), 1.0, 0.0).astype(BF16)
    return _split3_dot(imp, a)


def _ranked_scores(score, qpos, n_blocks):
    j = _iota(score.shape, 1)
    cur = qpos >> 6
    valid = j * SEL_BLOCK <= qpos
    forced = (j == 0) | (j == cur) | (j == cur - 1)
    return jnp.where(forced, -NEG, jnp.where(valid, score, NEG))


def _select_blocks_iter(score, qpos, n_blocks, n_select):
    r, l = score.shape
    removed = 3.0 * NEG
    j = _iota((r, l), 1)
    sc = jnp.where(j < n_blocks, _ranked_scores(score, qpos, n_blocks), removed)

    def body(_, carry):
        sc, sel = carry
        pick = j == jnp.argmax(sc, axis=-1, keepdims=True)
        return jnp.where(pick, removed, sc), jnp.where(pick, 1.0, sel)

    _, sel = lax.fori_loop(0, n_select, body, (sc, jnp.zeros((r, l), F32)))
    return sel


def _select_blocks_rank(score, qpos, n_blocks, n_select):
    r, l = score.shape
    nb8 = -(-n_blocks // 8) * 8
    st = _ranked_scores(score, qpos, n_blocks).T[:nb8]
    jrow = _iota((nb8, r), 0)
    rank = jnp.zeros((nb8, r), F32)
    for jp in range(n_blocks):
        row = st[jp:jp + 1, :]
        beats = (row > st) | ((row == st) & (jrow > jp))
        rank = rank + jnp.where(beats, 1.0, 0.0)
    sel_t = jnp.where((rank < n_select) & (jrow < n_blocks), 1.0, 0.0)
    if nb8 < l:
        sel_t = jnp.concatenate([sel_t, jnp.zeros((l - nb8, r), F32)], axis=0)
    return sel_t.T


def _softmax_parts(s, mask):
    s = jnp.where(mask, s, NEG)
    m = jnp.max(s, axis=-1, keepdims=True)
    e = jnp.where(mask, jnp.exp(s - m), 0.0)
    return e, jnp.sum(e, axis=-1, keepdims=True)


SEL_TILE = 512


def _attn_prompt_kernel(qz_ref, kvs_ref, kvw_ref, kvc_ref, gates_ref, exp_ref, o_ref,
                        s_sc, p_sc, pw_sc, m_sc, l_sc, acc_sc, *, seq):
    qb = Q_BLOCK
    tk = SEL_TILE
    start = pl.program_id(1) * qb
    qpos = start + _iota((qb, 1), 0)
    ncp = kvc_ref.shape[1]
    n_blocks = -(-seq // SEL_BLOCK)
    rows = Q_PER_KV * qb
    gates = gates_ref[0]
    n_tiles = (start + qb + tk - 1) // tk
    head_rows = [slice(r * qb, (r + 1) * qb) for r in range(Q_PER_KV)]
    kv_heads = range(N_KV)

    def q_rows(g):
        return jnp.concatenate(
            [qz_ref[0, :, (g * Q_PER_KV + r) * LANES:(g * Q_PER_KV + r + 1) * LANES] for r in range(Q_PER_KV)],
            axis=0)


    o_cmp, sel = [], []
    c = _iota((qb, ncp), 1)
    cmask = (CMP_STRIDE * c + (CMP_BLOCK - 1) <= qpos) & (c < ncp - 1)
    for g in kv_heads:
        s = _dot_nt(q_rows(g), kvc_ref[0, :, 0:LANES]).reshape(Q_PER_KV, qb, ncp)
        e, l = _softmax_parts(s, cmask)
        p = e / jnp.maximum(l, 1e-30)
        o_cmp.append(_dot(p.reshape(rows, ncp).astype(BF16), kvc_ref[0, :, LANES:2 * LANES]))
        imp = jnp.sum(p, axis=0)
        sel.append(_select_blocks_rank(_block_scores(imp, LANES), qpos, n_blocks,
                                       min(N_SELECT, n_blocks)).astype(BF16))

    m_sc[...] = jnp.full(m_sc.shape, NEG, F32)

    def pass1(kt, _):
        k0 = pl.multiple_of(kt * tk, tk)
        causal = (k0 + _iota((1, tk), 1)) <= qpos
        for g in kv_heads:
            s = _dot_nt(q_rows(g), kvs_ref[0, pl.ds(k0, tk), 0:LANES])
            mask = (_dot(sel[g], exp_ref[kt]) > 0.5) & causal
            for rs in head_rows:
                sr = jnp.where(mask, s[rs], NEG)
                s_sc[g, kt, rs, :] = sr
                m_sc[g, rs, :] = jnp.maximum(m_sc[g, rs, :], functools.reduce(jnp.maximum, _lane_tiles(sr)))
        return 0

    lax.fori_loop(0, n_tiles, pass1, 0)
    m_sc[...] = jnp.broadcast_to(jnp.max(m_sc[...], axis=-1, keepdims=True), m_sc.shape)
    l_sc[...] = jnp.zeros(l_sc.shape, F32)
    acc_sc[...] = jnp.zeros(acc_sc.shape, F32)

    def pass2(kt, _):
        k0 = pl.multiple_of(kt * tk, tk)
        for g in kv_heads:
            for rs in head_rows:
                e = jnp.exp(s_sc[g, kt, rs, :] - jnp.concatenate([m_sc[g, rs, :]] * (tk // LANES), axis=1))
                l_sc[g, rs, :] = l_sc[g, rs, :] + functools.reduce(jnp.add, _lane_tiles(e))
                p_sc[g, rs, :] = e.astype(BF16)
            acc_sc[g] = acc_sc[g] + _dot(p_sc[g], kvs_ref[0, pl.ds(k0, tk), LANES:2 * LANES])
        return 0

    lax.fori_loop(0, n_tiles, pass2, 0)

    wlen = WINDOW + qb
    w0 = pl.multiple_of(jnp.maximum(start - WINDOW, 0), qb)
    kpos = w0 + _iota((1, wlen), 1)
    wmask = (kpos <= qpos) & (kpos > qpos - WINDOW)
    for g in kv_heads:
        s = _dot_nt(q_rows(g), kvw_ref[0, pl.ds(w0, wlen), 0:LANES])
        lw = []
        for rs in head_rows:
            sr = jnp.where(wmask, s[rs], NEG)
            m = jnp.max(functools.reduce(jnp.maximum, _lane_tiles(sr)), axis=-1, keepdims=True)
            e = jnp.exp(sr - m)
            lw.append(jnp.sum(functools.reduce(jnp.add, _lane_tiles(e)), axis=-1, keepdims=True))
            pw_sc[g, rs, :] = e.astype(BF16)
        o_win = _dot(pw_sc[g], kvw_ref[0, pl.ds(w0, wlen), LANES:2 * LANES])
        o_slc = acc_sc[g] / jnp.maximum(jnp.sum(l_sc[g], axis=-1, keepdims=True), 1e-30)
        own = (_iota((qb, LANES), 1) >> 6) == g
        for r, rs in enumerate(head_rows):
            h = g * Q_PER_KV + r
            o = (gates[:, h:h + 1] * o_cmp[g][rs] + gates[:, N_HEADS + h:N_HEADS + h + 1] * o_slc[rs]
                 + gates[:, 2 * N_HEADS + h:2 * N_HEADS + h + 1] * (o_win[rs] / jnp.maximum(lw[r], 1e-30)))
            o_ref[0, :, h * LANES:(h + 1) * LANES] = jnp.where(own, o, 0.0)


def _expand_matrix(n_tiles, tile):
    t = jnp.arange(n_tiles)[:, None, None]
    j = jnp.arange(LANES)[None, :, None]
    k = jnp.arange(tile)[None, None, :]
    return (j == (t * tile + k) // SEL_BLOCK).astype(BF16)


def _attn_prompt(qz, kvs_b, kvw_b, kv_cmpr, gates):
    b, s, _ = qz.shape
    assert s % Q_BLOCK == 0 and s % SEL_TILE == 0 and s >= WINDOW + Q_BLOCK and s // SEL_BLOCK <= LANES
    ncp = kv_cmpr.shape[1]
    n_tiles = s // SEL_TILE
    expand = _expand_matrix(n_tiles, SEL_TILE)
    rows = Q_PER_KV * Q_BLOCK
    full = lambda n: pl.BlockSpec((1, n, KV_COLS), lambda i, j: (i, 0, 0))
    return pl.pallas_call(
        functools.partial(_attn_prompt_kernel, seq=s),
        grid=(b, s // Q_BLOCK),
        in_specs=[pl.BlockSpec((1, Q_BLOCK, QZ_DIM), lambda i, j: (i, j, 0)), full(s), full(s), full(ncp),
                  pl.BlockSpec((1, Q_BLOCK, LANES), lambda i, j: (i, j, 0)), _const_spec(expand.shape)],
        out_specs=pl.BlockSpec((1, Q_BLOCK, QZ_DIM), lambda i, j: (i, j, 0)),
        out_shape=jax.ShapeDtypeStruct((b, s, QZ_DIM), F32),
        scratch_shapes=[pltpu.VMEM((N_KV, n_tiles, rows, SEL_TILE), F32), pltpu.VMEM((N_KV, rows, SEL_TILE), BF16),
                        pltpu.VMEM((N_KV, rows, WINDOW + Q_BLOCK), BF16), pltpu.VMEM((N_KV, rows, LANES), F32),
                        pltpu.VMEM((N_KV, rows, LANES), F32), pltpu.VMEM((N_KV, rows, LANES), F32)],
        compiler_params=_cparams(2),
        name="attn_prompt",
    )(qz, kvs_b, kvw_b, kv_cmpr, gates, expand)


def _attn_sample_kernel(pt_ref, *refs, past, n_steps):
    g_pages = PAGES_PER_STEP
    page_refs = refs[:g_pages]
    (qz_ref, kvc_ref, gates_ref, kvs_new_ref, win_ref, kvw_new_ref, exp_ref,
     o_ref, q_sc, sel_sc, m_sc, l_sc, acc_sc, ocw_sc) = refs[g_pages:]
    kt = pl.program_id(1)
    t = qz_ref.shape[1]
    rows = N_HEADS * t
    page = page_refs[0].shape[2]
    tile = g_pages * page
    blocks_per_tile = tile // SEL_BLOCK
    n_past_blocks = past // SEL_BLOCK
    n_blocks = n_past_blocks + -(-t // SEL_BLOCK)
    sel_lanes = sel_sc.shape[0] * LANES
    half_cols = N_KV * HEAD_DIM
    qpos = past + _iota((t, 1), 0)
    qpos_gt = jnp.concatenate([qpos] * N_KV, axis=0)
    own_rows = _iota((N_KV, Q_PER_KV, t, LANES), 0) == (_iota((N_KV, Q_PER_KV, t, LANES), 3) >> 6)

    def online_update(s, mask, pv_fn):
        k = s.shape[1]
        s = jnp.where(mask, s.reshape(N_KV, Q_PER_KV, t, k), NEG)
        m_prev = m_sc[...]
        m_new = jnp.maximum(m_prev, jnp.max(s, axis=-1, keepdims=True))
        alpha = jnp.exp(m_prev - m_new)
        e = jnp.where(mask, jnp.exp(s - m_new), 0.0)
        l_sc[...] = alpha * l_sc[...] + jnp.sum(e, axis=-1, keepdims=True)
        pv = pv_fn(e.reshape(rows, k).astype(BF16)).reshape(N_KV, Q_PER_KV, t, LANES)
        acc_sc[...] = alpha * acc_sc[...] + pv
        m_sc[...] = m_new

    @pl.when(kt == 0)
    def _():
        q = jnp.concatenate([qz_ref[0, :, h * LANES:(h + 1) * LANES] for h in range(N_HEADS)], axis=0)
        q_sc[...] = q.astype(BF16)
        qb = q_sc[...]
        ncp = kvc_ref.shape[1]
        s = _dot_nt(qb, kvc_ref[0, :, 0:LANES]).reshape(N_HEADS, t, ncp)
        c = _iota((t, ncp), 1)
        cmask = CMP_STRIDE * c + (CMP_BLOCK - 1) <= qpos
        e, l = _softmax_parts(s, cmask)
        p = e / jnp.maximum(l, 1e-30)
        o_cmp = _dot(p.reshape(rows, ncp).astype(BF16), kvc_ref[0, :, LANES:2 * LANES])
        imp = jnp.sum(p.reshape(N_KV, Q_PER_KV, t, ncp), axis=1).reshape(N_KV * t, ncp)
        sel = _select_blocks_iter(_block_scores(imp, sel_lanes), qpos_gt, n_blocks, min(N_SELECT, n_blocks))
        for w in range(sel_sc.shape[0]):
            sel_sc[w] = sel[:, w * LANES:(w + 1) * LANES]
        wb = win_ref.shape[2]
        win = win_ref[0].astype(BF16)
        kvw_new = kvw_new_ref[0].astype(BF16)
        s_old = _dot(qb, win[0:half_cols]).reshape(N_HEADS, t, wb)
        s_new = _dot_nt(qb, kvw_new[:, 0:half_cols]).reshape(N_HEADS, t, t)
        kpos_old = past - wb + _iota((1, wb), 1)
        kpos_new = past + _iota((1, t), 1)
        mask_old = (kpos_old <= qpos) & (kpos_old > qpos - WINDOW) & (kpos_old >= 0)
        mask_new = (kpos_new <= qpos) & (kpos_new > qpos - WINDOW)
        s_old = jnp.where(mask_old, s_old, NEG)
        s_new = jnp.where(mask_new, s_new, NEG)
        m = jnp.maximum(jnp.max(s_old, axis=-1, keepdims=True), jnp.max(s_new, axis=-1, keepdims=True))
        e_old = jnp.where(mask_old, jnp.exp(s_old - m), 0.0)
        e_new = jnp.where(mask_new, jnp.exp(s_new - m), 0.0)
        l = jnp.sum(e_old, axis=-1, keepdims=True) + jnp.sum(e_new, axis=-1, keepdims=True)
        o_win = (_dot_nt(e_old.reshape(rows, wb).astype(BF16), win[half_cols:])
                 + _dot(e_new.reshape(rows, t).astype(BF16), kvw_new[:, half_cols:]))
        o_win = o_win.reshape(N_HEADS, t, LANES) / jnp.maximum(l, 1e-30)
        gts = gates_ref[0]
        gc = jnp.stack([jnp.broadcast_to(gts[:, h:h + 1], (t, LANES)) for h in range(N_HEADS)])
        gw = jnp.stack([jnp.broadcast_to(gts[:, 2 * N_HEADS + h:2 * N_HEADS + h + 1], (t, LANES))
                        for h in range(N_HEADS)])
        ocw_sc[...] = gc * o_cmp.reshape(N_HEADS, t, LANES) + gw * o_win
        m_sc[...] = jnp.full(m_sc.shape, NEG, F32)
        l_sc[...] = jnp.zeros(l_sc.shape, F32)
        acc_sc[...] = jnp.zeros(acc_sc.shape, F32)
        kvn = kvs_new_ref[0].astype(BF16)
        jn = kpos_new >> 6
        seln = jnp.zeros((N_KV * t, t), F32)
        for jb in range(n_past_blocks, n_blocks):
            seln = jnp.where(jn == jb, sel[:, jb:jb + 1], seln)
        nmask = ((seln > 0.5) & (kpos_new <= qpos_gt)).reshape(N_KV, 1, t, t)
        online_update(_dot_nt(qb, kvn[:, 0:half_cols]), nmask, lambda pb: _dot(pb, kvn[:, half_cols:]))

    k_t = jnp.concatenate([r[0, 0:half_cols, :] for r in page_refs], axis=1).astype(BF16)
    v_t = jnp.concatenate([r[0, half_cols:, :] for r in page_refs], axis=1).astype(BF16)
    b0 = kt * blocks_per_tile
    selw = sel_sc[b0 // LANES].astype(BF16)
    kpos = kt * tile + _iota((1, tile), 1)
    mask = ((_dot(selw, exp_ref[0]) > 0.5) & (kpos <= qpos_gt)).reshape(N_KV, 1, t, tile)
    online_update(_dot(q_sc[...], k_t), mask, lambda pb: _dot_nt(pb, v_t))

    @pl.when(kt == n_steps - 1)
    def _():
        gts = gates_ref[0]
        gs = jnp.stack([jnp.broadcast_to(gts[:, N_HEADS + h:N_HEADS + h + 1], (t, LANES))
                        for h in range(N_HEADS)])
        o_slc = (acc_sc[...] / jnp.maximum(l_sc[...], 1e-30)).reshape(N_HEADS, t, LANES)
        o = jnp.where(own_rows.reshape(N_HEADS, t, LANES), ocw_sc[...] + gs * o_slc, 0.0)
        for h in range(N_HEADS):
            o_ref[0, :, h * LANES:(h + 1) * LANES] = o[h]


def _attn_sample(qz, kv_cmpr, gates, kvs_new, slc_pool_t, win_t, kvw_new, page_table):
    bn, t, _ = qz.shape
    n_pages = page_table.shape[1]
    page = slc_pool_t.shape[2]
    past = n_pages * page
    g = PAGES_PER_STEP
    n_steps = n_pages // g
    tile = g * page
    blocks_per_tile = tile // SEL_BLOCK
    assert n_pages % g == 0 and page % SEL_BLOCK == 0 and LANES % blocks_per_tile == 0 and t % 8 == 0
    n_blocks = past // SEL_BLOCK + -(-t // SEL_BLOCK)
    sel_groups = -(-n_blocks // LANES)
    ncp = kv_cmpr.shape[1]
    wb = win_t.shape[2]
    n_var = LANES // blocks_per_tile
    expand = _expand_matrix(n_var, tile)

    def page_spec(k):
        return pl.BlockSpec((1, KV_COLS, page), lambda i, j, pt: (pt[i, j * g + k], 0, 0))

    per_b = lambda n, cols: pl.BlockSpec((1, n, cols), lambda i, j, pt: (i, 0, 0))
    grid_spec = pltpu.PrefetchScalarGridSpec(
        num_scalar_prefetch=1,
        grid=(bn, n_steps),
        in_specs=[page_spec(k) for k in range(g)]
        + [per_b(t, QZ_DIM), per_b(ncp, KV_COLS), per_b(t, LANES), per_b(t, KV_COLS), per_b(KV_COLS, wb),
           per_b(t, KV_COLS), pl.BlockSpec((1, LANES, tile), lambda i, j, pt: (j % n_var, 0, 0))],
        out_specs=per_b(t, QZ_DIM),
        scratch_shapes=[pltpu.VMEM((N_HEADS * t, LANES), BF16),
                        pltpu.VMEM((sel_groups, N_KV * t, LANES), F32),
                        pltpu.VMEM((N_KV, Q_PER_KV, t, 1), F32), pltpu.VMEM((N_KV, Q_PER_KV, t, 1), F32),
                        pltpu.VMEM((N_KV, Q_PER_KV, t, LANES), F32),
                        pltpu.VMEM((N_HEADS, t, LANES), F32)],
    )
    return pl.pallas_call(
        functools.partial(_attn_sample_kernel, past=past, n_steps=n_steps),
        grid_spec=grid_spec,
        out_shape=jax.ShapeDtypeStruct((bn, t, QZ_DIM), F32),
        compiler_params=_cparams(2),
        name="attn_sample",
    )(page_table, *([slc_pool_t] * g), qz, kv_cmpr, gates, kvs_new, win_t, kvw_new, expand)


def _layernorm(x, g, b):
    mu = jnp.mean(x, axis=-1, keepdims=True)
    xc = x - mu
    var = jnp.mean(xc * xc, axis=-1, keepdims=True)
    return xc * lax.rsqrt(var + EPS) * g + b


def _post_kernel(x_ref, o_ref, u_ref, uprev_ref, gb_ref, st1_ref, st2_ref, g1_ref, sh2_ref, sc2_ref, g2_ref,
                 convw_ref, gconv_ref, gnsa_ref, woc_ref, won_ref, ln1g_ref, ln1b_ref, ln2g_ref, ln2b_ref,
                 wg_ref, wu_ref, wd_ref, y_ref, *, seq):
    tm = x_ref.shape[1]
    u = u_ref[0]
    ext = jnp.concatenate([uprev_ref[0], u], axis=0)
    if tm <= seq:
        pos = (pl.program_id(1) * tm) % seq + _iota((tm, 1), 0)
    else:
        pos = lax.rem(_iota((tm, 1), 0), seq)
    p1 = jnp.where(pos >= 1, ext[7:7 + tm], st1_ref[0])
    p2 = jnp.where(pos >= 2, ext[6:6 + tm], st2_ref[0])
    cw = convw_ref[...]
    y_c = gb_ref[0] * (cw[0:1] * p2 + cw[1:2] * p1 + cw[2:3] * u)
    yn = y_c * lax.rsqrt(jnp.mean(y_c * y_c, axis=-1, keepdims=True) + EPS) * gconv_ref[...]
    o = o_ref[0]
    on = o * lax.rsqrt(jnp.sum(o * o, axis=-1, keepdims=True) * (1.0 / NSA_DIM) + EPS) * gnsa_ref[...]
    mix = _dot(yn.astype(BF16), woc_ref[...]) + _dot(on.astype(BF16), won_ref[...])
    x1 = _layernorm(ALPHA * x_ref[0] + g1_ref[0] * mix, ln1g_ref[...], ln1b_ref[...])
    h = (x1 * (1.0 + sc2_ref[0]) + sh2_ref[0]).astype(BF16)
    a = _dot(h, wg_ref[...])
    f = (a * jax.nn.sigmoid(a)) * _dot(h, wu_ref[...])
    f = _dot(f.astype(BF16), wd_ref[...])
    y_ref[0] = _layernorm(ALPHA * x1 + g2_ref[0] * f, ln2g_ref[...], ln2b_ref[...])


def _post(x, o, u, gb, st1, st2, mods, w, tm, seq):
    b, t, d = x.shape
    row = lambda cols: pl.BlockSpec((1, tm, cols), lambda i, j: (i, j, 0))

    def bcast(a):
        if a.shape[1] == t:
            return pl.BlockSpec((1, tm, a.shape[2]), lambda i, j: (i, j, 0))
        return pl.BlockSpec((1, 1, a.shape[2]), lambda i, j: (i, 0, 0))

    prev = pl.BlockSpec((1, 8, CONV_DIM), lambda i, j: (i, jnp.maximum(j * (tm // 8) - 1, 0), 0))
    consts = [w["conv_w"], w["g_conv"], w["g_nsa"], w["wo_c"], w["wo_n"], w["ln1_g"], w["ln1_b"], w["ln2_g"],
              w["ln2_b"], w["w_gate"], w["w_up"], w["w_down"]]
    return pl.pallas_call(
        functools.partial(_post_kernel, seq=seq),
        grid=(b, t // tm),
        in_specs=[row(d), row(QZ_DIM), row(CONV_DIM), prev, row(CONV_DIM), bcast(st1), bcast(st2)]
        + [bcast(m) for m in mods] + [_const_spec(c.shape) for c in consts],
        out_specs=row(d),
        out_shape=jax.ShapeDtypeStruct((b, t, d), F32),
        compiler_params=_cparams(2),
        name="post_block",
    )(x, o, u, u, gb, st1, st2, *mods, *consts)


def _prep_w_in(w_in):
    d = w_in.shape[0]
    c3 = 3 * CONV_DIM
    splits = (CONV_DIM, 2 * CONV_DIM, c3, c3 + NSA_DIM, c3 + NSA_DIM + KV_COLS, c3 + NSA_DIM + 2 * KV_COLS,
              c3 + NSA_DIM + 3 * KV_COLS)
    hc, gb, gc, q, kvc, kvs, kvw, gl = jnp.split(w_in, splits, axis=1)
    q4 = q.reshape(d, N_KV, Q_PER_KV, HEAD_DIM) * SCALE
    qz = jnp.einsum("dgrh,gk->dgrkh", q4, jnp.eye(N_KV, dtype=w_in.dtype)).reshape(d, QZ_DIM)
    glp = jnp.pad(gl, ((0, 0), (0, LANES - gl.shape[1])))
    return jnp.concatenate([hc, gb, gc, qz, kvc, kvs, kvw, glp], axis=1).astype(BF16)


def _prep_cmp(cmp_pe, cmp_w1, cmp_b1, cmp_w2):
    eye = jnp.eye(N_KV, dtype=cmp_w1.dtype)
    eye2 = jnp.eye(2, dtype=cmp_w1.dtype)
    w1r = cmp_w1.reshape(2, 2, CMP_STRIDE, HEAD_DIM, CMP_HID)
    wbig = jnp.einsum("etjdh,ef,gk->jegdtfkh", w1r, eye2, eye).reshape(CHUNK_COLS, TB_COLS).astype(BF16)
    w2big = jnp.einsum("ehd,ef,gk->eghfkd", cmp_w2, eye2, eye).reshape(TB_COLS // 2, KV_COLS).astype(BF16)
    w1p = cmp_w1.reshape(2, 2, CMP_STRIDE // 2, 2, HEAD_DIM, CMP_HID)
    wpair = jnp.einsum("etpjdh,gk->epjgdtkh", w1p, eye).reshape(
        2, CMP_STRIDE // 2, 2 * N_KV * HEAD_DIM, 2 * N_KV * CMP_HID).astype(BF16)
    w2pair = jnp.einsum("ehd,gk->eghkd", cmp_w2, eye).reshape(2, N_KV * CMP_HID, N_KV * HEAD_DIM).astype(BF16)
    pe2 = cmp_pe.reshape(2, CMP_BLOCK * HEAD_DIM)
    w1f = cmp_w1.reshape(2, CMP_BLOCK * HEAD_DIM, CMP_HID).astype(BF16)
    dense = (wbig, pe2, w1f, cmp_b1, w2big)
    paged = (wpair, wbig[:, TB_COLS // 2:], pe2, w1f, cmp_b1, w2pair)
    return dense, paged


def _prep_post(conv_w, g_conv, g_nsa, w_o, ln1_g, ln1_b, ln2_g, ln2_b, w_gate, w_up, w_down):
    d = w_o.shape[1]
    own = jnp.eye(N_KV, dtype=w_o.dtype)
    g_nsa_z = jnp.einsum("grh,gk->grkh", g_nsa.reshape(N_KV, Q_PER_KV, HEAD_DIM), own).reshape(1, QZ_DIM)
    wo_n = jnp.einsum("grhd,gk->grkhd", w_o[CONV_DIM:].reshape(N_KV, Q_PER_KV, HEAD_DIM, d), own)
    row = lambda a: a.reshape(1, -1)
    return dict(conv_w=conv_w, g_conv=row(g_conv), g_nsa=g_nsa_z, wo_c=w_o[:CONV_DIM].astype(BF16),
                wo_n=wo_n.reshape(QZ_DIM, d).astype(BF16), ln1_g=row(ln1_g), ln1_b=row(ln1_b),
                ln2_g=row(ln2_g), ln2_b=row(ln2_b), w_gate=w_gate.astype(BF16), w_up=w_up.astype(BF16),
                w_down=w_down.astype(BF16))


def _feature_major(cache):
    n, rows = cache.shape[:2]
    return jnp.transpose(cache, (0, 2, 3, 4, 1)).reshape(n, KV_COLS, rows)


def kernel(x_prompt, x_sample, cache_cmp_kv, cache_slc_kv, cache_win_kv, state_conv, page_table, c_prompt,
           c_sample, w_ada, b_ada, w_in, conv_w, cmp_pe, cmp_w1, cmp_b1, cmp_w2, g_conv_out, g_nsa_out, w_o,
           ln1_g, ln1_b, ln2_g, ln2_b, w_ffn_gate, w_ffn_up, w_ffn_down):
    assert w_ada.shape[0] == DEPTH
    bp, s, d = x_prompt.shape
    bs, t, _ = x_sample.shape
    kv_shape = (2, N_KV, HEAD_DIM)

    w_cat = _prep_w_in(w_in[0])
    cmp_dense_w, cmp_paged_w = _prep_cmp(cmp_pe[0], cmp_w1[0], cmp_b1[0], cmp_w2[0])
    post_w = _prep_post(conv_w[0], g_conv_out[0], g_nsa_out[0], w_o[0], ln1_g[0], ln1_b[0], ln2_g[0], ln2_b[0],
                        w_ffn_gate[0], w_ffn_up[0], w_ffn_down[0])

    mod = _modulation(jnp.concatenate([c_prompt, c_sample], axis=0), w_ada[0], b_ada[0])
    mods_p = [m[:, None, :] for m in jnp.split(mod[:bp], 6, axis=-1)]
    mods_s = [jnp.repeat(m, t, axis=0)[None] for m in jnp.split(mod[bp:], 6, axis=-1)]

    u, gb, qz, kvc, kvs, kvw, kvs_b, kvw_b, gates = _in_proj(x_prompt, mods_p[1], mods_p[0], w_cat, 512, BF16)
    kv_cmpr = _compress_dense(kvc, *cmp_dense_w)
    o = _attn_prompt(qz, kvs_b, kvw_b, kv_cmpr, gates)
    zero_state = jnp.zeros((bp, 1, CONV_DIM), F32)
    y_prompt = _post(x_prompt, o, u, gb, zero_state, zero_state, mods_p[2:], post_w, 512, s)
    w_keep = min(WINDOW, s)
    cmp_p = kvc.reshape(1, bp, s, *kv_shape)
    slc_p = kvs.reshape(1, bp, s, *kv_shape)
    win_p = kvw[:, s - w_keep:].reshape(1, bp, w_keep, *kv_shape)
    conv_p = u[:, s - (CONV_W - 1):][None]

    rows = bs * t
    us, gbs, qzs, kvcs, kvss, kvws, _, _, gates_s = _in_proj(
        x_sample.reshape(1, rows, d), mods_s[1], mods_s[0], w_cat, rows, F32)
    per_b = lambda a: a.reshape(bs, t, a.shape[-1])
    kvcs, kvss, kvws = per_b(kvcs), per_b(kvss), per_b(kvws)
    xnew = jnp.pad(kvcs, ((0, 0), (0, (-t) % CMP_STRIDE), (0, 0))).reshape(bs, 1, CHUNK_COLS)
    kv_cmpr_s = _compress_paged(_feature_major(cache_cmp_kv[0]), page_table, xnew, *cmp_paged_w)
    o_s = _attn_sample(per_b(qzs), kv_cmpr_s, per_b(gates_s), kvss, _feature_major(cache_slc_kv[0]),
                       _feature_major(cache_win_kv[0]), kvws, page_table)
    st = state_conv[0]
    tpos = jnp.arange(t)[None, :, None]
    st1 = jnp.where(tpos == 0, st[:, 1:2], 0.0).reshape(1, rows, CONV_DIM)
    st2 = jnp.where(tpos == 0, st[:, 0:1], jnp.where(tpos == 1, st[:, 1:2], 0.0)).reshape(1, rows, CONV_DIM)
    y_sample = _post(x_sample.reshape(1, rows, d), o_s.reshape(1, rows, QZ_DIM), us, gbs, st1, st2, mods_s[2:],
                     post_w, rows, t).reshape(bs, t, d)
    cmp_s = kvcs.reshape(1, bs, t, *kv_shape)
    slc_s = kvss.reshape(1, bs, t, *kv_shape)
    win_all = jnp.concatenate([cache_win_kv[0], kvws.reshape(bs, t, *kv_shape)], axis=1)
    win_s = win_all[:, t:][None]
    conv_s = jnp.concatenate([st, per_b(us)], axis=1)[:, t:][None]
    return (y_prompt, y_sample, cmp_p, slc_p, win_p, conv_p, cmp_s, slc_s, win_s, conv_s)
```

```python
import functools

import jax
import jax.numpy as jnp
from jax import lax
from jax.experimental import pallas as pl
from jax.experimental.pallas import tpu as pltpu

HEAD_DIM = 64
N_KV = 2
Q_PER_KV = 6
N_HEADS = N_KV * Q_PER_KV
CONV_DIM = 256
CONV_W = 3
NSA_DIM = N_HEADS * HEAD_DIM
KV_COLS = 2 * N_KV * HEAD_DIM
CMP_BLOCK = 32
CMP_STRIDE = 16
CMP_HID = 2 * HEAD_DIM
SEL_BLOCK = 64
CMP_PER_SEL = SEL_BLOCK // CMP_STRIDE
N_SELECT = 16
WINDOW = 512
Q_BLOCK = 128
DEPTH = 1
ALPHA = (2 * DEPTH) ** 0.25
EPS = 1e-5
SCALE = HEAD_DIM ** -0.5

LANES = 128
QZ_DIM = N_HEADS * LANES
CHUNK_COLS = CMP_STRIDE * KV_COLS
TB_COLS = 2 * 2 * N_KV * CMP_HID
NEG = -1e30
VMEM_LIMIT = 56 * 1024 * 1024

F32 = jnp.float32
BF16 = jnp.bfloat16


def _cparams(n_grid):
    return pltpu.CompilerParams(dimension_semantics=("arbitrary",) * n_grid, vmem_limit_bytes=VMEM_LIMIT)


def _const_spec(shape):
    nd = len(shape)
    return pl.BlockSpec(shape, lambda *_: (0,) * nd, pipeline_mode=pl.Buffered(1))


def _dot(a, b):
    return jnp.dot(a, b, preferred_element_type=F32)


def _dot_nt(a, b):
    return lax.dot_general(a, b, (((1,), (1,)), ((), ())), preferred_element_type=F32)


def _iota(shape, dim):
    return lax.broadcasted_iota(jnp.int32, shape, dim)


def _lane_tiles(x):
    return [x[:, w * LANES:(w + 1) * LANES] for w in range(x.shape[1] // LANES)]


def _mod_kernel(c_ref, w_ref, b_ref, o_ref):
    c = c_ref[...]
    a = (c * jax.nn.sigmoid(c)).astype(BF16)
    o_ref[...] = _dot(a, w_ref[...].astype(BF16)) + b_ref[...]


def _modulation(c_all, w_ada, b_ada):
    m, d = c_all.shape
    n = w_ada.shape[1]
    tn = 512
    return pl.pallas_call(
        _mod_kernel,
        grid=(n // tn,),
        in_specs=[pl.BlockSpec((m, d), lambda j: (0, 0)),
                  pl.BlockSpec((d, tn), lambda j: (0, j)),
                  pl.BlockSpec((1, tn), lambda j: (0, j))],
        out_specs=pl.BlockSpec((m, tn), lambda j: (0, j)),
        out_shape=jax.ShapeDtypeStruct((m, n), F32),
        compiler_params=_cparams(1),
        name="modulation",
    )(c_all, w_ada, b_ada.reshape(1, n))


_C_HC, _C_GB, _C_GC, _C_Q = 0, CONV_DIM, 2 * CONV_DIM, 3 * CONV_DIM
_C_KVC = _C_Q + QZ_DIM
_C_KVS = _C_KVC + KV_COLS
_C_KVW = _C_KVS + KV_COLS
_C_GL = _C_KVW + KV_COLS
W_IN_COLS = _C_GL + LANES


def _inproj_kernel(x_ref, sc_ref, sh_ref, w_ref, u_ref, gb_ref, qz_ref, kvc_ref, kvs_ref, kvw_ref,
                   kvsb_ref, kvwb_ref, gates_ref):
    h = (x_ref[0] * (1.0 + sc_ref[0]) + sh_ref[0]).astype(BF16)
    hc = _dot(h, w_ref[:, _C_HC:_C_GB])
    gb_ref[0] = _dot(h, w_ref[:, _C_GB:_C_GC])
    gc = _dot(h, w_ref[:, _C_GC:_C_Q])
    u_ref[0] = gc * hc
    qz_ref[0] = _dot(h, w_ref[:, _C_Q:_C_KVC]).astype(qz_ref.dtype)
    kvc_ref[0] = _dot(h, w_ref[:, _C_KVC:_C_KVS])
    kvs = _dot(h, w_ref[:, _C_KVS:_C_KVW])
    kvs_ref[0] = kvs
    kvsb_ref[0] = kvs.astype(BF16)
    kvw = _dot(h, w_ref[:, _C_KVW:_C_GL])
    kvw_ref[0] = kvw
    kvwb_ref[0] = kvw.astype(BF16)
    gates_ref[0] = jax.nn.sigmoid(_dot(h, w_ref[:, _C_GL:W_IN_COLS]))


def _in_proj(x, scale, shift, w_cat, tm, qz_dtype):
    b, t, d = x.shape
    rm = scale.shape[1]
    mod_block = (1, tm, d) if rm == t else (1, 1, d)
    mod_map = (lambda i, j: (i, j, 0)) if rm == t else (lambda i, j: (i, 0, 0))
    row = lambda cols: pl.BlockSpec((1, tm, cols), lambda i, j: (i, j, 0))
    sds = lambda cols, dt: jax.ShapeDtypeStruct((b, t, cols), dt)
    return pl.pallas_call(
        _inproj_kernel,
        grid=(b, t // tm),
        in_specs=[row(d), pl.BlockSpec(mod_block, mod_map), pl.BlockSpec(mod_block, mod_map),
                  _const_spec((d, W_IN_COLS))],
        out_specs=[row(CONV_DIM), row(CONV_DIM), row(QZ_DIM), row(KV_COLS), row(KV_COLS), row(KV_COLS),
                   row(KV_COLS), row(KV_COLS), row(LANES)],
        out_shape=[sds(CONV_DIM, F32), sds(CONV_DIM, F32), sds(QZ_DIM, qz_dtype), sds(KV_COLS, F32),
                   sds(KV_COLS, F32), sds(KV_COLS, F32), sds(KV_COLS, BF16), sds(KV_COLS, BF16),
                   sds(LANES, F32)],
        compiler_params=_cparams(2),
        name="in_proj",
    )(x, scale, shift, w_cat)


def _gelu_tanh(x):
    return jax.nn.gelu(x, approximate=True)


def _cmp_bias_e(pe_ref, w1f_ref, b1_ref, e):
    pe = jnp.broadcast_to(pe_ref[e:e + 1, :], (8, pe_ref.shape[1])).astype(BF16)
    return _dot(pe, w1f_ref[e])[0:1, :] + b1_ref[e:e + 1, :]


def _compress_dense_kernel(x_ref, wbig_ref, pe_ref, w1f_ref, b1_ref, w2_ref, o_ref):
    half = TB_COLS // 2
    tb = _dot(x_ref[0].astype(BF16), wbig_ref[...])
    n = tb.shape[0]
    top = tb[:, :half]
    bot_next = pltpu.roll(tb[:, half:], n - 1, 0)
    bias = jnp.concatenate([_cmp_bias_e(pe_ref, w1f_ref, b1_ref, e) for e in (0, 0, 1, 1)], axis=1)
    hid = _gelu_tanh(top + bot_next + bias)
    o_ref[0] = _dot(hid.astype(BF16), w2_ref[...]).astype(o_ref.dtype)


def _compress_dense(kvc, wbig, pe2, w1f, b1, w2big):
    b, s, _ = kvc.shape
    nch = s // CMP_STRIDE
    x = kvc.reshape(b, nch, CHUNK_COLS)
    return pl.pallas_call(
        _compress_dense_kernel,
        grid=(b,),
        in_specs=[pl.BlockSpec((1, nch, CHUNK_COLS), lambda i: (i, 0, 0)),
                  _const_spec(wbig.shape), _const_spec(pe2.shape), _const_spec(w1f.shape),
                  _const_spec(b1.shape), _const_spec(w2big.shape)],
        out_specs=pl.BlockSpec((1, nch, KV_COLS), lambda i: (i, 0, 0)),
        out_shape=jax.ShapeDtypeStruct((b, nch, KV_COLS), BF16),
        compiler_params=_cparams(1),
        name="compress_prompt",
    )(x, wbig, pe2, w1f, b1, w2big)


PAGES_PER_STEP = 32


def _compress_paged_kernel(pt_ref, *refs, n_steps):
    g_pages = PAGES_PER_STEP
    page_refs = refs[:g_pages]
    xnew_ref, wpair_ref, wbot_ref, pe_ref, w1f_ref, b1_ref, w2_ref, o_ref, r_sc, x_sc = refs[g_pages:]
    kt = pl.program_id(1)
    page = page_refs[0].shape[2]
    rows_step = g_pages * (page // CMP_STRIDE)
    half_cols = N_KV * HEAD_DIM

    row0 = pl.multiple_of(kt * rows_step, rows_step)
    for e in range(2):
        for p in range(g_pages):
            r_sc[p * page:(p + 1) * page, :] = page_refs[p][0, e * half_cols:(e + 1) * half_cols, :].T
        for j in range(CMP_STRIDE):
            xj = r_sc[pl.ds(j, rows_step, stride=CMP_STRIDE), :]
            x_sc[e, j, pl.ds(row0, rows_step), :] = xj.astype(BF16)

    @pl.when(kt == n_steps - 1)
    def _():
        n = x_sc.shape[2]
        hid_cols = N_KV * CMP_HID
        xn = jnp.broadcast_to(xnew_ref[0], (8, CHUNK_COLS)).astype(BF16)
        bot_new = _dot(xn, wbot_ref[...])[0:1, :]
        last = _iota((n, 1), 0) == n - 1
        outs = []
        for e in range(2):
            acc = jnp.zeros((n, 2 * hid_cols), F32)
            for jp in range(CMP_STRIDE // 2):
                xs = jnp.concatenate([x_sc[e, 2 * jp], x_sc[e, 2 * jp + 1]], axis=1)
                acc = acc + _dot(xs, wpair_ref[e, jp])
            top, bot = acc[:, :hid_cols], acc[:, hid_cols:]
            bot_next = jnp.where(last, bot_new[:, e * hid_cols:(e + 1) * hid_cols], pltpu.roll(bot, n - 1, 0))
            bias = jnp.concatenate([_cmp_bias_e(pe_ref, w1f_ref, b1_ref, e)] * N_KV, axis=1)
            hid = _gelu_tanh(top + bot_next + bias)
            outs.append(_dot(hid.astype(BF16), w2_ref[e]))
        o_ref[0] = jnp.concatenate(outs, axis=1).astype(o_ref.dtype)


def _compress_paged(pool_t, page_table, xnew, wpair, wbot, pe2, w1f, b1, w2pair):
    bn, n_pages = page_table.shape
    page = pool_t.shape[2]
    g = PAGES_PER_STEP
    n_steps = n_pages // g
    assert n_pages % g == 0 and page % CMP_STRIDE == 0 and page == LANES
    n = n_pages * page // CMP_STRIDE

    def page_spec(k):
        return pl.BlockSpec((1, KV_COLS, page), lambda i, j, pt: (pt[i, j * g + k], 0, 0))

    def const(shape):
        nd = len(shape)
        return pl.BlockSpec(shape, lambda i, j, pt: (0,) * nd, pipeline_mode=pl.Buffered(1))

    grid_spec = pltpu.PrefetchScalarGridSpec(
        num_scalar_prefetch=1,
        grid=(bn, n_steps),
        in_specs=[page_spec(k) for k in range(g)]
        + [pl.BlockSpec((1, 1, CHUNK_COLS), lambda i, j, pt: (i, 0, 0)), const(wpair.shape), const(wbot.shape),
           const(pe2.shape), const(w1f.shape), const(b1.shape), const(w2pair.shape)],
        out_specs=pl.BlockSpec((1, n, KV_COLS), lambda i, j, pt: (i, 0, 0)),
        scratch_shapes=[pltpu.VMEM((g * page, LANES), F32),
                        pltpu.VMEM((2, CMP_STRIDE, n, LANES), BF16)],
    )
    return pl.pallas_call(
        functools.partial(_compress_paged_kernel, n_steps=n_steps),
        grid_spec=grid_spec,
        out_shape=jax.ShapeDtypeStruct((bn, n, KV_COLS), BF16),
        compiler_params=_cparams(2),
        name="compress_sample",
    )(page_table, *([pool_t] * g), xnew, wpair, wbot, pe2, w1f, b1, w2pair)


def _split3_dot(x, m01):
    h1 = x.astype(BF16)
    r1 = x - h1.astype(F32)
    h2 = r1.astype(BF16)
    h3 = (r1 - h2.astype(F32)).astype(BF16)
    return _dot(h1, m01) + _dot(h2, m01) + _dot(h3, m01)


def _block_scores(imp, n_lanes):
    ncp = imp.shape[1]
    c = _iota((ncp, n_lanes), 0)
    j = _iota((ncp, n_lanes), 1)
    a = jnp.where((c >= CMP_PER_SEL * j - 1) & (c <= CMP_PER_SEL * j + CMP_PER_SEL - 1), 1.0, 0.0).astype(BF16)
    return _split3_dot(imp, a)


def _ranked_scores(score, qpos, n_blocks):
    j = _iota(score.shape, 1)
    cur = qpos >> 6
    valid = j * SEL_BLOCK <= qpos
    forced = (j == 0) | (j == cur) | (j == cur - 1)
    return jnp.where(forced, -NEG, jnp.where(valid, score, NEG))


def _select_blocks_iter(score, qpos, n_blocks, n_select):
    r, l = score.shape
    removed = 3.0 * NEG
    jf = _iota((r, l), 1).astype(F32)
    sc = jnp.where(jf < n_blocks, _ranked_scores(score, qpos, n_blocks), removed)

    sel = jnp.zeros((r, l), F32)
    for _ in range(n_select):
        mx = jnp.max(sc, axis=-1, keepdims=True)
        first = jnp.min(jnp.where(sc == mx, jf, 1e9), axis=-1, keepdims=True)
        pick = jf == first
        sc, sel = jnp.where(pick, removed, sc), jnp.where(pick, 1.0, sel)
    return sel


def _select_blocks_rank(score, qpos, n_blocks, n_select):
    r, l = score.shape
    nb8 = -(-n_blocks // 8) * 8
    st = _ranked_scores(score, qpos, n_blocks).T[:nb8]
    jrow = _iota((nb8, r), 0)
    rank = jnp.zeros((nb8, r), F32)
    for jp in range(n_blocks):
        row = st[jp:jp + 1, :]
        beats = (row > st) | ((row == st) & (jrow > jp))
        rank = rank + jnp.where(beats, 1.0, 0.0)
    sel_t = jnp.where((rank < n_select) & (jrow < n_blocks), 1.0, 0.0)
    if nb8 < l:
        sel_t = jnp.concatenate([sel_t, jnp.zeros((l - nb8, r), F32)], axis=0)
    return sel_t.T


def _softmax_parts(s, mask):
    s = jnp.where(mask, s, NEG)
    m = jnp.max(s, axis=-1, keepdims=True)
    e = jnp.where(mask, jnp.exp(s - m), 0.0)
    return e, jnp.sum(e, axis=-1, keepdims=True)


SEL_TILE = 512


def _attn_prompt_kernel(qz_ref, kvs_ref, kvw_ref, kvc_ref, gates_ref, exp_ref, o_ref,
                        s_sc, p_sc, pw_sc, m_sc, acc_sc, *, seq):
    qb = Q_BLOCK
    tk = SEL_TILE
    start = pl.program_id(1) * qb
    qpos = start + _iota((qb, 1), 0)
    ncp = kvc_ref.shape[1]
    n_blocks = -(-seq // SEL_BLOCK)
    rows = Q_PER_KV * qb
    gates = gates_ref[0]
    n_tiles = (start + qb + tk - 1) // tk
    head_rows = [slice(r * qb, (r + 1) * qb) for r in range(Q_PER_KV)]
    kv_heads = range(N_KV)

    def q_rows(g):
        return jnp.concatenate(
            [qz_ref[0, :, (g * Q_PER_KV + r) * LANES:(g * Q_PER_KV + r + 1) * LANES] for r in range(Q_PER_KV)],
            axis=0)


    o_cmp, sel = [], []
    c = _iota((qb, ncp), 1)
    cmask = (CMP_STRIDE * c + (CMP_BLOCK - 1) <= qpos) & (c < ncp - 1)
    for g in kv_heads:
        s = _dot_nt(q_rows(g), kvc_ref[0, :, 0:LANES]).reshape(Q_PER_KV, qb, ncp)
        e, l = _softmax_parts(s, cmask)
        p = e / jnp.maximum(l, 1e-30)
        o_cmp.append(_dot(p.reshape(rows, ncp).astype(BF16), kvc_ref[0, :, LANES:2 * LANES]))
        imp = jnp.sum(p, axis=0)
        sel.append(_select_blocks_rank(_block_scores(imp, LANES), qpos, n_blocks,
                                       min(N_SELECT, n_blocks)).astype(BF16))

    m_sc[...] = jnp.full(m_sc.shape, NEG, F32)

    def pass1(kt, _):
        k0 = pl.multiple_of(kt * tk, tk)
        causal = (k0 + _iota((1, tk), 1)) <= qpos
        for g in kv_heads:
            s = _dot_nt(q_rows(g), kvs_ref[0, pl.ds(k0, tk), 0:LANES])
            mask = (_dot(sel[g], exp_ref[kt]) > 0.5) & causal
            for rs in head_rows:
                sr = jnp.where(mask, s[rs], NEG)
                s_sc[g, kt, rs, :] = sr
                m_sc[g, rs, :] = jnp.maximum(m_sc[g, rs, :], functools.reduce(jnp.maximum, _lane_tiles(sr)))
        return 0

    lax.fori_loop(0, n_tiles, pass1, 0)
    m_sc[...] = jnp.broadcast_to(jnp.max(m_sc[...], axis=-1, keepdims=True), m_sc.shape)
    acc_sc[...] = jnp.zeros(acc_sc.shape, F32)

    def values_and_ones(g, vv):
        return jnp.where((_iota((1, LANES), 1) >> 6) == g, vv, jnp.ones_like(vv))

    def normalised(acc):
        return acc / jnp.maximum(pltpu.roll(acc, LANES // 2, 1), 1e-30)

    def pass2(kt, _):
        k0 = pl.multiple_of(kt * tk, tk)
        for g in kv_heads:
            for rs in head_rows:
                x = s_sc[g, kt, rs, :] - jnp.concatenate([m_sc[g, rs, :]] * (tk // LANES), axis=1)
                p_sc[g, rs, :] = jnp.exp(x.astype(BF16))
            vv = values_and_ones(g, kvs_ref[0, pl.ds(k0, tk), LANES:2 * LANES])
            acc_sc[g] = acc_sc[g] + _dot(p_sc[g], vv)
        return 0

    lax.fori_loop(0, n_tiles, pass2, 0)

    wlen = WINDOW + qb
    w0 = pl.multiple_of(jnp.maximum(start - WINDOW, 0), qb)
    kpos = w0 + _iota((1, wlen), 1)
    wmask = (kpos <= qpos) & (kpos > qpos - WINDOW)
    for g in kv_heads:
        s = _dot_nt(q_rows(g), kvw_ref[0, pl.ds(w0, wlen), 0:LANES])
        for rs in head_rows:
            sr = jnp.where(wmask, s[rs], NEG)
            m = jnp.max(functools.reduce(jnp.maximum, _lane_tiles(sr)), axis=-1, keepdims=True)
            pw_sc[g, rs, :] = jnp.exp((sr - m).astype(BF16))
        o_win = normalised(_dot(pw_sc[g], values_and_ones(g, kvw_ref[0, pl.ds(w0, wlen), LANES:2 * LANES])))
        o_slc = normalised(acc_sc[g])
        own = (_iota((qb, LANES), 1) >> 6) == g
        for r, rs in enumerate(head_rows):
            h = g * Q_PER_KV + r
            o = (gates[:, h:h + 1] * o_cmp[g][rs] + gates[:, N_HEADS + h:N_HEADS + h + 1] * o_slc[rs]
                 + gates[:, 2 * N_HEADS + h:2 * N_HEADS + h + 1] * o_win[rs])
            o_ref[0, :, h * LANES:(h + 1) * LANES] = jnp.where(own, o, 0.0)


def _expand_matrix(n_tiles, tile):
    t = jnp.arange(n_tiles)[:, None, None]
    j = jnp.arange(LANES)[None, :, None]
    k = jnp.arange(tile)[None, None, :]
    return (j == (t * tile + k) // SEL_BLOCK).astype(BF16)


def _attn_prompt(qz, kvs_b, kvw_b, kv_cmpr, gates):
    b, s, _ = qz.shape
    assert s % Q_BLOCK == 0 and s % SEL_TILE == 0 and s >= WINDOW + Q_BLOCK and s // SEL_BLOCK <= LANES
    ncp = kv_cmpr.shape[1]
    n_tiles = s // SEL_TILE
    expand = _expand_matrix(n_tiles, SEL_TILE)
    rows = Q_PER_KV * Q_BLOCK
    full = lambda n: pl.BlockSpec((1, n, KV_COLS), lambda i, j: (i, 0, 0))
    return pl.pallas_call(
        functools.partial(_attn_prompt_kernel, seq=s),
        grid=(b, s // Q_BLOCK),
        in_specs=[pl.BlockSpec((1, Q_BLOCK, QZ_DIM), lambda i, j: (i, j, 0)), full(s), full(s), full(ncp),
                  pl.BlockSpec((1, Q_BLOCK, LANES), lambda i, j: (i, j, 0)), _const_spec(expand.shape)],
        out_specs=pl.BlockSpec((1, Q_BLOCK, QZ_DIM), lambda i, j: (i, j, 0)),
        out_shape=jax.ShapeDtypeStruct((b, s, QZ_DIM), F32),
        scratch_shapes=[pltpu.VMEM((N_KV, n_tiles, rows, SEL_TILE), F32), pltpu.VMEM((N_KV, rows, SEL_TILE), BF16),
                        pltpu.VMEM((N_KV, rows, WINDOW + Q_BLOCK), BF16), pltpu.VMEM((N_KV, rows, LANES), F32),
                        pltpu.VMEM((N_KV, rows, LANES), F32)],
        compiler_params=_cparams(2),
        name="attn_prompt",
    )(qz, kvs_b, kvw_b, kv_cmpr, gates, expand)


def _attn_sample_kernel(pt_ref, *refs, past, n_steps):
    g_pages = PAGES_PER_STEP
    page_refs = refs[:g_pages]
    (qz_ref, kvc_ref, gates_ref, kvs_new_ref, win_ref, kvw_new_ref, exp_ref,
     o_ref, q_sc, sel_sc, s_sc, v_sc, snew_sc, mp_sc, ocw_sc) = refs[g_pages:]
    kt = pl.program_id(1)
    t = qz_ref.shape[1]
    rows = N_HEADS * t
    page = page_refs[0].shape[2]
    tile = g_pages * page
    blocks_per_tile = tile // SEL_BLOCK
    n_past_blocks = past // SEL_BLOCK
    n_blocks = n_past_blocks + -(-t // SEL_BLOCK)
    sel_lanes = sel_sc.shape[0] * LANES
    half_cols = N_KV * HEAD_DIM
    qpos = past + _iota((t, 1), 0)
    qpos_gt = jnp.concatenate([qpos] * N_KV, axis=0)
    own_rows = _iota((N_KV, Q_PER_KV, t, LANES), 0) == (_iota((N_KV, Q_PER_KV, t, LANES), 3) >> 6)

    def new_rows_padded():
        kvn = kvs_new_ref[0]
        return jnp.concatenate([kvn, jnp.zeros((LANES - t, KV_COLS), F32)], axis=0).astype(BF16)

    def masked_scores(s, mask):
        k = s.shape[1]
        return jnp.where(mask.reshape(N_KV, 1, t, k), s.reshape(N_KV, Q_PER_KV, t, k), NEG).reshape(rows, k)

    @pl.when(kt == 0)
    def _():
        q = jnp.concatenate([qz_ref[0, :, h * LANES:(h + 1) * LANES] for h in range(N_HEADS)], axis=0)
        q_sc[...] = q.astype(BF16)
        qb = q_sc[...]
        ncp = kvc_ref.shape[1]
        s = _dot_nt(qb, kvc_ref[0, :, 0:LANES]).reshape(N_HEADS, t, ncp)
        c = _iota((t, ncp), 1)
        cmask = CMP_STRIDE * c + (CMP_BLOCK - 1) <= qpos
        e, l = _softmax_parts(s, cmask)
        p = e / jnp.maximum(l, 1e-30)
        o_cmp = _dot(p.reshape(rows, ncp).astype(BF16), kvc_ref[0, :, LANES:2 * LANES])
        imp = jnp.sum(p.reshape(N_KV, Q_PER_KV, t, ncp), axis=1).reshape(N_KV * t, ncp)
        sel = _select_blocks_iter(_block_scores(imp, sel_lanes), qpos_gt, n_blocks, min(N_SELECT, n_blocks))
        for w in range(sel_sc.shape[0]):
            sel_sc[w] = sel[:, w * LANES:(w + 1) * LANES]
        wb = win_ref.shape[2]
        win = win_ref[0].astype(BF16)
        kvw_new = kvw_new_ref[0].astype(BF16)
        s_old = _dot(qb, win[0:half_cols]).reshape(N_HEADS, t, wb)
        s_new = _dot_nt(qb, kvw_new[:, 0:half_cols]).reshape(N_HEADS, t, t)
        kpos_old = past - wb + _iota((1, wb), 1)
        kpos_new = past + _iota((1, t), 1)
        mask_old = (kpos_old <= qpos) & (kpos_old > qpos - WINDOW) & (kpos_old >= 0)
        mask_new = (kpos_new <= qpos) & (kpos_new > qpos - WINDOW)
        s_old = jnp.where(mask_old, s_old, NEG)
        s_new = jnp.where(mask_new, s_new, NEG)
        m = jnp.maximum(jnp.max(s_old, axis=-1, keepdims=True), jnp.max(s_new, axis=-1, keepdims=True))
        e_old = jnp.where(mask_old, jnp.exp(s_old - m), 0.0)
        e_new = jnp.where(mask_new, jnp.exp(s_new - m), 0.0)
        l = jnp.sum(e_old, axis=-1, keepdims=True) + jnp.sum(e_new, axis=-1, keepdims=True)
        o_win = (_dot_nt(e_old.reshape(rows, wb).astype(BF16), win[half_cols:])
                 + _dot(e_new.reshape(rows, t).astype(BF16), kvw_new[:, half_cols:]))
        o_win = o_win.reshape(N_HEADS, t, LANES) / jnp.maximum(l, 1e-30)
        gts = gates_ref[0]
        gc = jnp.stack([jnp.broadcast_to(gts[:, h:h + 1], (t, LANES)) for h in range(N_HEADS)])
        gw = jnp.stack([jnp.broadcast_to(gts[:, 2 * N_HEADS + h:2 * N_HEADS + h + 1], (t, LANES))
                        for h in range(N_HEADS)])
        ocw_sc[...] = gc * o_cmp.reshape(N_HEADS, t, LANES) + gw * o_win
        jn = (past + _iota((1, LANES), 1)) >> 6
        seln = jnp.zeros((N_KV * t, LANES), F32)
        for jb in range(n_past_blocks, n_blocks):
            seln = jnp.where(jn == jb, sel[:, jb:jb + 1], seln)
        npos = past + _iota((1, LANES), 1)
        nmask = (seln > 0.5) & (npos <= qpos_gt) & (npos < past + t)
        s_n = masked_scores(_dot_nt(qb, new_rows_padded()[:, 0:half_cols]), nmask)
        snew_sc[...] = s_n
        mp_sc[...] = s_n

    k_t = jnp.concatenate([r[0, 0:half_cols, :] for r in page_refs], axis=1).astype(BF16)
    v_sc[kt] = jnp.concatenate([r[0, half_cols:, :] for r in page_refs], axis=1).astype(BF16)
    b0 = kt * blocks_per_tile
    selw = sel_sc[b0 // LANES].astype(BF16)
    kpos = kt * tile + _iota((1, tile), 1)
    mask = (_dot(selw, exp_ref[0]) > 0.5) & (kpos <= qpos_gt)
    s = masked_scores(_dot(q_sc[...], k_t), mask)
    s_sc[kt] = s
    mp_sc[...] = jnp.maximum(mp_sc[...], functools.reduce(jnp.maximum, _lane_tiles(s)))

    @pl.when(kt == n_steps - 1)
    def _():
        m = jnp.max(mp_sc[...], axis=-1, keepdims=True)
        e = jnp.exp(snew_sc[...] - m)
        lp = e
        acc = _dot(e.astype(BF16), new_rows_padded()[:, half_cols:])
        for k2 in range(n_steps):
            e = jnp.exp(s_sc[k2] - m)
            lp = lp + functools.reduce(jnp.add, _lane_tiles(e))
            acc = acc + _dot_nt(e.astype(BF16), v_sc[k2])
        l = jnp.sum(lp, axis=-1, keepdims=True)
        gts = gates_ref[0]
        gs = jnp.stack([jnp.broadcast_to(gts[:, N_HEADS + h:N_HEADS + h + 1], (t, LANES))
                        for h in range(N_HEADS)])
        o_slc = (acc / jnp.maximum(l, 1e-30)).reshape(N_HEADS, t, LANES)
        o = jnp.where(own_rows.reshape(N_HEADS, t, LANES), ocw_sc[...] + gs * o_slc, 0.0)
        for h in range(N_HEADS):
            o_ref[0, :, h * LANES:(h + 1) * LANES] = o[h]


def _attn_sample(qz, kv_cmpr, gates, kvs_new, slc_pool_t, win_t, kvw_new, page_table):
    bn, t, _ = qz.shape
    n_pages = page_table.shape[1]
    page = slc_pool_t.shape[2]
    past = n_pages * page
    g = PAGES_PER_STEP
    n_steps = n_pages // g
    tile = g * page
    blocks_per_tile = tile // SEL_BLOCK
    assert n_pages % g == 0 and page % SEL_BLOCK == 0 and LANES % blocks_per_tile == 0 and t % 8 == 0
    n_blocks = past // SEL_BLOCK + -(-t // SEL_BLOCK)
    sel_groups = -(-n_blocks // LANES)
    ncp = kv_cmpr.shape[1]
    wb = win_t.shape[2]
    n_var = LANES // blocks_per_tile
    expand = _expand_matrix(n_var, tile)

    def page_spec(k):
        return pl.BlockSpec((1, KV_COLS, page), lambda i, j, pt: (pt[i, j * g + k], 0, 0))

    per_b = lambda n, cols: pl.BlockSpec((1, n, cols), lambda i, j, pt: (i, 0, 0))
    grid_spec = pltpu.PrefetchScalarGridSpec(
        num_scalar_prefetch=1,
        grid=(bn, n_steps),
        in_specs=[page_spec(k) for k in range(g)]
        + [per_b(t, QZ_DIM), per_b(ncp, KV_COLS), per_b(t, LANES), per_b(t, KV_COLS), per_b(KV_COLS, wb),
           per_b(t, KV_COLS), pl.BlockSpec((1, LANES, tile), lambda i, j, pt: (j % n_var, 0, 0))],
        out_specs=per_b(t, QZ_DIM),
        scratch_shapes=[pltpu.VMEM((N_HEADS * t, LANES), BF16),
                        pltpu.VMEM((sel_groups, N_KV * t, LANES), F32),
                        pltpu.VMEM((n_steps, N_HEADS * t, tile), F32), pltpu.VMEM((n_steps, LANES, tile), BF16),
                        pltpu.VMEM((N_HEADS * t, LANES), F32), pltpu.VMEM((N_HEADS * t, LANES), F32),
                        pltpu.VMEM((N_HEADS, t, LANES), F32)],
    )
    return pl.pallas_call(
        functools.partial(_attn_sample_kernel, past=past, n_steps=n_steps),
        grid_spec=grid_spec,
        out_shape=jax.ShapeDtypeStruct((bn, t, QZ_DIM), F32),
        compiler_params=_cparams(2),
        name="attn_sample",
    )(page_table, *([slc_pool_t] * g), qz, kv_cmpr, gates, kvs_new, win_t, kvw_new, expand)


def _layernorm(x, g, b):
    mu = jnp.mean(x, axis=-1, keepdims=True)
    xc = x - mu
    var = jnp.mean(xc * xc, axis=-1, keepdims=True)
    return xc * lax.rsqrt(var + EPS) * g + b


def _post_kernel(x_ref, o_ref, u_ref, uprev_ref, gb_ref, st1_ref, st2_ref, g1_ref, sh2_ref, sc2_ref, g2_ref,
                 convw_ref, gconv_ref, gnsa_ref, woc_ref, won_ref, ln1g_ref, ln1b_ref, ln2g_ref, ln2b_ref,
                 wg_ref, wu_ref, wd_ref, y_ref, *, seq):
    tm = x_ref.shape[1]
    u = u_ref[0]
    ext = jnp.concatenate([uprev_ref[0], u], axis=0)
    if tm <= seq:
        pos = (pl.program_id(1) * tm) % seq + _iota((tm, 1), 0)
    else:
        pos = lax.rem(_iota((tm, 1), 0), seq)
    p1 = jnp.where(pos >= 1, ext[7:7 + tm], st1_ref[0])
    p2 = jnp.where(pos >= 2, ext[6:6 + tm], st2_ref[0])
    cw = convw_ref[...]
    y_c = gb_ref[0] * (cw[0:1] * p2 + cw[1:2] * p1 + cw[2:3] * u)
    yn = y_c * lax.rsqrt(jnp.mean(y_c * y_c, axis=-1, keepdims=True) + EPS) * gconv_ref[...]
    o = o_ref[0]
    on = o * lax.rsqrt(jnp.sum(o * o, axis=-1, keepdims=True) * (1.0 / NSA_DIM) + EPS) * gnsa_ref[...]
    mix = _dot(yn.astype(BF16), woc_ref[...]) + _dot(on.astype(BF16), won_ref[...])
    x1 = _layernorm(ALPHA * x_ref[0] + g1_ref[0] * mix, ln1g_ref[...], ln1b_ref[...])
    h = (x1 * (1.0 + sc2_ref[0]) + sh2_ref[0]).astype(BF16)
    a = _dot(h, wg_ref[...])
    f = (a * jax.nn.sigmoid(a)) * _dot(h, wu_ref[...])
    f = _dot(f.astype(BF16), wd_ref[...])
    y_ref[0] = _layernorm(ALPHA * x1 + g2_ref[0] * f, ln2g_ref[...], ln2b_ref[...])


def _post(x, o, u, gb, st1, st2, mods, w, tm, seq):
    b, t, d = x.shape
    row = lambda cols: pl.BlockSpec((1, tm, cols), lambda i, j: (i, j, 0))

    def bcast(a):
        if a.shape[1] == t:
            return pl.BlockSpec((1, tm, a.shape[2]), lambda i, j: (i, j, 0))
        return pl.BlockSpec((1, 1, a.shape[2]), lambda i, j: (i, 0, 0))

    prev = pl.BlockSpec((1, 8, CONV_DIM), lambda i, j: (i, jnp.maximum(j * (tm // 8) - 1, 0), 0))
    consts = [w["conv_w"], w["g_conv"], w["g_nsa"], w["wo_c"], w["wo_n"], w["ln1_g"], w["ln1_b"], w["ln2_g"],
              w["ln2_b"], w["w_gate"], w["w_up"], w["w_down"]]
    return pl.pallas_call(
        functools.partial(_post_kernel, seq=seq),
        grid=(b, t // tm),
        in_specs=[row(d), row(QZ_DIM), row(CONV_DIM), prev, row(CONV_DIM), bcast(st1), bcast(st2)]
        + [bcast(m) for m in mods] + [_const_spec(c.shape) for c in consts],
        out_specs=row(d),
        out_shape=jax.ShapeDtypeStruct((b, t, d), F32),
        compiler_params=_cparams(2),
        name="post_block",
    )(x, o, u, u, gb, st1, st2, *mods, *consts)


def _prep_w_in(w_in):
    d = w_in.shape[0]
    c3 = 3 * CONV_DIM
    splits = (CONV_DIM, 2 * CONV_DIM, c3, c3 + NSA_DIM, c3 + NSA_DIM + KV_COLS, c3 + NSA_DIM + 2 * KV_COLS,
              c3 + NSA_DIM + 3 * KV_COLS)
    hc, gb, gc, q, kvc, kvs, kvw, gl = jnp.split(w_in, splits, axis=1)
    q4 = q.reshape(d, N_KV, Q_PER_KV, HEAD_DIM) * SCALE
    qz = jnp.einsum("dgrh,gk->dgrkh", q4, jnp.eye(N_KV, dtype=w_in.dtype)).reshape(d, QZ_DIM)
    glp = jnp.pad(gl, ((0, 0), (0, LANES - gl.shape[1])))
    return jnp.concatenate([hc, gb, gc, qz, kvc, kvs, kvw, glp], axis=1).astype(BF16)


def _prep_cmp(cmp_pe, cmp_w1, cmp_b1, cmp_w2):
    eye = jnp.eye(N_KV, dtype=cmp_w1.dtype)
    eye2 = jnp.eye(2, dtype=cmp_w1.dtype)
    w1r = cmp_w1.reshape(2, 2, CMP_STRIDE, HEAD_DIM, CMP_HID)
    wbig = jnp.einsum("etjdh,ef,gk->jegdtfkh", w1r, eye2, eye).reshape(CHUNK_COLS, TB_COLS).astype(BF16)
    w2big = jnp.einsum("ehd,ef,gk->eghfkd", cmp_w2, eye2, eye).reshape(TB_COLS // 2, KV_COLS).astype(BF16)
    w1p = cmp_w1.reshape(2, 2, CMP_STRIDE // 2, 2, HEAD_DIM, CMP_HID)
    wpair = jnp.einsum("etpjdh,gk->epjgdtkh", w1p, eye).reshape(
        2, CMP_STRIDE // 2, 2 * N_KV * HEAD_DIM, 2 * N_KV * CMP_HID).astype(BF16)
    w2pair = jnp.einsum("ehd,gk->eghkd", cmp_w2, eye).reshape(2, N_KV * CMP_HID, N_KV * HEAD_DIM).astype(BF16)
    pe2 = cmp_pe.reshape(2, CMP_BLOCK * HEAD_DIM)
    w1f = cmp_w1.reshape(2, CMP_BLOCK * HEAD_DIM, CMP_HID).astype(BF16)
    dense = (wbig, pe2, w1f, cmp_b1, w2big)
    paged = (wpair, wbig[:, TB_COLS // 2:], pe2, w1f, cmp_b1, w2pair)
    return dense, paged


def _prep_post(conv_w, g_conv, g_nsa, w_o, ln1_g, ln1_b, ln2_g, ln2_b, w_gate, w_up, w_down):
    d = w_o.shape[1]
    own = jnp.eye(N_KV, dtype=w_o.dtype)
    g_nsa_z = jnp.einsum("grh,gk->grkh", g_nsa.reshape(N_KV, Q_PER_KV, HEAD_DIM), own).reshape(1, QZ_DIM)
    wo_n = jnp.einsum("grhd,gk->grkhd", w_o[CONV_DIM:].reshape(N_KV, Q_PER_KV, HEAD_DIM, d), own)
    row = lambda a: a.reshape(1, -1)
    return dict(conv_w=conv_w, g_conv=row(g_conv), g_nsa=g_nsa_z, wo_c=w_o[:CONV_DIM].astype(BF16),
                wo_n=wo_n.reshape(QZ_DIM, d).astype(BF16), ln1_g=row(ln1_g), ln1_b=row(ln1_b),
                ln2_g=row(ln2_g), ln2_b=row(ln2_b), w_gate=w_gate.astype(BF16), w_up=w_up.astype(BF16),
                w_down=w_down.astype(BF16))


def _feature_major(cache):
    n, rows = cache.shape[:2]
    return jnp.transpose(cache, (0, 2, 3, 4, 1)).reshape(n, KV_COLS, rows)


def kernel(x_prompt, x_sample, cache_cmp_kv, cache_slc_kv, cache_win_kv, state_conv, page_table, c_prompt,
           c_sample, w_ada, b_ada, w_in, conv_w, cmp_pe, cmp_w1, cmp_b1, cmp_w2, g_conv_out, g_nsa_out, w_o,
           ln1_g, ln1_b, ln2_g, ln2_b, w_ffn_gate, w_ffn_up, w_ffn_down):
    assert w_ada.shape[0] == DEPTH
    bp, s, d = x_prompt.shape
    bs, t, _ = x_sample.shape
    kv_shape = (2, N_KV, HEAD_DIM)

    w_cat = _prep_w_in(w_in[0])
    cmp_dense_w, cmp_paged_w = _prep_cmp(cmp_pe[0], cmp_w1[0], cmp_b1[0], cmp_w2[0])
    post_w = _prep_post(conv_w[0], g_conv_out[0], g_nsa_out[0], w_o[0], ln1_g[0], ln1_b[0], ln2_g[0], ln2_b[0],
                        w_ffn_gate[0], w_ffn_up[0], w_ffn_down[0])

    mod = _modulation(jnp.concatenate([c_prompt, c_sample], axis=0), w_ada[0], b_ada[0])
    mods_p = [m[:, None, :] for m in jnp.split(mod[:bp], 6, axis=-1)]
    mods_s = [jnp.repeat(m, t, axis=0)[None] for m in jnp.split(mod[bp:], 6, axis=-1)]

    u, gb, qz, kvc, kvs, kvw, kvs_b, kvw_b, gates = _in_proj(x_prompt, mods_p[1], mods_p[0], w_cat, 512, BF16)
    kv_cmpr = _compress_dense(kvc, *cmp_dense_w)
    o = _attn_prompt(qz, kvs_b, kvw_b, kv_cmpr, gates)
    zero_state = jnp.zeros((bp, 1, CONV_DIM), F32)
    y_prompt = _post(x_prompt, o, u, gb, zero_state, zero_state, mods_p[2:], post_w, 512, s)
    w_keep = min(WINDOW, s)
    cmp_p = kvc.reshape(1, bp, s, *kv_shape)
    slc_p = kvs.reshape(1, bp, s, *kv_shape)
    win_p = kvw[:, s - w_keep:].reshape(1, bp, w_keep, *kv_shape)
    conv_p = u[:, s - (CONV_W - 1):][None]

    rows = bs * t
    us, gbs, qzs, kvcs, kvss, kvws, _, _, gates_s = _in_proj(
        x_sample.reshape(1, rows, d), mods_s[1], mods_s[0], w_cat, rows, F32)
    per_b = lambda a: a.reshape(bs, t, a.shape[-1])
    kvcs, kvss, kvws = per_b(kvcs), per_b(kvss), per_b(kvws)
    xnew = jnp.pad(kvcs, ((0, 0), (0, (-t) % CMP_STRIDE), (0, 0))).reshape(bs, 1, CHUNK_COLS)
    kv_cmpr_s = _compress_paged(_feature_major(cache_cmp_kv[0]), page_table, xnew, *cmp_paged_w)
    o_s = _attn_sample(per_b(qzs), kv_cmpr_s, per_b(gates_s), kvss, _feature_major(cache_slc_kv[0]),
                       _feature_major(cache_win_kv[0]), kvws, page_table)
    st = state_conv[0]
    tpos = jnp.arange(t)[None, :, None]
    st1 = jnp.where(tpos == 0, st[:, 1:2], 0.0).reshape(1, rows, CONV_DIM)
    st2 = jnp.where(tpos == 0, st[:, 0:1], jnp.where(tpos == 1, st[:, 1:2], 0.0)).reshape(1, rows, CONV_DIM)
    y_sample = _post(x_sample.reshape(1, rows, d), o_s.reshape(1, rows, QZ_DIM), us, gbs, st1, st2, mods_s[2:],
                     post_w, rows, t).reshape(bs, t, d)
    cmp_s = kvcs.reshape(1, bs, t, *kv_shape)
    slc_s = kvss.reshape(1, bs, t, *kv_shape)
    win_all = jnp.concatenate([cache_win_kv[0], kvws.reshape(bs, t, *kv_shape)], axis=1)
    win_s = win_all[:, t:][None]
    conv_s = jnp.concatenate([st, per_b(us)], axis=1)[:, t:][None]
    return (y_prompt, y_sample, cmp_p, slc_p, win_p, conv_p, cmp_s, slc_s, win_s, conv_s)
```

```python
import functools

import jax
import jax.numpy as jnp
from jax import lax
from jax.experimental import pallas as pl
from jax.experimental.pallas import tpu as pltpu

HEAD_DIM = 64
N_KV = 2
Q_PER_KV = 6
N_HEADS = N_KV * Q_PER_KV
CONV_DIM = 256
CONV_W = 3
NSA_DIM = N_HEADS * HEAD_DIM
KV_COLS = 2 * N_KV * HEAD_DIM
CMP_BLOCK = 32
CMP_STRIDE = 16
CMP_HID = 2 * HEAD_DIM
SEL_BLOCK = 64
CMP_PER_SEL = SEL_BLOCK // CMP_STRIDE
N_SELECT = 16
WINDOW = 512
Q_BLOCK = 128
DEPTH = 1
ALPHA = (2 * DEPTH) ** 0.25
EPS = 1e-5
SCALE = HEAD_DIM ** -0.5

LANES = 128
QZ_DIM = N_HEADS * LANES
CHUNK_COLS = CMP_STRIDE * KV_COLS
TB_COLS = 2 * 2 * N_KV * CMP_HID
NEG = -1e30
VMEM_LIMIT = 56 * 1024 * 1024

F32 = jnp.float32
BF16 = jnp.bfloat16


def _cparams(n_grid):
    return pltpu.CompilerParams(dimension_semantics=("arbitrary",) * n_grid, vmem_limit_bytes=VMEM_LIMIT)


def _const_spec(shape):
    nd = len(shape)
    return pl.BlockSpec(shape, lambda *_: (0,) * nd, pipeline_mode=pl.Buffered(1))


def _dot(a, b):
    return jnp.dot(a, b, preferred_element_type=F32)


def _dot_nt(a, b):
    return lax.dot_general(a, b, (((1,), (1,)), ((), ())), preferred_element_type=F32)


def _iota(shape, dim):
    return lax.broadcasted_iota(jnp.int32, shape, dim)


def _lane_tiles(x):
    return [x[:, w * LANES:(w + 1) * LANES] for w in range(x.shape[1] // LANES)]


def _mod_kernel(c_ref, w_ref, b_ref, o_ref):
    c = c_ref[...]
    a = (c * jax.nn.sigmoid(c)).astype(BF16)
    o_ref[...] = _dot(a, w_ref[...].astype(BF16)) + b_ref[...]


def _modulation(c_all, w_ada, b_ada):
    m, d = c_all.shape
    n = w_ada.shape[1]
    tn = 512
    return pl.pallas_call(
        _mod_kernel,
        grid=(n // tn,),
        in_specs=[pl.BlockSpec((m, d), lambda j: (0, 0)),
                  pl.BlockSpec((d, tn), lambda j: (0, j)),
                  pl.BlockSpec((1, tn), lambda j: (0, j))],
        out_specs=pl.BlockSpec((m, tn), lambda j: (0, j)),
        out_shape=jax.ShapeDtypeStruct((m, n), F32),
        compiler_params=_cparams(1),
        name="modulation",
    )(c_all, w_ada, b_ada.reshape(1, n))


_C_HC, _C_GB, _C_GC, _C_Q = 0, CONV_DIM, 2 * CONV_DIM, 3 * CONV_DIM
_C_KVC = _C_Q + QZ_DIM
_C_KVS = _C_KVC + KV_COLS
_C_KVW = _C_KVS + KV_COLS
_C_GL = _C_KVW + KV_COLS
W_IN_COLS = _C_GL + LANES


def _inproj_kernel(x_ref, sc_ref, sh_ref, w_ref, u_ref, gb_ref, qz_ref, kvc_ref, kvs_ref, kvw_ref,
                   kvsb_ref, kvwb_ref, gates_ref):
    h = (x_ref[0] * (1.0 + sc_ref[0]) + sh_ref[0]).astype(BF16)
    hc = _dot(h, w_ref[:, _C_HC:_C_GB])
    gb_ref[0] = _dot(h, w_ref[:, _C_GB:_C_GC])
    gc = _dot(h, w_ref[:, _C_GC:_C_Q])
    u_ref[0] = gc * hc
    qz_ref[0] = _dot(h, w_ref[:, _C_Q:_C_KVC]).astype(qz_ref.dtype)
    kvc_ref[0] = _dot(h, w_ref[:, _C_KVC:_C_KVS])
    kvs = _dot(h, w_ref[:, _C_KVS:_C_KVW])
    kvs_ref[0] = kvs
    kvsb_ref[0] = kvs.astype(BF16)
    kvw = _dot(h, w_ref[:, _C_KVW:_C_GL])
    kvw_ref[0] = kvw
    kvwb_ref[0] = kvw.astype(BF16)
    gates_ref[0] = jax.nn.sigmoid(_dot(h, w_ref[:, _C_GL:W_IN_COLS]))


def _in_proj(x, scale, shift, w_cat, tm, qz_dtype):
    b, t, d = x.shape
    rm = scale.shape[1]
    mod_block = (1, tm, d) if rm == t else (1, 1, d)
    mod_map = (lambda i, j: (i, j, 0)) if rm == t else (lambda i, j: (i, 0, 0))
    row = lambda cols: pl.BlockSpec((1, tm, cols), lambda i, j: (i, j, 0))
    sds = lambda cols, dt: jax.ShapeDtypeStruct((b, t, cols), dt)
    return pl.pallas_call(
        _inproj_kernel,
        grid=(b, t // tm),
        in_specs=[row(d), pl.BlockSpec(mod_block, mod_map), pl.BlockSpec(mod_block, mod_map),
                  _const_spec((d, W_IN_COLS))],
        out_specs=[row(CONV_DIM), row(CONV_DIM), row(QZ_DIM), row(KV_COLS), row(KV_COLS), row(KV_COLS),
                   row(KV_COLS), row(KV_COLS), row(LANES)],
        out_shape=[sds(CONV_DIM, F32), sds(CONV_DIM, F32), sds(QZ_DIM, qz_dtype), sds(KV_COLS, F32),
                   sds(KV_COLS, F32), sds(KV_COLS, F32), sds(KV_COLS, BF16), sds(KV_COLS, BF16),
                   sds(LANES, F32)],
        compiler_params=_cparams(2),
        name="in_proj",
    )(x, scale, shift, w_cat)


def _gelu_tanh(x):
    return jax.nn.gelu(x, approximate=True)


def _cmp_bias_e(pe_ref, w1f_ref, b1_ref, e):
    pe = jnp.broadcast_to(pe_ref[e:e + 1, :], (8, pe_ref.shape[1])).astype(BF16)
    return _dot(pe, w1f_ref[e])[0:1, :] + b1_ref[e:e + 1, :]


def _compress_dense_kernel(x_ref, wbig_ref, pe_ref, w1f_ref, b1_ref, w2_ref, o_ref):
    half = TB_COLS // 2
    tb = _dot(x_ref[0].astype(BF16), wbig_ref[...])
    n = tb.shape[0]
    top = tb[:, :half]
    bot_next = pltpu.roll(tb[:, half:], n - 1, 0)
    bias = jnp.concatenate([_cmp_bias_e(pe_ref, w1f_ref, b1_ref, e) for e in (0, 0, 1, 1)], axis=1)
    hid = _gelu_tanh(top + bot_next + bias)
    o_ref[0] = _dot(hid.astype(BF16), w2_ref[...]).astype(o_ref.dtype)


def _compress_dense(kvc, wbig, pe2, w1f, b1, w2big):
    b, s, _ = kvc.shape
    nch = s // CMP_STRIDE
    x = kvc.reshape(b, nch, CHUNK_COLS)
    return pl.pallas_call(
        _compress_dense_kernel,
        grid=(b,),
        in_specs=[pl.BlockSpec((1, nch, CHUNK_COLS), lambda i: (i, 0, 0)),
                  _const_spec(wbig.shape), _const_spec(pe2.shape), _const_spec(w1f.shape),
                  _const_spec(b1.shape), _const_spec(w2big.shape)],
        out_specs=pl.BlockSpec((1, nch, KV_COLS), lambda i: (i, 0, 0)),
        out_shape=jax.ShapeDtypeStruct((b, nch, KV_COLS), BF16),
        compiler_params=_cparams(1),
        name="compress_prompt",
    )(x, wbig, pe2, w1f, b1, w2big)


PAGES_PER_STEP = 32


def _compress_paged_kernel(pt_ref, *refs, n_steps):
    g_pages = PAGES_PER_STEP
    page_refs = refs[:g_pages]
    xnew_ref, wpair_ref, wbot_ref, pe_ref, w1f_ref, b1_ref, w2_ref, o_ref, r_sc, tb_sc = refs[g_pages:]
    kt = pl.program_id(1)
    page = page_refs[0].shape[2]
    rows_step = g_pages * (page // CMP_STRIDE)
    half_cols = N_KV * HEAD_DIM
    hid_cols = N_KV * CMP_HID

    row0 = pl.multiple_of(kt * rows_step, rows_step)
    for e in range(2):
        for p in range(g_pages):
            r_sc[e, p * page:(p + 1) * page, :] = page_refs[p][0, e * half_cols:(e + 1) * half_cols, :].T
        acc = None
        for jp in range(CMP_STRIDE // 2):
            xs = jnp.concatenate([r_sc[e, pl.ds(2 * jp + jj, rows_step, stride=CMP_STRIDE), :] for jj in (0, 1)],
                                 axis=1).astype(BF16)
            part = _dot(xs, wpair_ref[e, jp])
            acc = part if acc is None else acc + part
        tb_sc[e, pl.ds(row0, rows_step), :] = acc

    @pl.when(kt == n_steps - 1)
    def _():
        n = tb_sc.shape[1]
        xn = jnp.broadcast_to(xnew_ref[0], (8, CHUNK_COLS)).astype(BF16)
        bot_new = _dot(xn, wbot_ref[...])[0:1, :]
        last = _iota((n, 1), 0) == n - 1
        outs = []
        for e in range(2):
            top, bot = tb_sc[e, :, :hid_cols], tb_sc[e, :, hid_cols:]
            bot_next = jnp.where(last, bot_new[:, e * hid_cols:(e + 1) * hid_cols], pltpu.roll(bot, n - 1, 0))
            bias = jnp.concatenate([_cmp_bias_e(pe_ref, w1f_ref, b1_ref, e)] * N_KV, axis=1)
            hid = _gelu_tanh(top + bot_next + bias)
            outs.append(_dot(hid.astype(BF16), w2_ref[e]))
        o_ref[0] = jnp.concatenate(outs, axis=1).astype(o_ref.dtype)


def _compress_paged(pool_t, page_table, xnew, wpair, wbot, pe2, w1f, b1, w2pair):
    bn, n_pages = page_table.shape
    page = pool_t.shape[2]
    g = PAGES_PER_STEP
    n_steps = n_pages // g
    assert n_pages % g == 0 and page % CMP_STRIDE == 0 and page == LANES
    n = n_pages * page // CMP_STRIDE

    def page_spec(k):
        return pl.BlockSpec((1, KV_COLS, page), lambda i, j, pt: (pt[i, j * g + k], 0, 0))

    def const(shape):
        nd = len(shape)
        return pl.BlockSpec(shape, lambda i, j, pt: (0,) * nd, pipeline_mode=pl.Buffered(1))

    grid_spec = pltpu.PrefetchScalarGridSpec(
        num_scalar_prefetch=1,
        grid=(bn, n_steps),
        in_specs=[page_spec(k) for k in range(g)]
        + [pl.BlockSpec((1, 1, CHUNK_COLS), lambda i, j, pt: (i, 0, 0)), const(wpair.shape), const(wbot.shape),
           const(pe2.shape), const(w1f.shape), const(b1.shape), const(w2pair.shape)],
        out_specs=pl.BlockSpec((1, n, KV_COLS), lambda i, j, pt: (i, 0, 0)),
        scratch_shapes=[pltpu.VMEM((2, g * page, LANES), F32),
                        pltpu.VMEM((2, n, 2 * N_KV * CMP_HID), F32)],
    )
    return pl.pallas_call(
        functools.partial(_compress_paged_kernel, n_steps=n_steps),
        grid_spec=grid_spec,
        out_shape=jax.ShapeDtypeStruct((bn, n, KV_COLS), BF16),
        compiler_params=_cparams(2),
        name="compress_sample",
    )(page_table, *([pool_t] * g), xnew, wpair, wbot, pe2, w1f, b1, w2pair)


def _split3_dot(x, m01):
    h1 = x.astype(BF16)
    r1 = x - h1.astype(F32)
    h2 = r1.astype(BF16)
    h3 = (r1 - h2.astype(F32)).astype(BF16)
    return _dot(h1, m01) + _dot(h2, m01) + _dot(h3, m01)


def _block_scores(imp, n_lanes):
    ncp = imp.shape[1]
    c = _iota((ncp, n_lanes), 0)
    j = _iota((ncp, n_lanes), 1)
    a = jnp.where((c >= CMP_PER_SEL * j - 1) & (c <= CMP_PER_SEL * j + CMP_PER_SEL - 1), 1.0, 0.0).astype(BF16)
    return _split3_dot(imp, a)


def _ranked_scores(score, qpos, n_blocks):
    j = _iota(score.shape, 1)
    cur = qpos >> 6
    valid = j * SEL_BLOCK <= qpos
    forced = (j == 0) | (j == cur) | (j == cur - 1)
    return jnp.where(forced, -NEG, jnp.where(valid, score, NEG))


def _select_blocks_iter(score, qpos, n_blocks, n_select):
    r, l = score.shape
    removed = 3.0 * NEG
    jf = _iota((r, l), 1).astype(F32)
    sc = jnp.where(jf < n_blocks, _ranked_scores(score, qpos, n_blocks), removed)

    sel = jnp.zeros((r, l), F32)
    for _ in range(n_select):
        mx = jnp.max(sc, axis=-1, keepdims=True)
        first = jnp.min(jnp.where(sc == mx, jf, 1e9), axis=-1, keepdims=True)
        pick = jf == first
        sc, sel = jnp.where(pick, removed, sc), jnp.where(pick, 1.0, sel)
    return sel


def _select_blocks_rank(score, qpos, n_blocks, n_select):
    r, l = score.shape
    nb8 = -(-n_blocks // 8) * 8
    st = _ranked_scores(score, qpos, n_blocks).T[:nb8]
    jrow = _iota((nb8, r), 0)
    rank = jnp.zeros((nb8, r), F32)
    for jp in range(n_blocks):
        row = st[jp:jp + 1, :]
        beats = (row > st) | ((row == st) & (jrow > jp))
        rank = rank + jnp.where(beats, 1.0, 0.0)
    sel_t = jnp.where((rank < n_select) & (jrow < n_blocks), 1.0, 0.0)
    if nb8 < l:
        sel_t = jnp.concatenate([sel_t, jnp.zeros((l - nb8, r), F32)], axis=0)
    return sel_t.T


def _softmax_parts(s, mask):
    s = jnp.where(mask, s, NEG)
    m = jnp.max(s, axis=-1, keepdims=True)
    e = jnp.where(mask, jnp.exp(s - m), 0.0)
    return e, jnp.sum(e, axis=-1, keepdims=True)


SEL_TILE = 512


def _attn_prompt_kernel(qz_ref, kvs_ref, kvw_ref, kvc_ref, gates_ref, exp_ref, o_ref,
                        s_sc, p_sc, pw_sc, m_sc, acc_sc, *, seq):
    qb = Q_BLOCK
    tk = SEL_TILE
    start = pl.program_id(1) * qb
    qpos = start + _iota((qb, 1), 0)
    ncp = kvc_ref.shape[1]
    n_blocks = -(-seq // SEL_BLOCK)
    rows = Q_PER_KV * qb
    gates = gates_ref[0]
    n_tiles = (start + qb + tk - 1) // tk
    head_rows = [slice(r * qb, (r + 1) * qb) for r in range(Q_PER_KV)]
    kv_heads = range(N_KV)

    def q_rows(g):
        return jnp.concatenate(
            [qz_ref[0, :, (g * Q_PER_KV + r) * LANES:(g * Q_PER_KV + r + 1) * LANES] for r in range(Q_PER_KV)],
            axis=0)


    o_cmp, sel = [], []
    c = _iota((qb, ncp), 1)
    cmask = (CMP_STRIDE * c + (CMP_BLOCK - 1) <= qpos) & (c < ncp - 1)
    for g in kv_heads:
        s = _dot_nt(q_rows(g), kvc_ref[0, :, 0:LANES]).reshape(Q_PER_KV, qb, ncp)
        e, l = _softmax_parts(s, cmask)
        p = e / jnp.maximum(l, 1e-30)
        o_cmp.append(_dot(p.reshape(rows, ncp).astype(BF16), kvc_ref[0, :, LANES:2 * LANES]))
        imp = jnp.sum(p, axis=0)
        sel.append(_select_blocks_rank(_block_scores(imp, LANES), qpos, n_blocks,
                                       min(N_SELECT, n_blocks)).astype(BF16))

    m_sc[...] = jnp.full(m_sc.shape, NEG, F32)

    def pass1(kt, _):
        k0 = pl.multiple_of(kt * tk, tk)
        causal = (k0 + _iota((1, tk), 1)) <= qpos
        for g in kv_heads:
            s = _dot_nt(q_rows(g), kvs_ref[0, pl.ds(k0, tk), 0:LANES])
            mask = (_dot(sel[g], exp_ref[kt]) > 0.5) & causal
            for rs in head_rows:
                sr = jnp.where(mask, s[rs], NEG)
                s_sc[g, kt, rs, :] = sr
                m_sc[g, rs, :] = jnp.maximum(m_sc[g, rs, :], functools.reduce(jnp.maximum, _lane_tiles(sr)))
        return 0

    lax.fori_loop(0, n_tiles, pass1, 0)
    m_sc[...] = jnp.broadcast_to(jnp.max(m_sc[...], axis=-1, keepdims=True), m_sc.shape)
    acc_sc[...] = jnp.zeros(acc_sc.shape, F32)

    def values_and_ones(g, vv):
        return jnp.where((_iota((1, LANES), 1) >> 6) == g, vv, jnp.ones_like(vv))

    def normalised(acc):
        return acc / jnp.maximum(pltpu.roll(acc, LANES // 2, 1), 1e-30)

    def pass2(kt, _):
        k0 = pl.multiple_of(kt * tk, tk)
        for g in kv_heads:
            for rs in head_rows:
                x = s_sc[g, kt, rs, :] - jnp.concatenate([m_sc[g, rs, :]] * (tk // LANES), axis=1)
                p_sc[g, rs, :] = jnp.exp(x.astype(BF16))
            vv = values_and_ones(g, kvs_ref[0, pl.ds(k0, tk), LANES:2 * LANES])
            acc_sc[g] = acc_sc[g] + _dot(p_sc[g], vv)
        return 0

    lax.fori_loop(0, n_tiles, pass2, 0)

    wlen = WINDOW + qb
    w0 = pl.multiple_of(jnp.maximum(start - WINDOW, 0), qb)
    kpos = w0 + _iota((1, wlen), 1)
    wmask = (kpos <= qpos) & (kpos > qpos - WINDOW)
    for g in kv_heads:
        s = _dot_nt(q_rows(g), kvw_ref[0, pl.ds(w0, wlen), 0:LANES])
        for rs in head_rows:
            sr = jnp.where(wmask, s[rs], NEG)
            m = jnp.max(functools.reduce(jnp.maximum, _lane_tiles(sr)), axis=-1, keepdims=True)
            pw_sc[g, rs, :] = jnp.exp((sr - m).astype(BF16))
        o_win = normalised(_dot(pw_sc[g], values_and_ones(g, kvw_ref[0, pl.ds(w0, wlen), LANES:2 * LANES])))
        o_slc = normalised(acc_sc[g])
        own = (_iota((qb, LANES), 1) >> 6) == g
        for r, rs in enumerate(head_rows):
            h = g * Q_PER_KV + r
            o = (gates[:, h:h + 1] * o_cmp[g][rs] + gates[:, N_HEADS + h:N_HEADS + h + 1] * o_slc[rs]
                 + gates[:, 2 * N_HEADS + h:2 * N_HEADS + h + 1] * o_win[rs])
            o_ref[0, :, h * LANES:(h + 1) * LANES] = jnp.where(own, o, 0.0)


def _expand_matrix(n_tiles, tile):
    t = jnp.arange(n_tiles)[:, None, None]
    j = jnp.arange(LANES)[None, :, None]
    k = jnp.arange(tile)[None, None, :]
    return (j == (t * tile + k) // SEL_BLOCK).astype(BF16)


def _attn_prompt(qz, kvs_b, kvw_b, kv_cmpr, gates):
    b, s, _ = qz.shape
    assert s % Q_BLOCK == 0 and s % SEL_TILE == 0 and s >= WINDOW + Q_BLOCK and s // SEL_BLOCK <= LANES
    ncp = kv_cmpr.shape[1]
    n_tiles = s // SEL_TILE
    expand = _expand_matrix(n_tiles, SEL_TILE)
    rows = Q_PER_KV * Q_BLOCK
    full = lambda n: pl.BlockSpec((1, n, KV_COLS), lambda i, j: (i, 0, 0))
    return pl.pallas_call(
        functools.partial(_attn_prompt_kernel, seq=s),
        grid=(b, s // Q_BLOCK),
        in_specs=[pl.BlockSpec((1, Q_BLOCK, QZ_DIM), lambda i, j: (i, j, 0)), full(s), full(s), full(ncp),
                  pl.BlockSpec((1, Q_BLOCK, LANES), lambda i, j: (i, j, 0)), _const_spec(expand.shape)],
        out_specs=pl.BlockSpec((1, Q_BLOCK, QZ_DIM), lambda i, j: (i, j, 0)),
        out_shape=jax.ShapeDtypeStruct((b, s, QZ_DIM), F32),
        scratch_shapes=[pltpu.VMEM((N_KV, n_tiles, rows, SEL_TILE), F32), pltpu.VMEM((N_KV, rows, SEL_TILE), BF16),
                        pltpu.VMEM((N_KV, rows, WINDOW + Q_BLOCK), BF16), pltpu.VMEM((N_KV, rows, LANES), F32),
                        pltpu.VMEM((N_KV, rows, LANES), F32)],
        compiler_params=_cparams(2),
        name="attn_prompt",
    )(qz, kvs_b, kvw_b, kv_cmpr, gates, expand)


def _attn_sample_kernel(pt_ref, *refs, past, n_steps):
    g_pages = PAGES_PER_STEP
    page_refs = refs[:g_pages]
    (qz_ref, kvc_ref, gates_ref, kvs_new_ref, win_ref, kvw_new_ref, exp_ref,
     o_ref, q_sc, sel_sc, s_sc, v_sc, snew_sc, mp_sc, ocw_sc) = refs[g_pages:]
    kt = pl.program_id(1)
    t = qz_ref.shape[1]
    rows = N_HEADS * t
    page = page_refs[0].shape[2]
    tile = g_pages * page
    blocks_per_tile = tile // SEL_BLOCK
    n_past_blocks = past // SEL_BLOCK
    n_blocks = n_past_blocks + -(-t // SEL_BLOCK)
    sel_lanes = sel_sc.shape[0] * LANES
    half_cols = N_KV * HEAD_DIM
    qpos = past + _iota((t, 1), 0)
    qpos_gt = jnp.concatenate([qpos] * N_KV, axis=0)
    own_rows = _iota((N_KV, Q_PER_KV, t, LANES), 0) == (_iota((N_KV, Q_PER_KV, t, LANES), 3) >> 6)

    def new_rows_padded():
        kvn = kvs_new_ref[0]
        return jnp.concatenate([kvn, jnp.zeros((LANES - t, KV_COLS), F32)], axis=0).astype(BF16)

    def masked_scores(s, mask):
        k = s.shape[1]
        return jnp.where(mask.reshape(N_KV, 1, t, k), s.reshape(N_KV, Q_PER_KV, t, k), NEG).reshape(rows, k)

    @pl.when(kt == 0)
    def _():
        q = jnp.concatenate([qz_ref[0, :, h * LANES:(h + 1) * LANES] for h in range(N_HEADS)], axis=0)
        q_sc[...] = q.astype(BF16)
        qb = q_sc[...]
        ncp = kvc_ref.shape[1]
        s = _dot_nt(qb, kvc_ref[0, :, 0:LANES]).reshape(N_HEADS, t, ncp)
        c = _iota((t, ncp), 1)
        cmask = CMP_STRIDE * c + (CMP_BLOCK - 1) <= qpos
        e, l = _softmax_parts(s, cmask)
        p = e / jnp.maximum(l, 1e-30)
        o_cmp = _dot(p.reshape(rows, ncp).astype(BF16), kvc_ref[0, :, LANES:2 * LANES])
        imp = jnp.sum(p.reshape(N_KV, Q_PER_KV, t, ncp), axis=1).reshape(N_KV * t, ncp)
        sel = _select_blocks_iter(_block_scores(imp, sel_lanes), qpos_gt, n_blocks, min(N_SELECT, n_blocks))
        for w in range(sel_sc.shape[0]):
            sel_sc[w] = sel[:, w * LANES:(w + 1) * LANES]
        wb = win_ref.shape[2]
        win = win_ref[0].astype(BF16)
        kvw_new = kvw_new_ref[0].astype(BF16)
        s_old = _dot(qb, win[0:half_cols]).reshape(N_HEADS, t, wb)
        s_new = _dot_nt(qb, kvw_new[:, 0:half_cols]).reshape(N_HEADS, t, t)
        kpos_old = past - wb + _iota((1, wb), 1)
        kpos_new = past + _iota((1, t), 1)
        mask_old = (kpos_old <= qpos) & (kpos_old > qpos - WINDOW) & (kpos_old >= 0)
        mask_new = (kpos_new <= qpos) & (kpos_new > qpos - WINDOW)
        s_old = jnp.where(mask_old, s_old, NEG)
        s_new = jnp.where(mask_new, s_new, NEG)
        m = jnp.maximum(jnp.max(s_old, axis=-1, keepdims=True), jnp.max(s_new, axis=-1, keepdims=True))
        e_old = jnp.where(mask_old, jnp.exp(s_old - m), 0.0)
        e_new = jnp.where(mask_new, jnp.exp(s_new - m), 0.0)
        l = jnp.sum(e_old, axis=-1, keepdims=True) + jnp.sum(e_new, axis=-1, keepdims=True)
        o_win = (_dot_nt(e_old.reshape(rows, wb).astype(BF16), win[half_cols:])
                 + _dot(e_new.reshape(rows, t).astype(BF16), kvw_new[:, half_cols:]))
        o_win = o_win.reshape(N_HEADS, t, LANES) / jnp.maximum(l, 1e-30)
        gts = gates_ref[0]
        gc = jnp.stack([jnp.broadcast_to(gts[:, h:h + 1], (t, LANES)) for h in range(N_HEADS)])
        gw = jnp.stack([jnp.broadcast_to(gts[:, 2 * N_HEADS + h:2 * N_HEADS + h + 1], (t, LANES))
                        for h in range(N_HEADS)])
        ocw_sc[...] = gc * o_cmp.reshape(N_HEADS, t, LANES) + gw * o_win
        jn = (past + _iota((1, LANES), 1)) >> 6
        seln = jnp.zeros((N_KV * t, LANES), F32)
        for jb in range(n_past_blocks, n_blocks):
            seln = jnp.where(jn == jb, sel[:, jb:jb + 1], seln)
        npos = past + _iota((1, LANES), 1)
        nmask = (seln > 0.5) & (npos <= qpos_gt) & (npos < past + t)
        s_n = masked_scores(_dot_nt(qb, new_rows_padded()[:, 0:half_cols]), nmask)
        snew_sc[...] = s_n
        mp_sc[...] = s_n

    k_t = jnp.concatenate([r[0, 0:half_cols, :] for r in page_refs], axis=1).astype(BF16)
    v_sc[kt] = jnp.concatenate([r[0, half_cols:, :] for r in page_refs], axis=1).astype(BF16)
    b0 = kt * blocks_per_tile
    selw = sel_sc[b0 // LANES].astype(BF16)
    kpos = kt * tile + _iota((1, tile), 1)
    mask = (_dot(selw, exp_ref[0]) > 0.5) & (kpos <= qpos_gt)
    s = masked_scores(_dot(q_sc[...], k_t), mask)
    s_sc[kt] = s
    mp_sc[...] = jnp.maximum(mp_sc[...], functools.reduce(jnp.maximum, _lane_tiles(s)))

    @pl.when(kt == n_steps - 1)
    def _():
        m = jnp.max(mp_sc[...], axis=-1, keepdims=True)
        e = jnp.exp(snew_sc[...] - m)
        lp = e
        acc = _dot(e.astype(BF16), new_rows_padded()[:, half_cols:])
        for k2 in range(n_steps):
            e = jnp.exp(s_sc[k2] - m)
            lp = lp + functools.reduce(jnp.add, _lane_tiles(e))
            acc = acc + _dot_nt(e.astype(BF16), v_sc[k2])
        l = jnp.sum(lp, axis=-1, keepdims=True)
        gts = gates_ref[0]
        gs = jnp.stack([jnp.broadcast_to(gts[:, N_HEADS + h:N_HEADS + h + 1], (t, LANES))
                        for h in range(N_HEADS)])
        o_slc = (acc / jnp.maximum(l, 1e-30)).reshape(N_HEADS, t, LANES)
        o = jnp.where(own_rows.reshape(N_HEADS, t, LANES), ocw_sc[...] + gs * o_slc, 0.0)
        for h in range(N_HEADS):
            o_ref[0, :, h * LANES:(h + 1) * LANES] = o[h]


def _attn_sample(qz, kv_cmpr, gates, kvs_new, slc_pool_t, win_t, kvw_new, page_table):
    bn, t, _ = qz.shape
    n_pages = page_table.shape[1]
    page = slc_pool_t.shape[2]
    past = n_pages * page
    g = PAGES_PER_STEP
    n_steps = n_pages // g
    tile = g * page
    blocks_per_tile = tile // SEL_BLOCK
    assert n_pages % g == 0 and page % SEL_BLOCK == 0 and LANES % blocks_per_tile == 0 and t % 8 == 0
    n_blocks = past // SEL_BLOCK + -(-t // SEL_BLOCK)
    sel_groups = -(-n_blocks // LANES)
    ncp = kv_cmpr.shape[1]
    wb = win_t.shape[2]
    n_var = LANES // blocks_per_tile
    expand = _expand_matrix(n_var, tile)

    def page_spec(k):
        return pl.BlockSpec((1, KV_COLS, page), lambda i, j, pt: (pt[i, j * g + k], 0, 0))

    per_b = lambda n, cols: pl.BlockSpec((1, n, cols), lambda i, j, pt: (i, 0, 0))
    grid_spec = pltpu.PrefetchScalarGridSpec(
        num_scalar_prefetch=1,
        grid=(bn, n_steps),
        in_specs=[page_spec(k) for k in range(g)]
        + [per_b(t, QZ_DIM), per_b(ncp, KV_COLS), per_b(t, LANES), per_b(t, KV_COLS), per_b(KV_COLS, wb),
           per_b(t, KV_COLS), pl.BlockSpec((1, LANES, tile), lambda i, j, pt: (j % n_var, 0, 0))],
        out_specs=per_b(t, QZ_DIM),
        scratch_shapes=[pltpu.VMEM((N_HEADS * t, LANES), BF16),
                        pltpu.VMEM((sel_groups, N_KV * t, LANES), F32),
                        pltpu.VMEM((n_steps, N_HEADS * t, tile), F32), pltpu.VMEM((n_steps, LANES, tile), BF16),
                        pltpu.VMEM((N_HEADS * t, LANES), F32), pltpu.VMEM((N_HEADS * t, LANES), F32),
                        pltpu.VMEM((N_HEADS, t, LANES), F32)],
    )
    return pl.pallas_call(
        functools.partial(_attn_sample_kernel, past=past, n_steps=n_steps),
        grid_spec=grid_spec,
        out_shape=jax.ShapeDtypeStruct((bn, t, QZ_DIM), F32),
        compiler_params=_cparams(2),
        name="attn_sample",
    )(page_table, *([slc_pool_t] * g), qz, kv_cmpr, gates, kvs_new, win_t, kvw_new, expand)


def _layernorm(x, g, b):
    mu = jnp.mean(x, axis=-1, keepdims=True)
    xc = x - mu
    var = jnp.mean(xc * xc, axis=-1, keepdims=True)
    return xc * lax.rsqrt(var + EPS) * g + b


def _post_kernel(x_ref, o_ref, u_ref, uprev_ref, gb_ref, st1_ref, st2_ref, g1_ref, sh2_ref, sc2_ref, g2_ref,
                 convw_ref, gconv_ref, gnsa_ref, woc_ref, won_ref, ln1g_ref, ln1b_ref, ln2g_ref, ln2b_ref,
                 wg_ref, wu_ref, wd_ref, y_ref, *, seq):
    tm = x_ref.shape[1]
    u = u_ref[0]
    ext = jnp.concatenate([uprev_ref[0], u], axis=0)
    if tm <= seq:
        pos = (pl.program_id(1) * tm) % seq + _iota((tm, 1), 0)
    else:
        pos = lax.rem(_iota((tm, 1), 0), seq)
    p1 = jnp.where(pos >= 1, ext[7:7 + tm], st1_ref[0])
    p2 = jnp.where(pos >= 2, ext[6:6 + tm], st2_ref[0])
    cw = convw_ref[...]
    y_c = gb_ref[0] * (cw[0:1] * p2 + cw[1:2] * p1 + cw[2:3] * u)
    yn = y_c * lax.rsqrt(jnp.mean(y_c * y_c, axis=-1, keepdims=True) + EPS) * gconv_ref[...]
    o = o_ref[0]
    on = o * lax.rsqrt(jnp.sum(o * o, axis=-1, keepdims=True) * (1.0 / NSA_DIM) + EPS) * gnsa_ref[...]
    mix = _dot(yn.astype(BF16), woc_ref[...]) + _dot(on.astype(BF16), won_ref[...])
    x1 = _layernorm(ALPHA * x_ref[0] + g1_ref[0] * mix, ln1g_ref[...], ln1b_ref[...])
    h = (x1 * (1.0 + sc2_ref[0]) + sh2_ref[0]).astype(BF16)
    a = _dot(h, wg_ref[...])
    f = (a * jax.nn.sigmoid(a)) * _dot(h, wu_ref[...])
    f = _dot(f.astype(BF16), wd_ref[...])
    y_ref[0] = _layernorm(ALPHA * x1 + g2_ref[0] * f, ln2g_ref[...], ln2b_ref[...])


def _post(x, o, u, gb, st1, st2, mods, w, tm, seq):
    b, t, d = x.shape
    row = lambda cols: pl.BlockSpec((1, tm, cols), lambda i, j: (i, j, 0))

    def bcast(a):
        if a.shape[1] == t:
            return pl.BlockSpec((1, tm, a.shape[2]), lambda i, j: (i, j, 0))
        return pl.BlockSpec((1, 1, a.shape[2]), lambda i, j: (i, 0, 0))

    prev = pl.BlockSpec((1, 8, CONV_DIM), lambda i, j: (i, jnp.maximum(j * (tm // 8) - 1, 0), 0))
    consts = [w["conv_w"], w["g_conv"], w["g_nsa"], w["wo_c"], w["wo_n"], w["ln1_g"], w["ln1_b"], w["ln2_g"],
              w["ln2_b"], w["w_gate"], w["w_up"], w["w_down"]]
    return pl.pallas_call(
        functools.partial(_post_kernel, seq=seq),
        grid=(b, t // tm),
        in_specs=[row(d), row(QZ_DIM), row(CONV_DIM), prev, row(CONV_DIM), bcast(st1), bcast(st2)]
        + [bcast(m) for m in mods] + [_const_spec(c.shape) for c in consts],
        out_specs=row(d),
        out_shape=jax.ShapeDtypeStruct((b, t, d), F32),
        compiler_params=_cparams(2),
        name="post_block",
    )(x, o, u, u, gb, st1, st2, *mods, *consts)


def _prep_w_in(w_in):
    d = w_in.shape[0]
    c3 = 3 * CONV_DIM
    splits = (CONV_DIM, 2 * CONV_DIM, c3, c3 + NSA_DIM, c3 + NSA_DIM + KV_COLS, c3 + NSA_DIM + 2 * KV_COLS,
              c3 + NSA_DIM + 3 * KV_COLS)
    hc, gb, gc, q, kvc, kvs, kvw, gl = jnp.split(w_in, splits, axis=1)
    q4 = q.reshape(d, N_KV, Q_PER_KV, HEAD_DIM) * SCALE
    qz = jnp.einsum("dgrh,gk->dgrkh", q4, jnp.eye(N_KV, dtype=w_in.dtype)).reshape(d, QZ_DIM)
    glp = jnp.pad(gl, ((0, 0), (0, LANES - gl.shape[1])))
    return jnp.concatenate([hc, gb, gc, qz, kvc, kvs, kvw, glp], axis=1).astype(BF16)


def _prep_cmp(cmp_pe, cmp_w1, cmp_b1, cmp_w2):
    eye = jnp.eye(N_KV, dtype=cmp_w1.dtype)
    eye2 = jnp.eye(2, dtype=cmp_w1.dtype)
    w1r = cmp_w1.reshape(2, 2, CMP_STRIDE, HEAD_DIM, CMP_HID)
    wbig = jnp.einsum("etjdh,ef,gk->jegdtfkh", w1r, eye2, eye).reshape(CHUNK_COLS, TB_COLS).astype(BF16)
    w2big = jnp.einsum("ehd,ef,gk->eghfkd", cmp_w2, eye2, eye).reshape(TB_COLS // 2, KV_COLS).astype(BF16)
    w1p = cmp_w1.reshape(2, 2, CMP_STRIDE // 2, 2, HEAD_DIM, CMP_HID)
    wpair = jnp.einsum("etpjdh,gk->epjgdtkh", w1p, eye).reshape(
        2, CMP_STRIDE // 2, 2 * N_KV * HEAD_DIM, 2 * N_KV * CMP_HID).astype(BF16)
    w2pair = jnp.einsum("ehd,gk->eghkd", cmp_w2, eye).reshape(2, N_KV * CMP_HID, N_KV * HEAD_DIM).astype(BF16)
    pe2 = cmp_pe.reshape(2, CMP_BLOCK * HEAD_DIM)
    w1f = cmp_w1.reshape(2, CMP_BLOCK * HEAD_DIM, CMP_HID).astype(BF16)
    dense = (wbig, pe2, w1f, cmp_b1, w2big)
    paged = (wpair, wbig[:, TB_COLS // 2:], pe2, w1f, cmp_b1, w2pair)
    return dense, paged


def _prep_post(conv_w, g_conv, g_nsa, w_o, ln1_g, ln1_b, ln2_g, ln2_b, w_gate, w_up, w_down):
    d = w_o.shape[1]
    own = jnp.eye(N_KV, dtype=w_o.dtype)
    g_nsa_z = jnp.einsum("grh,gk->grkh", g_nsa.reshape(N_KV, Q_PER_KV, HEAD_DIM), own).reshape(1, QZ_DIM)
    wo_n = jnp.einsum("grhd,gk->grkhd", w_o[CONV_DIM:].reshape(N_KV, Q_PER_KV, HEAD_DIM, d), own)
    row = lambda a: a.reshape(1, -1)
    return dict(conv_w=conv_w, g_conv=row(g_conv), g_nsa=g_nsa_z, wo_c=w_o[:CONV_DIM].astype(BF16),
                wo_n=wo_n.reshape(QZ_DIM, d).astype(BF16), ln1_g=row(ln1_g), ln1_b=row(ln1_b),
                ln2_g=row(ln2_g), ln2_b=row(ln2_b), w_gate=w_gate.astype(BF16), w_up=w_up.astype(BF16),
                w_down=w_down.astype(BF16))


def _feature_major(cache):
    n, rows = cache.shape[:2]
    return jnp.transpose(cache, (0, 2, 3, 4, 1)).reshape(n, KV_COLS, rows)


def kernel(x_prompt, x_sample, cache_cmp_kv, cache_slc_kv, cache_win_kv, state_conv, page_table, c_prompt,
           c_sample, w_ada, b_ada, w_in, conv_w, cmp_pe, cmp_w1, cmp_b1, cmp_w2, g_conv_out, g_nsa_out, w_o,
           ln1_g, ln1_b, ln2_g, ln2_b, w_ffn_gate, w_ffn_up, w_ffn_down):
    assert w_ada.shape[0] == DEPTH
    bp, s, d = x_prompt.shape
    bs, t, _ = x_sample.shape
    kv_shape = (2, N_KV, HEAD_DIM)

    w_cat = _prep_w_in(w_in[0])
    cmp_dense_w, cmp_paged_w = _prep_cmp(cmp_pe[0], cmp_w1[0], cmp_b1[0], cmp_w2[0])
    post_w = _prep_post(conv_w[0], g_conv_out[0], g_nsa_out[0], w_o[0], ln1_g[0], ln1_b[0], ln2_g[0], ln2_b[0],
                        w_ffn_gate[0], w_ffn_up[0], w_ffn_down[0])

    mod = _modulation(jnp.concatenate([c_prompt, c_sample], axis=0), w_ada[0], b_ada[0])
    mods_p = [m[:, None, :] for m in jnp.split(mod[:bp], 6, axis=-1)]
    mods_s = [jnp.repeat(m, t, axis=0)[None] for m in jnp.split(mod[bp:], 6, axis=-1)]

    u, gb, qz, kvc, kvs, kvw, kvs_b, kvw_b, gates = _in_proj(x_prompt, mods_p[1], mods_p[0], w_cat, 512, BF16)
    kv_cmpr = _compress_dense(kvc, *cmp_dense_w)
    o = _attn_prompt(qz, kvs_b, kvw_b, kv_cmpr, gates)
    zero_state = jnp.zeros((bp, 1, CONV_DIM), F32)
    y_prompt = _post(x_prompt, o, u, gb, zero_state, zero_state, mods_p[2:], post_w, 512, s)
    w_keep = min(WINDOW, s)
    cmp_p = kvc.reshape(1, bp, s, *kv_shape)
    slc_p = kvs.reshape(1, bp, s, *kv_shape)
    win_p = kvw[:, s - w_keep:].reshape(1, bp, w_keep, *kv_shape)
    conv_p = u[:, s - (CONV_W - 1):][None]

    rows = bs * t
    us, gbs, qzs, kvcs, kvss, kvws, _, _, gates_s = _in_proj(
        x_sample.reshape(1, rows, d), mods_s[1], mods_s[0], w_cat, rows, F32)
    per_b = lambda a: a.reshape(bs, t, a.shape[-1])
    kvcs, kvss, kvws = per_b(kvcs), per_b(kvss), per_b(kvws)
    xnew = jnp.pad(kvcs, ((0, 0), (0, (-t) % CMP_STRIDE), (0, 0))).reshape(bs, 1, CHUNK_COLS)
    kv_cmpr_s = _compress_paged(_feature_major(cache_cmp_kv[0]), page_table, xnew, *cmp_paged_w)
    o_s = _attn_sample(per_b(qzs), kv_cmpr_s, per_b(gates_s), kvss, _feature_major(cache_slc_kv[0]),
                       _feature_major(cache_win_kv[0]), kvws, page_table)
    st = state_conv[0]
    tpos = jnp.arange(t)[None, :, None]
    st1 = jnp.where(tpos == 0, st[:, 1:2], 0.0).reshape(1, rows, CONV_DIM)
    st2 = jnp.where(tpos == 0, st[:, 0:1], jnp.where(tpos == 1, st[:, 1:2], 0.0)).reshape(1, rows, CONV_DIM)
    y_sample = _post(x_sample.reshape(1, rows, d), o_s.reshape(1, rows, QZ_DIM), us, gbs, st1, st2, mods_s[2:],
                     post_w, rows, t).reshape(bs, t, d)
    cmp_s = kvcs.reshape(1, bs, t, *kv_shape)
    slc_s = kvss.reshape(1, bs, t, *kv_shape)
    win_all = jnp.concatenate([cache_win_kv[0], kvws.reshape(bs, t, *kv_shape)], axis=1)
    win_s = win_all[:, t:][None]
    conv_s = jnp.concatenate([st, per_b(us)], axis=1)[:, t:][None]
    return (y_prompt, y_sample, cmp_p, slc_p, win_p, conv_p, cmp_s, slc_s, win_s, conv_s)
```

```python
import functools

import jax
import jax.numpy as jnp
from jax import lax
from jax.experimental import pallas as pl
from jax.experimental.pallas import tpu as pltpu

HEAD_DIM = 64
N_KV = 2
Q_PER_KV = 6
N_HEADS = N_KV * Q_PER_KV
CONV_DIM = 256
CONV_W = 3
NSA_DIM = N_HEADS * HEAD_DIM
KV_COLS = 2 * N_KV * HEAD_DIM
CMP_BLOCK = 32
CMP_STRIDE = 16
CMP_HID = 2 * HEAD_DIM
SEL_BLOCK = 64
CMP_PER_SEL = SEL_BLOCK // CMP_STRIDE
N_SELECT = 16
WINDOW = 512
Q_BLOCK = 128
DEPTH = 1
ALPHA = (2 * DEPTH) ** 0.25
EPS = 1e-5
SCALE = HEAD_DIM ** -0.5

LANES = 128
QZ_DIM = N_HEADS * LANES
CHUNK_COLS = CMP_STRIDE * KV_COLS
TB_COLS = 2 * 2 * N_KV * CMP_HID
NEG = -1e30
VMEM_LIMIT = 56 * 1024 * 1024

F32 = jnp.float32
BF16 = jnp.bfloat16


def _cparams(n_grid):
    return pltpu.CompilerParams(dimension_semantics=("arbitrary",) * n_grid, vmem_limit_bytes=VMEM_LIMIT)


def _const_spec(shape):
    nd = len(shape)
    return pl.BlockSpec(shape, lambda *_: (0,) * nd, pipeline_mode=pl.Buffered(1))


def _dot(a, b):
    return jnp.dot(a, b, preferred_element_type=F32)


def _dot_nt(a, b):
    return lax.dot_general(a, b, (((1,), (1,)), ((), ())), preferred_element_type=F32)


def _iota(shape, dim):
    return lax.broadcasted_iota(jnp.int32, shape, dim)


def _lane_tiles(x):
    return [x[:, w * LANES:(w + 1) * LANES] for w in range(x.shape[1] // LANES)]


def _mod_kernel(c_ref, w_ref, b_ref, o_ref):
    c = c_ref[...]
    a = (c * jax.nn.sigmoid(c)).astype(BF16)
    o_ref[...] = _dot(a, w_ref[...].astype(BF16)) + b_ref[...]


def _modulation(c_all, w_ada, b_ada):
    m, d = c_all.shape
    n = w_ada.shape[1]
    tn = 512
    return pl.pallas_call(
        _mod_kernel,
        grid=(n // tn,),
        in_specs=[pl.BlockSpec((m, d), lambda j: (0, 0)),
                  pl.BlockSpec((d, tn), lambda j: (0, j)),
                  pl.BlockSpec((1, tn), lambda j: (0, j))],
        out_specs=pl.BlockSpec((m, tn), lambda j: (0, j)),
        out_shape=jax.ShapeDtypeStruct((m, n), F32),
        compiler_params=_cparams(1),
        name="modulation",
    )(c_all, w_ada, b_ada.reshape(1, n))


_C_HC, _C_GB, _C_GC, _C_Q = 0, CONV_DIM, 2 * CONV_DIM, 3 * CONV_DIM
_C_KVC = _C_Q + QZ_DIM
_C_KVS = _C_KVC + KV_COLS
_C_KVW = _C_KVS + KV_COLS
_C_GL = _C_KVW + KV_COLS
W_IN_COLS = _C_GL + LANES


def _inproj_kernel(x_ref, sc_ref, sh_ref, w_ref, u_ref, gb_ref, qz_ref, kvc_ref, kvs_ref, kvw_ref,
                   kvsb_ref, kvwb_ref, gates_ref):
    h = (x_ref[0] * (1.0 + sc_ref[0]) + sh_ref[0]).astype(BF16)
    hc = _dot(h, w_ref[:, _C_HC:_C_GB])
    gb_ref[0] = _dot(h, w_ref[:, _C_GB:_C_GC])
    gc = _dot(h, w_ref[:, _C_GC:_C_Q])
    u_ref[0] = gc * hc
    qz_ref[0] = _dot(h, w_ref[:, _C_Q:_C_KVC]).astype(qz_ref.dtype)
    kvc_ref[0] = _dot(h, w_ref[:, _C_KVC:_C_KVS])
    kvs = _dot(h, w_ref[:, _C_KVS:_C_KVW])
    kvs_ref[0] = kvs
    kvsb_ref[0] = kvs.astype(BF16)
    kvw = _dot(h, w_ref[:, _C_KVW:_C_GL])
    kvw_ref[0] = kvw
    kvwb_ref[0] = kvw.astype(BF16)
    gates_ref[0] = jax.nn.sigmoid(_dot(h, w_ref[:, _C_GL:W_IN_COLS]))


def _in_proj(x, scale, shift, w_cat, tm, qz_dtype):
    b, t, d = x.shape
    rm = scale.shape[1]
    mod_block = (1, tm, d) if rm == t else (1, 1, d)
    mod_map = (lambda i, j: (i, j, 0)) if rm == t else (lambda i, j: (i, 0, 0))
    row = lambda cols: pl.BlockSpec((1, tm, cols), lambda i, j: (i, j, 0))
    sds = lambda cols, dt: jax.ShapeDtypeStruct((b, t, cols), dt)
    return pl.pallas_call(
        _inproj_kernel,
        grid=(b, t // tm),
        in_specs=[row(d), pl.BlockSpec(mod_block, mod_map), pl.BlockSpec(mod_block, mod_map),
                  _const_spec((d, W_IN_COLS))],
        out_specs=[row(CONV_DIM), row(CONV_DIM), row(QZ_DIM), row(KV_COLS), row(KV_COLS), row(KV_COLS),
                   row(KV_COLS), row(KV_COLS), row(LANES)],
        out_shape=[sds(CONV_DIM, F32), sds(CONV_DIM, F32), sds(QZ_DIM, qz_dtype), sds(KV_COLS, F32),
                   sds(KV_COLS, F32), sds(KV_COLS, F32), sds(KV_COLS, BF16), sds(KV_COLS, BF16),
                   sds(LANES, F32)],
        compiler_params=_cparams(2),
        name="in_proj",
    )(x, scale, shift, w_cat)


def _gelu_tanh(x):
    return jax.nn.gelu(x, approximate=True)


def _cmp_bias_e(pe_ref, w1f_ref, b1_ref, e):
    pe = jnp.broadcast_to(pe_ref[e:e + 1, :], (8, pe_ref.shape[1])).astype(BF16)
    return _dot(pe, w1f_ref[e])[0:1, :] + b1_ref[e:e + 1, :]


def _compress_dense_kernel(x_ref, wbig_ref, pe_ref, w1f_ref, b1_ref, w2_ref, o_ref):
    half = TB_COLS // 2
    tb = _dot(x_ref[0].astype(BF16), wbig_ref[...])
    n = tb.shape[0]
    top = tb[:, :half]
    bot_next = pltpu.roll(tb[:, half:], n - 1, 0)
    bias = jnp.concatenate([_cmp_bias_e(pe_ref, w1f_ref, b1_ref, e) for e in (0, 0, 1, 1)], axis=1)
    hid = _gelu_tanh(top + bot_next + bias)
    o_ref[0] = _dot(hid.astype(BF16), w2_ref[...]).astype(o_ref.dtype)


def _compress_dense(kvc, wbig, pe2, w1f, b1, w2big):
    b, s, _ = kvc.shape
    nch = s // CMP_STRIDE
    x = kvc.reshape(b, nch, CHUNK_COLS)
    return pl.pallas_call(
        _compress_dense_kernel,
        grid=(b,),
        in_specs=[pl.BlockSpec((1, nch, CHUNK_COLS), lambda i: (i, 0, 0)),
                  _const_spec(wbig.shape), _const_spec(pe2.shape), _const_spec(w1f.shape),
                  _const_spec(b1.shape), _const_spec(w2big.shape)],
        out_specs=pl.BlockSpec((1, nch, KV_COLS), lambda i: (i, 0, 0)),
        out_shape=jax.ShapeDtypeStruct((b, nch, KV_COLS), BF16),
        compiler_params=_cparams(1),
        name="compress_prompt",
    )(x, wbig, pe2, w1f, b1, w2big)


PAGES_PER_STEP = 32
CHUNK_PITCH = 24


def _page_copy(pool_ref, page_index, buf, sem, slot, p):
    return pltpu.make_async_copy(pool_ref.at[page_index], buf.at[slot, p], sem.at[slot, p])


def _stream_pages(pt_ref, pool_ref, buf, sem, n_steps):
    g_pages = PAGES_PER_STEP
    step = pl.program_id(0) * n_steps + pl.program_id(1)
    total = pl.num_programs(0) * n_steps
    slot = step % 2

    @pl.when(step == 0)
    def _():
        for p in range(g_pages):
            _page_copy(pool_ref, pt_ref[0, p], buf, sem, 0, p).start()

    nxt = jnp.minimum(step + 1, total - 1)
    nb, nk = nxt // n_steps, nxt % n_steps
    for p in range(g_pages):
        _page_copy(pool_ref, 0, buf, sem, slot, p).wait()
    for p in range(g_pages):
        _page_copy(pool_ref, pt_ref[nb, nk * g_pages + p], buf, sem, 1 - slot, p).start()
    return slot


def _drain_pages(pool_ref, buf, sem, n_steps):
    step = pl.program_id(0) * n_steps + pl.program_id(1)

    @pl.when(step == pl.num_programs(0) * n_steps - 1)
    def _():
        for p in range(PAGES_PER_STEP):
            _page_copy(pool_ref, 0, buf, sem, 1 - step % 2, p).wait()


def _compress_paged_kernel(pt_ref, pool_ref, *refs, n_steps):
    g_pages = PAGES_PER_STEP
    (xnew_ref, wpair_ref, wbot_ref, pe_ref, w1f_ref, b1_ref, w2_ref, o_ref, r_sc, tb_sc, buf, sem) = refs
    kt = pl.program_id(1)
    page = buf.shape[3]
    slot = _stream_pages(pt_ref, pool_ref, buf, sem, n_steps)
    cpp = page // CMP_STRIDE
    rows_step = g_pages * cpp
    half_cols = N_KV * HEAD_DIM
    hid_cols = N_KV * CMP_HID

    row0 = pl.multiple_of(kt * rows_step, rows_step)
    for e in range(2):
        for p in range(g_pages):
            rows_t = buf[slot, p, e * half_cols:(e + 1) * half_cols, :].T
            for c in range(cpp):
                r_sc[e, pl.ds((p * cpp + c) * CHUNK_PITCH, CMP_STRIDE), :] = rows_t[c * CMP_STRIDE:(c + 1) * CMP_STRIDE]
        acc = None
        for jp in range(CMP_STRIDE // 2):
            xs = jnp.concatenate([r_sc[e, pl.ds(2 * jp + jj, rows_step, stride=CHUNK_PITCH), :] for jj in (0, 1)],
                                 axis=1).astype(BF16)
            part = _dot(xs, wpair_ref[e, jp])
            acc = part if acc is None else acc + part
        tb_sc[e, pl.ds(row0, rows_step), :] = acc

    @pl.when(kt == n_steps - 1)
    def _():
        n = tb_sc.shape[1]
        xn = jnp.broadcast_to(xnew_ref[0], (8, CHUNK_COLS)).astype(BF16)
        bot_new = _dot(xn, wbot_ref[...])[0:1, :]
        last = _iota((n, 1), 0) == n - 1
        outs = []
        for e in range(2):
            top, bot = tb_sc[e, :, :hid_cols], tb_sc[e, :, hid_cols:]
            bot_next = jnp.where(last, bot_new[:, e * hid_cols:(e + 1) * hid_cols], pltpu.roll(bot, n - 1, 0))
            bias = jnp.concatenate([_cmp_bias_e(pe_ref, w1f_ref, b1_ref, e)] * N_KV, axis=1)
            hid = _gelu_tanh(top + bot_next + bias)
            outs.append(_dot(hid.astype(BF16), w2_ref[e]))
        o_ref[0] = jnp.concatenate(outs, axis=1).astype(o_ref.dtype)

    _drain_pages(pool_ref, buf, sem, n_steps)


def _page_stream_scratch(page):
    return [pltpu.VMEM((2, PAGES_PER_STEP, KV_COLS, page), F32), pltpu.SemaphoreType.DMA((2, PAGES_PER_STEP))]


def _compress_paged(pool_t, page_table, xnew, wpair, wbot, pe2, w1f, b1, w2pair):
    bn, n_pages = page_table.shape
    page = pool_t.shape[2]
    g = PAGES_PER_STEP
    n_steps = n_pages // g
    assert n_pages % g == 0 and page % CMP_STRIDE == 0 and page == LANES
    n = n_pages * page // CMP_STRIDE

    def const(shape):
        nd = len(shape)
        return pl.BlockSpec(shape, lambda i, j, pt: (0,) * nd, pipeline_mode=pl.Buffered(1))

    grid_spec = pltpu.PrefetchScalarGridSpec(
        num_scalar_prefetch=1,
        grid=(bn, n_steps),
        in_specs=[pl.BlockSpec(memory_space=pl.ANY),
                  pl.BlockSpec((1, 1, CHUNK_COLS), lambda i, j, pt: (i, 0, 0)), const(wpair.shape), const(wbot.shape),
                  const(pe2.shape), const(w1f.shape), const(b1.shape), const(w2pair.shape)],
        out_specs=pl.BlockSpec((1, n, KV_COLS), lambda i, j, pt: (i, 0, 0)),
        scratch_shapes=[pltpu.VMEM((2, g * (page // CMP_STRIDE) * CHUNK_PITCH, LANES), F32),
                        pltpu.VMEM((2, n, 2 * N_KV * CMP_HID), F32)] + _page_stream_scratch(page),
    )
    return pl.pallas_call(
        functools.partial(_compress_paged_kernel, n_steps=n_steps),
        grid_spec=grid_spec,
        out_shape=jax.ShapeDtypeStruct((bn, n, KV_COLS), BF16),
        compiler_params=_cparams(2),
        name="compress_sample",
    )(page_table, pool_t, xnew, wpair, wbot, pe2, w1f, b1, w2pair)


def _split3_dot(x, m01):
    h1 = x.astype(BF16)
    r1 = x - h1.astype(F32)
    h2 = r1.astype(BF16)
    h3 = (r1 - h2.astype(F32)).astype(BF16)
    return _dot(h1, m01) + _dot(h2, m01) + _dot(h3, m01)


def _block_scores(imp, n_lanes):
    ncp = imp.shape[1]
    c = _iota((ncp, n_lanes), 0)
    j = _iota((ncp, n_lanes), 1)
    a = jnp.where((c >= CMP_PER_SEL * j - 1) & (c <= CMP_PER_SEL * j + CMP_PER_SEL - 1), 1.0, 0.0).astype(BF16)
    return _split3_dot(imp, a)


def _ranked_scores(score, qpos, n_blocks):
    j = _iota(score.shape, 1)
    cur = qpos >> 6
    valid = j * SEL_BLOCK <= qpos
    forced = (j == 0) | (j == cur) | (j == cur - 1)
    return jnp.where(forced, -NEG, jnp.where(valid, score, NEG))


def _select_blocks_iter(score, qpos, n_blocks, n_select):
    r, l = score.shape
    removed = 3.0 * NEG
    jf = _iota((r, l), 1).astype(F32)
    sc = jnp.where(jf < n_blocks, _ranked_scores(score, qpos, n_blocks), removed)

    sel = jnp.zeros((r, l), F32)
    for _ in range(n_select):
        mx = jnp.max(sc, axis=-1, keepdims=True)
        first = jnp.min(jnp.where(sc == mx, jf, 1e9), axis=-1, keepdims=True)
        pick = jf == first
        sc, sel = jnp.where(pick, removed, sc), jnp.where(pick, 1.0, sel)
    return sel


def _select_blocks_rank(score, qpos, n_blocks, n_select):
    r, l = score.shape
    nb8 = -(-n_blocks // 8) * 8
    st = _ranked_scores(score, qpos, n_blocks).T[:nb8]
    jrow = _iota((nb8, r), 0)
    rank = jnp.zeros((nb8, r), F32)
    for jp in range(n_blocks):
        row = st[jp:jp + 1, :]
        beats = (row > st) | ((row == st) & (jrow > jp))
        rank = rank + jnp.where(beats, 1.0, 0.0)
    sel_t = jnp.where((rank < n_select) & (jrow < n_blocks), 1.0, 0.0)
    if nb8 < l:
        sel_t = jnp.concatenate([sel_t, jnp.zeros((l - nb8, r), F32)], axis=0)
    return sel_t.T


def _softmax_parts(s, mask):
    s = jnp.where(mask, s, NEG)
    m = jnp.max(s, axis=-1, keepdims=True)
    e = jnp.where(mask, jnp.exp(s - m), 0.0)
    return e, jnp.sum(e, axis=-1, keepdims=True)


SEL_TILE = 512


def _attn_prompt_kernel(qz_ref, kvs_ref, kvw_ref, kvc_ref, gates_ref, exp_ref, o_ref,
                        s_sc, p_sc, pw_sc, m_sc, acc_sc, ocw_sc, *, seq):
    qb = Q_BLOCK
    tk = SEL_TILE
    start = pl.program_id(1) * qb
    qpos = start + _iota((qb, 1), 0)
    ncp = kvc_ref.shape[1]
    n_blocks = -(-seq // SEL_BLOCK)
    rows = Q_PER_KV * qb
    gates = gates_ref[0]
    n_tiles = (start + qb + tk - 1) // tk
    head_rows = [slice(r * qb, (r + 1) * qb) for r in range(Q_PER_KV)]
    kv_heads = range(N_KV)

    def q_rows(g):
        return jnp.concatenate(
            [qz_ref[0, :, (g * Q_PER_KV + r) * LANES:(g * Q_PER_KV + r + 1) * LANES] for r in range(Q_PER_KV)],
            axis=0)


    def values_and_ones(g, vv):
        return jnp.where((_iota((1, LANES), 1) >> 6) == g, vv, jnp.ones_like(vv))

    def normalised(acc):
        return acc / jnp.maximum(pltpu.roll(acc, LANES // 2, 1), 1e-30)

    o_cmp, sel = [], []
    c = _iota((qb, ncp), 1)
    cmask = (CMP_STRIDE * c + (CMP_BLOCK - 1) <= qpos) & (c < ncp - 1)
    for g in kv_heads:
        s = _dot_nt(q_rows(g), kvc_ref[0, :, 0:LANES]).reshape(Q_PER_KV, qb, ncp)
        e, l = _softmax_parts(s, cmask)
        p = e / jnp.maximum(l, 1e-30)
        o_cmp.append(_dot(p.reshape(rows, ncp).astype(BF16), kvc_ref[0, :, LANES:2 * LANES]))
        imp = jnp.sum(p, axis=0)
        sel.append(_select_blocks_rank(_block_scores(imp, LANES), qpos, n_blocks,
                                       min(N_SELECT, n_blocks)).astype(BF16))

    wlen = WINDOW + qb
    w0 = pl.multiple_of(jnp.maximum(start - WINDOW, 0), qb)
    kpos = w0 + _iota((1, wlen), 1)
    wmask = (kpos <= qpos) & (kpos > qpos - WINDOW)
    for g in kv_heads:
        s = _dot_nt(q_rows(g), kvw_ref[0, pl.ds(w0, wlen), 0:LANES])
        for rs in head_rows:
            sr = jnp.where(wmask, s[rs], NEG)
            m = jnp.max(functools.reduce(jnp.maximum, _lane_tiles(sr)), axis=-1, keepdims=True)
            pw_sc[g, rs, :] = jnp.exp((sr - m).astype(BF16))
        o_win = normalised(_dot(pw_sc[g], values_and_ones(g, kvw_ref[0, pl.ds(w0, wlen), LANES:2 * LANES])))
        for r, rs in enumerate(head_rows):
            h = g * Q_PER_KV + r
            ocw_sc[g, rs, :] = (gates[:, h:h + 1] * o_cmp[g][rs]
                                + gates[:, 2 * N_HEADS + h:2 * N_HEADS + h + 1] * o_win[rs])

    m_sc[...] = jnp.full(m_sc.shape, NEG, F32)

    def pass1(kt, _):
        k0 = pl.multiple_of(kt * tk, tk)
        causal = (k0 + _iota((1, tk), 1)) <= qpos
        for g in kv_heads:
            s = _dot_nt(q_rows(g), kvs_ref[0, pl.ds(k0, tk), 0:LANES])
            mask = (_dot(sel[g], exp_ref[kt]) > 0.5) & causal
            for rs in head_rows:
                sr = jnp.where(mask, s[rs], NEG)
                s_sc[g, kt, rs, :] = sr
                m_sc[g, rs, :] = jnp.maximum(m_sc[g, rs, :], functools.reduce(jnp.maximum, _lane_tiles(sr)))
        return 0

    lax.fori_loop(0, n_tiles, pass1, 0)
    m_sc[...] = jnp.broadcast_to(jnp.max(m_sc[...], axis=-1, keepdims=True), m_sc.shape)
    acc_sc[...] = jnp.zeros(acc_sc.shape, F32)

    def pass2(kt, _):
        k0 = pl.multiple_of(kt * tk, tk)
        for g in kv_heads:
            for rs in head_rows:
                x = s_sc[g, kt, rs, :] - jnp.concatenate([m_sc[g, rs, :]] * (tk // LANES), axis=1)
                p_sc[g, rs, :] = jnp.exp(x.astype(BF16))
            vv = values_and_ones(g, kvs_ref[0, pl.ds(k0, tk), LANES:2 * LANES])
            acc_sc[g] = acc_sc[g] + _dot(p_sc[g], vv)
        return 0

    lax.fori_loop(0, n_tiles, pass2, 0)

    for g in kv_heads:
        o_slc = normalised(acc_sc[g])
        own = (_iota((qb, LANES), 1) >> 6) == g
        for r, rs in enumerate(head_rows):
            h = g * Q_PER_KV + r
            o = ocw_sc[g, rs, :] + gates[:, N_HEADS + h:N_HEADS + h + 1] * o_slc[rs]
            o_ref[0, :, h * LANES:(h + 1) * LANES] = jnp.where(own, o, 0.0)


def _expand_matrix(n_tiles, tile):
    t = jnp.arange(n_tiles)[:, None, None]
    j = jnp.arange(LANES)[None, :, None]
    k = jnp.arange(tile)[None, None, :]
    return (j == (t * tile + k) // SEL_BLOCK).astype(BF16)


def _attn_prompt(qz, kvs_b, kvw_b, kv_cmpr, gates):
    b, s, _ = qz.shape
    assert s % Q_BLOCK == 0 and s % SEL_TILE == 0 and s >= WINDOW + Q_BLOCK and s // SEL_BLOCK <= LANES
    ncp = kv_cmpr.shape[1]
    n_tiles = s // SEL_TILE
    expand = _expand_matrix(n_tiles, SEL_TILE)
    rows = Q_PER_KV * Q_BLOCK
    full = lambda n: pl.BlockSpec((1, n, KV_COLS), lambda i, j: (i, 0, 0))
    return pl.pallas_call(
        functools.partial(_attn_prompt_kernel, seq=s),
        grid=(b, s // Q_BLOCK),
        in_specs=[pl.BlockSpec((1, Q_BLOCK, QZ_DIM), lambda i, j: (i, j, 0)), full(s), full(s), full(ncp),
                  pl.BlockSpec((1, Q_BLOCK, LANES), lambda i, j: (i, j, 0)), _const_spec(expand.shape)],
        out_specs=pl.BlockSpec((1, Q_BLOCK, QZ_DIM), lambda i, j: (i, j, 0)),
        out_shape=jax.ShapeDtypeStruct((b, s, QZ_DIM), F32),
        scratch_shapes=[pltpu.VMEM((N_KV, n_tiles, rows, SEL_TILE), F32), pltpu.VMEM((N_KV, rows, SEL_TILE), BF16),
                        pltpu.VMEM((N_KV, rows, WINDOW + Q_BLOCK), BF16), pltpu.VMEM((N_KV, rows, LANES), F32),
                        pltpu.VMEM((N_KV, rows, LANES), F32), pltpu.VMEM((N_KV, rows, LANES), F32)],
        compiler_params=_cparams(2),
        name="attn_prompt",
    )(qz, kvs_b, kvw_b, kv_cmpr, gates, expand)


def _attn_sample_kernel(pt_ref, pool_ref, *refs, past, n_steps):
    g_pages = PAGES_PER_STEP
    (qz_ref, kvc_ref, gates_ref, kvs_new_ref, win_ref, kvw_new_ref, exp_ref,
     o_ref, q_sc, sel_sc, s_sc, v_sc, snew_sc, mp_sc, ocw_sc, buf, sem) = refs
    kt = pl.program_id(1)
    t = qz_ref.shape[1]
    rows = N_HEADS * t
    page = buf.shape[3]
    tile = g_pages * page
    blocks_per_tile = tile // SEL_BLOCK
    n_past_blocks = past // SEL_BLOCK
    n_blocks = n_past_blocks + -(-t // SEL_BLOCK)
    sel_lanes = sel_sc.shape[0] * LANES
    half_cols = N_KV * HEAD_DIM
    qpos = past + _iota((t, 1), 0)
    qpos_gt = jnp.concatenate([qpos] * N_KV, axis=0)
    own_rows = _iota((N_KV, Q_PER_KV, t, LANES), 0) == (_iota((N_KV, Q_PER_KV, t, LANES), 3) >> 6)

    def new_rows_padded():
        kvn = kvs_new_ref[0]
        return jnp.concatenate([kvn, jnp.zeros((LANES - t, KV_COLS), F32)], axis=0).astype(BF16)

    def masked_scores(s, mask):
        k = s.shape[1]
        return jnp.where(mask.reshape(N_KV, 1, t, k), s.reshape(N_KV, Q_PER_KV, t, k), NEG).reshape(rows, k)

    @pl.when(kt == 0)
    def _():
        q = jnp.concatenate([qz_ref[0, :, h * LANES:(h + 1) * LANES] for h in range(N_HEADS)], axis=0)
        q_sc[...] = q.astype(BF16)
        qb = q_sc[...]
        ncp = kvc_ref.shape[1]
        s = _dot_nt(qb, kvc_ref[0, :, 0:LANES]).reshape(N_HEADS, t, ncp)
        c = _iota((t, ncp), 1)
        cmask = CMP_STRIDE * c + (CMP_BLOCK - 1) <= qpos
        e, l = _softmax_parts(s, cmask)
        p = e / jnp.maximum(l, 1e-30)
        o_cmp = _dot(p.reshape(rows, ncp).astype(BF16), kvc_ref[0, :, LANES:2 * LANES])
        imp = jnp.sum(p.reshape(N_KV, Q_PER_KV, t, ncp), axis=1).reshape(N_KV * t, ncp)
        sel = _select_blocks_iter(_block_scores(imp, sel_lanes), qpos_gt, n_blocks, min(N_SELECT, n_blocks))
        for w in range(sel_sc.shape[0]):
            sel_sc[w] = sel[:, w * LANES:(w + 1) * LANES]
        wb = win_ref.shape[2]
        win = win_ref[0].astype(BF16)
        kvw_new = kvw_new_ref[0].astype(BF16)
        s_old = _dot(qb, win[0:half_cols]).reshape(N_HEADS, t, wb)
        s_new = _dot_nt(qb, kvw_new[:, 0:half_cols]).reshape(N_HEADS, t, t)
        kpos_old = past - wb + _iota((1, wb), 1)
        kpos_new = past + _iota((1, t), 1)
        mask_old = (kpos_old <= qpos) & (kpos_old > qpos - WINDOW) & (kpos_old >= 0)
        mask_new = (kpos_new <= qpos) & (kpos_new > qpos - WINDOW)
        s_old = jnp.where(mask_old, s_old, NEG)
        s_new = jnp.where(mask_new, s_new, NEG)
        m = jnp.maximum(jnp.max(s_old, axis=-1, keepdims=True), jnp.max(s_new, axis=-1, keepdims=True))
        e_old = jnp.where(mask_old, jnp.exp(s_old - m), 0.0)
        e_new = jnp.where(mask_new, jnp.exp(s_new - m), 0.0)
        l = jnp.sum(e_old, axis=-1, keepdims=True) + jnp.sum(e_new, axis=-1, keepdims=True)
        o_win = (_dot_nt(e_old.reshape(rows, wb).astype(BF16), win[half_cols:])
                 + _dot(e_new.reshape(rows, t).astype(BF16), kvw_new[:, half_cols:]))
        o_win = o_win.reshape(N_HEADS, t, LANES) / jnp.maximum(l, 1e-30)
        gts = gates_ref[0]
        gc = jnp.stack([jnp.broadcast_to(gts[:, h:h + 1], (t, LANES)) for h in range(N_HEADS)])
        gw = jnp.stack([jnp.broadcast_to(gts[:, 2 * N_HEADS + h:2 * N_HEADS + h + 1], (t, LANES))
                        for h in range(N_HEADS)])
        ocw_sc[...] = gc * o_cmp.reshape(N_HEADS, t, LANES) + gw * o_win
        jn = (past + _iota((1, LANES), 1)) >> 6
        seln = jnp.zeros((N_KV * t, LANES), F32)
        for jb in range(n_past_blocks, n_blocks):
            seln = jnp.where(jn == jb, sel[:, jb:jb + 1], seln)
        npos = past + _iota((1, LANES), 1)
        nmask = (seln > 0.5) & (npos <= qpos_gt) & (npos < past + t)
        s_n = masked_scores(_dot_nt(qb, new_rows_padded()[:, 0:half_cols]), nmask)
        snew_sc[...] = s_n
        mp_sc[...] = s_n

    slot = _stream_pages(pt_ref, pool_ref, buf, sem, n_steps)
    k_t = jnp.concatenate([buf[slot, p, 0:half_cols, :] for p in range(g_pages)], axis=1).astype(BF16)
    v_sc[kt] = jnp.concatenate([buf[slot, p, half_cols:, :] for p in range(g_pages)], axis=1).astype(BF16)
    b0 = kt * blocks_per_tile
    selw = sel_sc[b0 // LANES].astype(BF16)
    kpos = kt * tile + _iota((1, tile), 1)
    mask = (_dot(selw, exp_ref[0]) > 0.5) & (kpos <= qpos_gt)
    s = masked_scores(_dot(q_sc[...], k_t), mask)
    s_sc[kt] = s
    mp_sc[...] = jnp.maximum(mp_sc[...], functools.reduce(jnp.maximum, _lane_tiles(s)))

    @pl.when(kt == n_steps - 1)
    def _():
        m = jnp.max(mp_sc[...], axis=-1, keepdims=True)
        e = jnp.exp(snew_sc[...] - m)
        lp = e
        acc = _dot(e.astype(BF16), new_rows_padded()[:, half_cols:])
        for k2 in range(n_steps):
            e = jnp.exp(s_sc[k2] - m)
            lp = lp + functools.reduce(jnp.add, _lane_tiles(e))
            acc = acc + _dot_nt(e.astype(BF16), v_sc[k2])
        l = jnp.sum(lp, axis=-1, keepdims=True)
        gts = gates_ref[0]
        gs = jnp.stack([jnp.broadcast_to(gts[:, N_HEADS + h:N_HEADS + h + 1], (t, LANES))
                        for h in range(N_HEADS)])
        o_slc = (acc / jnp.maximum(l, 1e-30)).reshape(N_HEADS, t, LANES)
        o = jnp.where(own_rows.reshape(N_HEADS, t, LANES), ocw_sc[...] + gs * o_slc, 0.0)
        for h in range(N_HEADS):
            o_ref[0, :, h * LANES:(h + 1) * LANES] = o[h]

    _drain_pages(pool_ref, buf, sem, n_steps)


def _attn_sample(qz, kv_cmpr, gates, kvs_new, slc_pool_t, win_t, kvw_new, page_table):
    bn, t, _ = qz.shape
    n_pages = page_table.shape[1]
    page = slc_pool_t.shape[2]
    past = n_pages * page
    g = PAGES_PER_STEP
    n_steps = n_pages // g
    tile = g * page
    blocks_per_tile = tile // SEL_BLOCK
    assert n_pages % g == 0 and page % SEL_BLOCK == 0 and LANES % blocks_per_tile == 0 and t % 8 == 0
    n_blocks = past // SEL_BLOCK + -(-t // SEL_BLOCK)
    sel_groups = -(-n_blocks // LANES)
    ncp = kv_cmpr.shape[1]
    wb = win_t.shape[2]
    n_var = LANES // blocks_per_tile
    expand = _expand_matrix(n_var, tile)

    per_b = lambda n, cols: pl.BlockSpec((1, n, cols), lambda i, j, pt: (i, 0, 0))
    grid_spec = pltpu.PrefetchScalarGridSpec(
        num_scalar_prefetch=1,
        grid=(bn, n_steps),
        in_specs=[pl.BlockSpec(memory_space=pl.ANY),
                  per_b(t, QZ_DIM), per_b(ncp, KV_COLS), per_b(t, LANES), per_b(t, KV_COLS), per_b(KV_COLS, wb),
                  per_b(t, KV_COLS), pl.BlockSpec((1, LANES, tile), lambda i, j, pt: (j % n_var, 0, 0))],
        out_specs=per_b(t, QZ_DIM),
        scratch_shapes=[pltpu.VMEM((N_HEADS * t, LANES), BF16),
                        pltpu.VMEM((sel_groups, N_KV * t, LANES), F32),
                        pltpu.VMEM((n_steps, N_HEADS * t, tile), F32), pltpu.VMEM((n_steps, LANES, tile), BF16),
                        pltpu.VMEM((N_HEADS * t, LANES), F32), pltpu.VMEM((N_HEADS * t, LANES), F32),
                        pltpu.VMEM((N_HEADS, t, LANES), F32)] + _page_stream_scratch(page),
    )
    return pl.pallas_call(
        functools.partial(_attn_sample_kernel, past=past, n_steps=n_steps),
        grid_spec=grid_spec,
        out_shape=jax.ShapeDtypeStruct((bn, t, QZ_DIM), F32),
        compiler_params=_cparams(2),
        name="attn_sample",
    )(page_table, slc_pool_t, qz, kv_cmpr, gates, kvs_new, win_t, kvw_new, expand)


def _layernorm(x, g, b):
    mu = jnp.mean(x, axis=-1, keepdims=True)
    xc = x - mu
    var = jnp.mean(xc * xc, axis=-1, keepdims=True)
    return xc * lax.rsqrt(var + EPS) * g + b


def _post_kernel(x_ref, o_ref, u_ref, uprev_ref, gb_ref, st1_ref, st2_ref, g1_ref, sh2_ref, sc2_ref, g2_ref,
                 convw_ref, gconv_ref, gnsa_ref, woc_ref, won_ref, ln1g_ref, ln1b_ref, ln2g_ref, ln2b_ref,
                 wg_ref, wu_ref, wd_ref, y_ref, *, seq):
    tm = x_ref.shape[1]
    u = u_ref[0]
    ext = jnp.concatenate([uprev_ref[0], u], axis=0)
    if tm <= seq:
        pos = (pl.program_id(1) * tm) % seq + _iota((tm, 1), 0)
    else:
        pos = lax.rem(_iota((tm, 1), 0), seq)
    p1 = jnp.where(pos >= 1, ext[7:7 + tm], st1_ref[0])
    p2 = jnp.where(pos >= 2, ext[6:6 + tm], st2_ref[0])
    cw = convw_ref[...]
    y_c = gb_ref[0] * (cw[0:1] * p2 + cw[1:2] * p1 + cw[2:3] * u)
    yn = y_c * lax.rsqrt(jnp.mean(y_c * y_c, axis=-1, keepdims=True) + EPS) * gconv_ref[...]
    o = o_ref[0]
    on = o * lax.rsqrt(jnp.sum(o * o, axis=-1, keepdims=True) * (1.0 / NSA_DIM) + EPS) * gnsa_ref[...]
    mix = _dot(yn.astype(BF16), woc_ref[...]) + _dot(on.astype(BF16), won_ref[...])
    x1 = _layernorm(ALPHA * x_ref[0] + g1_ref[0] * mix, ln1g_ref[...], ln1b_ref[...])
    h = (x1 * (1.0 + sc2_ref[0]) + sh2_ref[0]).astype(BF16)
    a = _dot(h, wg_ref[...])
    f = (a * jax.nn.sigmoid(a)) * _dot(h, wu_ref[...])
    f = _dot(f.astype(BF16), wd_ref[...])
    y_ref[0] = _layernorm(ALPHA * x1 + g2_ref[0] * f, ln2g_ref[...], ln2b_ref[...])


def _post(x, o, u, gb, st1, st2, mods, w, tm, seq):
    b, t, d = x.shape
    row = lambda cols: pl.BlockSpec((1, tm, cols), lambda i, j: (i, j, 0))

    def bcast(a):
        if a.shape[1] == t:
            return pl.BlockSpec((1, tm, a.shape[2]), lambda i, j: (i, j, 0))
        return pl.BlockSpec((1, 1, a.shape[2]), lambda i, j: (i, 0, 0))

    prev = pl.BlockSpec((1, 8, CONV_DIM), lambda i, j: (i, jnp.maximum(j * (tm // 8) - 1, 0), 0))
    consts = [w["conv_w"], w["g_conv"], w["g_nsa"], w["wo_c"], w["wo_n"], w["ln1_g"], w["ln1_b"], w["ln2_g"],
              w["ln2_b"], w["w_gate"], w["w_up"], w["w_down"]]
    return pl.pallas_call(
        functools.partial(_post_kernel, seq=seq),
        grid=(b, t // tm),
        in_specs=[row(d), row(QZ_DIM), row(CONV_DIM), prev, row(CONV_DIM), bcast(st1), bcast(st2)]
        + [bcast(m) for m in mods] + [_const_spec(c.shape) for c in consts],
        out_specs=row(d),
        out_shape=jax.ShapeDtypeStruct((b, t, d), F32),
        compiler_params=_cparams(2),
        name="post_block",
    )(x, o, u, u, gb, st1, st2, *mods, *consts)


def _prep_w_in(w_in):
    d = w_in.shape[0]
    c3 = 3 * CONV_DIM
    splits = (CONV_DIM, 2 * CONV_DIM, c3, c3 + NSA_DIM, c3 + NSA_DIM + KV_COLS, c3 + NSA_DIM + 2 * KV_COLS,
              c3 + NSA_DIM + 3 * KV_COLS)
    hc, gb, gc, q, kvc, kvs, kvw, gl = jnp.split(w_in, splits, axis=1)
    q4 = q.reshape(d, N_KV, Q_PER_KV, HEAD_DIM) * SCALE
    qz = jnp.einsum("dgrh,gk->dgrkh", q4, jnp.eye(N_KV, dtype=w_in.dtype)).reshape(d, QZ_DIM)
    glp = jnp.pad(gl, ((0, 0), (0, LANES - gl.shape[1])))
    return jnp.concatenate([hc, gb, gc, qz, kvc, kvs, kvw, glp], axis=1).astype(BF16)


def _prep_cmp(cmp_pe, cmp_w1, cmp_b1, cmp_w2):
    eye = jnp.eye(N_KV, dtype=cmp_w1.dtype)
    eye2 = jnp.eye(2, dtype=cmp_w1.dtype)
    w1r = cmp_w1.reshape(2, 2, CMP_STRIDE, HEAD_DIM, CMP_HID)
    wbig = jnp.einsum("etjdh,ef,gk->jegdtfkh", w1r, eye2, eye).reshape(CHUNK_COLS, TB_COLS).astype(BF16)
    w2big = jnp.einsum("ehd,ef,gk->eghfkd", cmp_w2, eye2, eye).reshape(TB_COLS // 2, KV_COLS).astype(BF16)
    w1p = cmp_w1.reshape(2, 2, CMP_STRIDE // 2, 2, HEAD_DIM, CMP_HID)
    wpair = jnp.einsum("etpjdh,gk->epjgdtkh", w1p, eye).reshape(
        2, CMP_STRIDE // 2, 2 * N_KV * HEAD_DIM, 2 * N_KV * CMP_HID).astype(BF16)
    w2pair = jnp.einsum("ehd,gk->eghkd", cmp_w2, eye).reshape(2, N_KV * CMP_HID, N_KV * HEAD_DIM).astype(BF16)
    pe2 = cmp_pe.reshape(2, CMP_BLOCK * HEAD_DIM)
    w1f = cmp_w1.reshape(2, CMP_BLOCK * HEAD_DIM, CMP_HID).astype(BF16)
    dense = (wbig, pe2, w1f, cmp_b1, w2big)
    paged = (wpair, wbig[:, TB_COLS // 2:], pe2, w1f, cmp_b1, w2pair)
    return dense, paged


def _prep_post(conv_w, g_conv, g_nsa, w_o, ln1_g, ln1_b, ln2_g, ln2_b, w_gate, w_up, w_down):
    d = w_o.shape[1]
    own = jnp.eye(N_KV, dtype=w_o.dtype)
    g_nsa_z = jnp.einsum("grh,gk->grkh", g_nsa.reshape(N_KV, Q_PER_KV, HEAD_DIM), own).reshape(1, QZ_DIM)
    wo_n = jnp.einsum("grhd,gk->grkhd", w_o[CONV_DIM:].reshape(N_KV, Q_PER_KV, HEAD_DIM, d), own)
    row = lambda a: a.reshape(1, -1)
    return dict(conv_w=conv_w, g_conv=row(g_conv), g_nsa=g_nsa_z, wo_c=w_o[:CONV_DIM].astype(BF16),
                wo_n=wo_n.reshape(QZ_DIM, d).astype(BF16), ln1_g=row(ln1_g), ln1_b=row(ln1_b),
                ln2_g=row(ln2_g), ln2_b=row(ln2_b), w_gate=w_gate.astype(BF16), w_up=w_up.astype(BF16),
                w_down=w_down.astype(BF16))


def _feature_major(cache):
    n, rows = cache.shape[:2]
    return jnp.transpose(cache, (0, 2, 3, 4, 1)).reshape(n, KV_COLS, rows)


def kernel(x_prompt, x_sample, cache_cmp_kv, cache_slc_kv, cache_win_kv, state_conv, page_table, c_prompt,
           c_sample, w_ada, b_ada, w_in, conv_w, cmp_pe, cmp_w1, cmp_b1, cmp_w2, g_conv_out, g_nsa_out, w_o,
           ln1_g, ln1_b, ln2_g, ln2_b, w_ffn_gate, w_ffn_up, w_ffn_down):
    assert w_ada.shape[0] == DEPTH
    bp, s, d = x_prompt.shape
    bs, t, _ = x_sample.shape
    kv_shape = (2, N_KV, HEAD_DIM)

    w_cat = _prep_w_in(w_in[0])
    cmp_dense_w, cmp_paged_w = _prep_cmp(cmp_pe[0], cmp_w1[0], cmp_b1[0], cmp_w2[0])
    post_w = _prep_post(conv_w[0], g_conv_out[0], g_nsa_out[0], w_o[0], ln1_g[0], ln1_b[0], ln2_g[0], ln2_b[0],
                        w_ffn_gate[0], w_ffn_up[0], w_ffn_down[0])

    mod = _modulation(jnp.concatenate([c_prompt, c_sample], axis=0), w_ada[0], b_ada[0])
    mods_p = [m[:, None, :] for m in jnp.split(mod[:bp], 6, axis=-1)]
    mods_s = [jnp.repeat(m, t, axis=0)[None] for m in jnp.split(mod[bp:], 6, axis=-1)]

    u, gb, qz, kvc, kvs, kvw, kvs_b, kvw_b, gates = _in_proj(x_prompt, mods_p[1], mods_p[0], w_cat, 512, BF16)
    kv_cmpr = _compress_dense(kvc, *cmp_dense_w)
    o = _attn_prompt(qz, kvs_b, kvw_b, kv_cmpr, gates)
    zero_state = jnp.zeros((bp, 1, CONV_DIM), F32)
    y_prompt = _post(x_prompt, o, u, gb, zero_state, zero_state, mods_p[2:], post_w, 512, s)
    w_keep = min(WINDOW, s)
    cmp_p = kvc.reshape(1, bp, s, *kv_shape)
    slc_p = kvs.reshape(1, bp, s, *kv_shape)
    win_p = kvw[:, s - w_keep:].reshape(1, bp, w_keep, *kv_shape)
    conv_p = u[:, s - (CONV_W - 1):][None]

    rows = bs * t
    us, gbs, qzs, kvcs, kvss, kvws, _, _, gates_s = _in_proj(
        x_sample.reshape(1, rows, d), mods_s[1], mods_s[0], w_cat, rows, F32)
    per_b = lambda a: a.reshape(bs, t, a.shape[-1])
    kvcs, kvss, kvws = per_b(kvcs), per_b(kvss), per_b(kvws)
    xnew = jnp.pad(kvcs, ((0, 0), (0, (-t) % CMP_STRIDE), (0, 0))).reshape(bs, 1, CHUNK_COLS)
    kv_cmpr_s = _compress_paged(_feature_major(cache_cmp_kv[0]), page_table, xnew, *cmp_paged_w)
    o_s = _attn_sample(per_b(qzs), kv_cmpr_s, per_b(gates_s), kvss, _feature_major(cache_slc_kv[0]),
                       _feature_major(cache_win_kv[0]), kvws, page_table)
    st = state_conv[0]
    tpos = jnp.arange(t)[None, :, None]
    st1 = jnp.where(tpos == 0, st[:, 1:2], 0.0).reshape(1, rows, CONV_DIM)
    st2 = jnp.where(tpos == 0, st[:, 0:1], jnp.where(tpos == 1, st[:, 1:2], 0.0)).reshape(1, rows, CONV_DIM)
    y_sample = _post(x_sample.reshape(1, rows, d), o_s.reshape(1, rows, QZ_DIM), us, gbs, st1, st2, mods_s[2:],
                     post_w, rows, t).reshape(bs, t, d)
    cmp_s = kvcs.reshape(1, bs, t, *kv_shape)
    slc_s = kvss.reshape(1, bs, t, *kv_shape)
    win_all = jnp.concatenate([cache_win_kv[0], kvws.reshape(bs, t, *kv_shape)], axis=1)
    win_s = win_all[:, t:][None]
    conv_s = jnp.concatenate([st, per_b(us)], axis=1)[:, t:][None]
    return (y_prompt, y_sample, cmp_p, slc_p, win_p, conv_p, cmp_s, slc_s, win_s, conv_s)
```

```python
import functools

import jax
import jax.numpy as jnp
from jax import lax
from jax.experimental import pallas as pl
from jax.experimental.pallas import tpu as pltpu

HEAD_DIM = 64
N_KV = 2
Q_PER_KV = 6
N_HEADS = N_KV * Q_PER_KV
CONV_DIM = 256
CONV_W = 3
NSA_DIM = N_HEADS * HEAD_DIM
KV_COLS = 2 * N_KV * HEAD_DIM
CMP_BLOCK = 32
CMP_STRIDE = 16
CMP_HID = 2 * HEAD_DIM
SEL_BLOCK = 64
CMP_PER_SEL = SEL_BLOCK // CMP_STRIDE
N_SELECT = 16
WINDOW = 512
Q_BLOCK = 128
DEPTH = 1
ALPHA = (2 * DEPTH) ** 0.25
EPS = 1e-5
SCALE = HEAD_DIM ** -0.5

LANES = 128
QZ_DIM = N_HEADS * LANES
CHUNK_COLS = CMP_STRIDE * KV_COLS
TB_COLS = 2 * 2 * N_KV * CMP_HID
NEG = -1e30
VMEM_LIMIT = 56 * 1024 * 1024

F32 = jnp.float32
BF16 = jnp.bfloat16


def _cparams(n_grid):
    return pltpu.CompilerParams(dimension_semantics=("arbitrary",) * n_grid, vmem_limit_bytes=VMEM_LIMIT)


def _const_spec(shape):
    nd = len(shape)
    return pl.BlockSpec(shape, lambda *_: (0,) * nd, pipeline_mode=pl.Buffered(1))


def _dot(a, b):
    return jnp.dot(a, b, preferred_element_type=F32)


def _dot_nt(a, b):
    return lax.dot_general(a, b, (((1,), (1,)), ((), ())), preferred_element_type=F32)


def _iota(shape, dim):
    return lax.broadcasted_iota(jnp.int32, shape, dim)


def _lane_tiles(x):
    return [x[:, w * LANES:(w + 1) * LANES] for w in range(x.shape[1] // LANES)]


def _mod_kernel(c_ref, w_ref, b_ref, o_ref):
    c = c_ref[...]
    a = (c * jax.nn.sigmoid(c)).astype(BF16)
    o_ref[...] = _dot(a, w_ref[...].astype(BF16)) + b_ref[...]


def _modulation(c_all, w_ada, b_ada):
    m, d = c_all.shape
    n = w_ada.shape[1]
    tn = 512
    return pl.pallas_call(
        _mod_kernel,
        grid=(n // tn,),
        in_specs=[pl.BlockSpec((m, d), lambda j: (0, 0)),
                  pl.BlockSpec((d, tn), lambda j: (0, j)),
                  pl.BlockSpec((1, tn), lambda j: (0, j))],
        out_specs=pl.BlockSpec((m, tn), lambda j: (0, j)),
        out_shape=jax.ShapeDtypeStruct((m, n), F32),
        compiler_params=_cparams(1),
        name="modulation",
    )(c_all, w_ada, b_ada.reshape(1, n))


_C_HC, _C_GB, _C_GC, _C_Q = 0, CONV_DIM, 2 * CONV_DIM, 3 * CONV_DIM
_C_KVC = _C_Q + QZ_DIM
_C_KVS = _C_KVC + KV_COLS
_C_KVW = _C_KVS + KV_COLS
_C_GL = _C_KVW + KV_COLS
W_IN_COLS = _C_GL + LANES


def _inproj_kernel(x_ref, sc_ref, sh_ref, w_ref, u_ref, gb_ref, qz_ref, kvc_ref, kvs_ref, kvw_ref,
                   kvsb_ref, kvwb_ref, gates_ref):
    h = (x_ref[0] * (1.0 + sc_ref[0]) + sh_ref[0]).astype(BF16)
    hc = _dot(h, w_ref[:, _C_HC:_C_GB])
    gb_ref[0] = _dot(h, w_ref[:, _C_GB:_C_GC])
    gc = _dot(h, w_ref[:, _C_GC:_C_Q])
    u_ref[0] = gc * hc
    qz_ref[0] = _dot(h, w_ref[:, _C_Q:_C_KVC]).astype(qz_ref.dtype)
    kvc_ref[0] = _dot(h, w_ref[:, _C_KVC:_C_KVS])
    kvs = _dot(h, w_ref[:, _C_KVS:_C_KVW])
    kvs_ref[0] = kvs
    kvsb_ref[0] = kvs.astype(BF16)
    kvw = _dot(h, w_ref[:, _C_KVW:_C_GL])
    kvw_ref[0] = kvw
    kvwb_ref[0] = kvw.astype(BF16)
    gates_ref[0] = jax.nn.sigmoid(_dot(h, w_ref[:, _C_GL:W_IN_COLS]))


def _in_proj(x, scale, shift, w_cat, tm, qz_dtype):
    b, t, d = x.shape
    rm = scale.shape[1]
    mod_block = (1, tm, d) if rm == t else (1, 1, d)
    mod_map = (lambda i, j: (i, j, 0)) if rm == t else (lambda i, j: (i, 0, 0))
    row = lambda cols: pl.BlockSpec((1, tm, cols), lambda i, j: (i, j, 0))
    sds = lambda cols, dt: jax.ShapeDtypeStruct((b, t, cols), dt)
    return pl.pallas_call(
        _inproj_kernel,
        grid=(b, t // tm),
        in_specs=[row(d), pl.BlockSpec(mod_block, mod_map), pl.BlockSpec(mod_block, mod_map),
                  _const_spec((d, W_IN_COLS))],
        out_specs=[row(CONV_DIM), row(CONV_DIM), row(QZ_DIM), row(KV_COLS), row(KV_COLS), row(KV_COLS),
                   row(KV_COLS), row(KV_COLS), row(LANES)],
        out_shape=[sds(CONV_DIM, F32), sds(CONV_DIM, F32), sds(QZ_DIM, qz_dtype), sds(KV_COLS, F32),
                   sds(KV_COLS, F32), sds(KV_COLS, F32), sds(KV_COLS, BF16), sds(KV_COLS, BF16),
                   sds(LANES, F32)],
        compiler_params=_cparams(2),
        name="in_proj",
    )(x, scale, shift, w_cat)


def _gelu_tanh(x):
    return jax.nn.gelu(x, approximate=True)


def _cmp_bias_e(pe_ref, w1f_ref, b1_ref, e):
    pe = jnp.broadcast_to(pe_ref[e:e + 1, :], (8, pe_ref.shape[1])).astype(BF16)
    return _dot(pe, w1f_ref[e])[0:1, :] + b1_ref[e:e + 1, :]


def _compress_dense_kernel(x_ref, wbig_ref, pe_ref, w1f_ref, b1_ref, w2_ref, o_ref):
    half = TB_COLS // 2
    tb = _dot(x_ref[0].astype(BF16), wbig_ref[...])
    n = tb.shape[0]
    top = tb[:, :half]
    bot_next = pltpu.roll(tb[:, half:], n - 1, 0)
    bias = jnp.concatenate([_cmp_bias_e(pe_ref, w1f_ref, b1_ref, e) for e in (0, 0, 1, 1)], axis=1)
    hid = _gelu_tanh(top + bot_next + bias)
    o_ref[0] = _dot(hid.astype(BF16), w2_ref[...]).astype(o_ref.dtype)


def _compress_dense(kvc, wbig, pe2, w1f, b1, w2big):
    b, s, _ = kvc.shape
    nch = s // CMP_STRIDE
    x = kvc.reshape(b, nch, CHUNK_COLS)
    return pl.pallas_call(
        _compress_dense_kernel,
        grid=(b,),
        in_specs=[pl.BlockSpec((1, nch, CHUNK_COLS), lambda i: (i, 0, 0)),
                  _const_spec(wbig.shape), _const_spec(pe2.shape), _const_spec(w1f.shape),
                  _const_spec(b1.shape), _const_spec(w2big.shape)],
        out_specs=pl.BlockSpec((1, nch, KV_COLS), lambda i: (i, 0, 0)),
        out_shape=jax.ShapeDtypeStruct((b, nch, KV_COLS), BF16),
        compiler_params=_cparams(1),
        name="compress_prompt",
    )(x, wbig, pe2, w1f, b1, w2big)


PAGES_PER_STEP = 32
CHUNK_PITCH = 24


def _page_copy(pool_ref, page_index, buf, sem, slot, p):
    return pltpu.make_async_copy(pool_ref.at[page_index], buf.at[slot, p], sem.at[slot, p])


def _await_pages(pt_ref, pool_ref, buf, sem, n_steps):
    b, kt = pl.program_id(0), pl.program_id(1)

    @pl.when((b == 0) & (kt == 0))
    def _():
        for k in range(n_steps):
            for p in range(PAGES_PER_STEP):
                _page_copy(pool_ref, pt_ref[0, k * PAGES_PER_STEP + p], buf, sem, k, p).start()

    for p in range(PAGES_PER_STEP):
        _page_copy(pool_ref, 0, buf, sem, kt, p).wait()
    return kt


def _refill_pages(pt_ref, pool_ref, buf, sem):
    kt = pl.program_id(1)
    nb = jnp.minimum(pl.program_id(0) + 1, pl.num_programs(0) - 1)
    for p in range(PAGES_PER_STEP):
        _page_copy(pool_ref, pt_ref[nb, kt * PAGES_PER_STEP + p], buf, sem, kt, p).start()


def _drain_pages(pool_ref, buf, sem, n_steps):
    @pl.when((pl.program_id(0) == pl.num_programs(0) - 1) & (pl.program_id(1) == n_steps - 1))
    def _():
        for k in range(n_steps):
            for p in range(PAGES_PER_STEP):
                _page_copy(pool_ref, 0, buf, sem, k, p).wait()


def _compress_paged_kernel(pt_ref, pool_ref, *refs, n_steps):
    g_pages = PAGES_PER_STEP
    (xnew_ref, wpair_ref, wbot_ref, pe_ref, w1f_ref, b1_ref, w2_ref, o_ref, r_sc, tb_sc, buf, sem) = refs
    kt = pl.program_id(1)
    page = buf.shape[3]
    slot = _await_pages(pt_ref, pool_ref, buf, sem, n_steps)
    cpp = page // CMP_STRIDE
    rows_step = g_pages * cpp
    half_cols = N_KV * HEAD_DIM
    hid_cols = N_KV * CMP_HID

    row0 = pl.multiple_of(kt * rows_step, rows_step)
    for e in range(2):
        for p in range(g_pages):
            rows_t = buf[slot, p, e * half_cols:(e + 1) * half_cols, :].T
            for c in range(cpp):
                r_sc[e, pl.ds((p * cpp + c) * CHUNK_PITCH, CMP_STRIDE), :] = rows_t[c * CMP_STRIDE:(c + 1) * CMP_STRIDE]
        acc = None
        for jp in range(CMP_STRIDE // 2):
            xs = jnp.concatenate([r_sc[e, pl.ds(2 * jp + jj, rows_step, stride=CHUNK_PITCH), :] for jj in (0, 1)],
                                 axis=1).astype(BF16)
            part = _dot(xs, wpair_ref[e, jp])
            acc = part if acc is None else acc + part
        tb_sc[e, pl.ds(row0, rows_step), :] = acc
    _refill_pages(pt_ref, pool_ref, buf, sem)

    @pl.when(kt == n_steps - 1)
    def _():
        n = tb_sc.shape[1]
        xn = jnp.broadcast_to(xnew_ref[0], (8, CHUNK_COLS)).astype(BF16)
        bot_new = _dot(xn, wbot_ref[...])[0:1, :]
        last = _iota((n, 1), 0) == n - 1
        outs = []
        for e in range(2):
            top, bot = tb_sc[e, :, :hid_cols], tb_sc[e, :, hid_cols:]
            bot_next = jnp.where(last, bot_new[:, e * hid_cols:(e + 1) * hid_cols], pltpu.roll(bot, n - 1, 0))
            bias = jnp.concatenate([_cmp_bias_e(pe_ref, w1f_ref, b1_ref, e)] * N_KV, axis=1)
            hid = _gelu_tanh(top + bot_next + bias)
            outs.append(_dot(hid.astype(BF16), w2_ref[e]))
        o_ref[0] = jnp.concatenate(outs, axis=1).astype(o_ref.dtype)

    _drain_pages(pool_ref, buf, sem, n_steps)


def _page_stream_scratch(page, n_steps):
    return [pltpu.VMEM((n_steps, PAGES_PER_STEP, KV_COLS, page), F32),
            pltpu.SemaphoreType.DMA((n_steps, PAGES_PER_STEP))]


def _compress_paged(pool_t, page_table, xnew, wpair, wbot, pe2, w1f, b1, w2pair):
    bn, n_pages = page_table.shape
    page = pool_t.shape[2]
    g = PAGES_PER_STEP
    n_steps = n_pages // g
    assert n_pages % g == 0 and page % CMP_STRIDE == 0 and page == LANES
    n = n_pages * page // CMP_STRIDE

    def const(shape):
        nd = len(shape)
        return pl.BlockSpec(shape, lambda i, j, pt: (0,) * nd, pipeline_mode=pl.Buffered(1))

    grid_spec = pltpu.PrefetchScalarGridSpec(
        num_scalar_prefetch=1,
        grid=(bn, n_steps),
        in_specs=[pl.BlockSpec(memory_space=pl.ANY),
                  pl.BlockSpec((1, 1, CHUNK_COLS), lambda i, j, pt: (i, 0, 0)), const(wpair.shape), const(wbot.shape),
                  const(pe2.shape), const(w1f.shape), const(b1.shape), const(w2pair.shape)],
        out_specs=pl.BlockSpec((1, n, KV_COLS), lambda i, j, pt: (i, 0, 0)),
        scratch_shapes=[pltpu.VMEM((2, g * (page // CMP_STRIDE) * CHUNK_PITCH, LANES), F32),
                        pltpu.VMEM((2, n, 2 * N_KV * CMP_HID), F32)] + _page_stream_scratch(page, n_steps),
    )
    return pl.pallas_call(
        functools.partial(_compress_paged_kernel, n_steps=n_steps),
        grid_spec=grid_spec,
        out_shape=jax.ShapeDtypeStruct((bn, n, KV_COLS), BF16),
        compiler_params=_cparams(2),
        name="compress_sample",
    )(page_table, pool_t, xnew, wpair, wbot, pe2, w1f, b1, w2pair)


def _split3_dot(x, m01):
    h1 = x.astype(BF16)
    r1 = x - h1.astype(F32)
    h2 = r1.astype(BF16)
    h3 = (r1 - h2.astype(F32)).astype(BF16)
    return _dot(h1, m01) + _dot(h2, m01) + _dot(h3, m01)


def _block_scores(imp, n_lanes):
    ncp = imp.shape[1]
    c = _iota((ncp, n_lanes), 0)
    j = _iota((ncp, n_lanes), 1)
    a = jnp.where((c >= CMP_PER_SEL * j - 1) & (c <= CMP_PER_SEL * j + CMP_PER_SEL - 1), 1.0, 0.0).astype(BF16)
    return _split3_dot(imp, a)


def _ranked_scores(score, qpos, n_blocks):
    j = _iota(score.shape, 1)
    cur = qpos >> 6
    valid = j * SEL_BLOCK <= qpos
    forced = (j == 0) | (j == cur) | (j == cur - 1)
    return jnp.where(forced, -NEG, jnp.where(valid, score, NEG))


def _select_blocks_iter(score, qpos, n_blocks, n_select):
    r, l = score.shape
    removed = 3.0 * NEG
    jf = _iota((r, l), 1).astype(F32)
    sc = jnp.where(jf < n_blocks, _ranked_scores(score, qpos, n_blocks), removed)

    sel = jnp.zeros((r, l), F32)
    for _ in range(n_select):
        mx = jnp.max(sc, axis=-1, keepdims=True)
        first = jnp.min(jnp.where(sc == mx, jf, 1e9), axis=-1, keepdims=True)
        pick = jf == first
        sc, sel = jnp.where(pick, removed, sc), jnp.where(pick, 1.0, sel)
    return sel


def _select_blocks_rank(score, qpos, n_blocks, n_select):
    r, l = score.shape
    nb8 = -(-n_blocks // 8) * 8
    st = _ranked_scores(score, qpos, n_blocks).T[:nb8]
    jrow = _iota((nb8, r), 0)
    rank = jnp.zeros((nb8, r), F32)
    for jp in range(n_blocks):
        row = st[jp:jp + 1, :]
        beats = (row > st) | ((row == st) & (jrow > jp))
        rank = rank + jnp.where(beats, 1.0, 0.0)
    sel_t = jnp.where((rank < n_select) & (jrow < n_blocks), 1.0, 0.0)
    if nb8 < l:
        sel_t = jnp.concatenate([sel_t, jnp.zeros((l - nb8, r), F32)], axis=0)
    return sel_t.T


def _softmax_parts(s, mask):
    s = jnp.where(mask, s, NEG)
    m = jnp.max(s, axis=-1, keepdims=True)
    e = jnp.where(mask, jnp.exp(s - m), 0.0)
    return e, jnp.sum(e, axis=-1, keepdims=True)


SEL_TILE = 512


def _attn_prompt_kernel(qz_ref, kvs_ref, kvw_ref, kvc_ref, gates_ref, exp_ref, o_ref,
                        s_sc, p_sc, pw_sc, m_sc, acc_sc, ocw_sc, *, seq):
    qb = Q_BLOCK
    tk = SEL_TILE
    start = pl.program_id(1) * qb
    qpos = start + _iota((qb, 1), 0)
    ncp = kvc_ref.shape[1]
    n_blocks = -(-seq // SEL_BLOCK)
    rows = Q_PER_KV * qb
    gates = gates_ref[0]
    n_tiles = (start + qb + tk - 1) // tk
    head_rows = [slice(r * qb, (r + 1) * qb) for r in range(Q_PER_KV)]
    kv_heads = range(N_KV)

    def q_rows(g):
        return jnp.concatenate(
            [qz_ref[0, :, (g * Q_PER_KV + r) * LANES:(g * Q_PER_KV + r + 1) * LANES] for r in range(Q_PER_KV)],
            axis=0)


    def values_and_ones(g, vv):
        return jnp.where((_iota((1, LANES), 1) >> 6) == g, vv, jnp.ones_like(vv))

    def normalised(acc):
        return acc / jnp.maximum(pltpu.roll(acc, LANES // 2, 1), 1e-30)

    o_cmp, sel = [], []
    c = _iota((qb, ncp), 1)
    cmask = (CMP_STRIDE * c + (CMP_BLOCK - 1) <= qpos) & (c < ncp - 1)
    for g in kv_heads:
        s = _dot_nt(q_rows(g), kvc_ref[0, :, 0:LANES]).reshape(Q_PER_KV, qb, ncp)
        e, l = _softmax_parts(s, cmask)
        p = e / jnp.maximum(l, 1e-30)
        o_cmp.append(_dot(p.reshape(rows, ncp).astype(BF16), kvc_ref[0, :, LANES:2 * LANES]))
        imp = jnp.sum(p, axis=0)
        sel.append(_select_blocks_rank(_block_scores(imp, LANES), qpos, n_blocks,
                                       min(N_SELECT, n_blocks)).astype(BF16))

    wlen = WINDOW + qb
    w0 = pl.multiple_of(jnp.maximum(start - WINDOW, 0), qb)
    kpos = w0 + _iota((1, wlen), 1)
    wmask = (kpos <= qpos) & (kpos > qpos - WINDOW)
    for g in kv_heads:
        s = _dot_nt(q_rows(g), kvw_ref[0, pl.ds(w0, wlen), 0:LANES])
        for rs in head_rows:
            sr = jnp.where(wmask, s[rs], NEG)
            m = jnp.max(functools.reduce(jnp.maximum, _lane_tiles(sr)), axis=-1, keepdims=True)
            pw_sc[g, rs, :] = jnp.exp((sr - m).astype(BF16))
        o_win = normalised(_dot(pw_sc[g], values_and_ones(g, kvw_ref[0, pl.ds(w0, wlen), LANES:2 * LANES])))
        for r, rs in enumerate(head_rows):
            h = g * Q_PER_KV + r
            ocw_sc[g, rs, :] = (gates[:, h:h + 1] * o_cmp[g][rs]
                                + gates[:, 2 * N_HEADS + h:2 * N_HEADS + h + 1] * o_win[rs])

    m_sc[...] = jnp.full(m_sc.shape, NEG, F32)

    def pass1(kt, _):
        k0 = pl.multiple_of(kt * tk, tk)
        causal = (k0 + _iota((1, tk), 1)) <= qpos
        for g in kv_heads:
            s = _dot_nt(q_rows(g), kvs_ref[0, pl.ds(k0, tk), 0:LANES])
            mask = (_dot(sel[g], exp_ref[kt]) > 0.5) & causal
            for rs in head_rows:
                sr = jnp.where(mask, s[rs], NEG)
                s_sc[g, kt, rs, :] = sr
                m_sc[g, rs, :] = jnp.maximum(m_sc[g, rs, :], functools.reduce(jnp.maximum, _lane_tiles(sr)))
        return 0

    lax.fori_loop(0, n_tiles, pass1, 0)
    m_sc[...] = jnp.broadcast_to(jnp.max(m_sc[...], axis=-1, keepdims=True), m_sc.shape)
    acc_sc[...] = jnp.zeros(acc_sc.shape, F32)

    def pass2(kt, _):
        k0 = pl.multiple_of(kt * tk, tk)
        for g in kv_heads:
            for rs in head_rows:
                x = s_sc[g, kt, rs, :] - jnp.concatenate([m_sc[g, rs, :]] * (tk // LANES), axis=1)
                p_sc[g, rs, :] = jnp.exp(x.astype(BF16))
            vv = values_and_ones(g, kvs_ref[0, pl.ds(k0, tk), LANES:2 * LANES])
            acc_sc[g] = acc_sc[g] + _dot(p_sc[g], vv)
        return 0

    lax.fori_loop(0, n_tiles, pass2, 0)

    for g in kv_heads:
        o_slc = normalised(acc_sc[g])
        own = (_iota((qb, LANES), 1) >> 6) == g
        for r, rs in enumerate(head_rows):
            h = g * Q_PER_KV + r
            o = ocw_sc[g, rs, :] + gates[:, N_HEADS + h:N_HEADS + h + 1] * o_slc[rs]
            o_ref[0, :, h * LANES:(h + 1) * LANES] = jnp.where(own, o, 0.0)


def _expand_matrix(n_tiles, tile):
    t = jnp.arange(n_tiles)[:, None, None]
    j = jnp.arange(LANES)[None, :, None]
    k = jnp.arange(tile)[None, None, :]
    return (j == (t * tile + k) // SEL_BLOCK).astype(BF16)


def _attn_prompt(qz, kvs_b, kvw_b, kv_cmpr, gates):
    b, s, _ = qz.shape
    assert s % Q_BLOCK == 0 and s % SEL_TILE == 0 and s >= WINDOW + Q_BLOCK and s // SEL_BLOCK <= LANES
    ncp = kv_cmpr.shape[1]
    n_tiles = s // SEL_TILE
    expand = _expand_matrix(n_tiles, SEL_TILE)
    rows = Q_PER_KV * Q_BLOCK
    full = lambda n: pl.BlockSpec((1, n, KV_COLS), lambda i, j: (i, 0, 0))
    return pl.pallas_call(
        functools.partial(_attn_prompt_kernel, seq=s),
        grid=(b, s // Q_BLOCK),
        in_specs=[pl.BlockSpec((1, Q_BLOCK, QZ_DIM), lambda i, j: (i, j, 0)), full(s), full(s), full(ncp),
                  pl.BlockSpec((1, Q_BLOCK, LANES), lambda i, j: (i, j, 0)), _const_spec(expand.shape)],
        out_specs=pl.BlockSpec((1, Q_BLOCK, QZ_DIM), lambda i, j: (i, j, 0)),
        out_shape=jax.ShapeDtypeStruct((b, s, QZ_DIM), F32),
        scratch_shapes=[pltpu.VMEM((N_KV, n_tiles, rows, SEL_TILE), F32), pltpu.VMEM((N_KV, rows, SEL_TILE), BF16),
                        pltpu.VMEM((N_KV, rows, WINDOW + Q_BLOCK), BF16), pltpu.VMEM((N_KV, rows, LANES), F32),
                        pltpu.VMEM((N_KV, rows, LANES), F32), pltpu.VMEM((N_KV, rows, LANES), F32)],
        compiler_params=_cparams(2),
        name="attn_prompt",
    )(qz, kvs_b, kvw_b, kv_cmpr, gates, expand)


def _attn_sample_kernel(pt_ref, pool_ref, *refs, past, n_steps):
    g_pages = PAGES_PER_STEP
    (qz_ref, kvc_ref, gates_ref, kvs_new_ref, win_ref, kvw_new_ref, exp_ref,
     o_ref, q_sc, sel_sc, s_sc, v_sc, snew_sc, mp_sc, ocw_sc, buf, sem) = refs
    kt = pl.program_id(1)
    t = qz_ref.shape[1]
    rows = N_HEADS * t
    page = buf.shape[3]
    tile = g_pages * page
    blocks_per_tile = tile // SEL_BLOCK
    n_past_blocks = past // SEL_BLOCK
    n_blocks = n_past_blocks + -(-t // SEL_BLOCK)
    sel_lanes = sel_sc.shape[0] * LANES
    half_cols = N_KV * HEAD_DIM
    qpos = past + _iota((t, 1), 0)
    qpos_gt = jnp.concatenate([qpos] * N_KV, axis=0)
    own_rows = _iota((N_KV, Q_PER_KV, t, LANES), 0) == (_iota((N_KV, Q_PER_KV, t, LANES), 3) >> 6)

    def new_rows_padded():
        kvn = kvs_new_ref[0]
        return jnp.concatenate([kvn, jnp.zeros((LANES - t, KV_COLS), F32)], axis=0).astype(BF16)

    def masked_scores(s, mask):
        k = s.shape[1]
        return jnp.where(mask.reshape(N_KV, 1, t, k), s.reshape(N_KV, Q_PER_KV, t, k), NEG).reshape(rows, k)

    @pl.when(kt == 0)
    def _():
        q = jnp.concatenate([qz_ref[0, :, h * LANES:(h + 1) * LANES] for h in range(N_HEADS)], axis=0)
        q_sc[...] = q.astype(BF16)
        qb = q_sc[...]
        ncp = kvc_ref.shape[1]
        s = _dot_nt(qb, kvc_ref[0, :, 0:LANES]).reshape(N_HEADS, t, ncp)
        c = _iota((t, ncp), 1)
        cmask = CMP_STRIDE * c + (CMP_BLOCK - 1) <= qpos
        e, l = _softmax_parts(s, cmask)
        p = e / jnp.maximum(l, 1e-30)
        o_cmp = _dot(p.reshape(rows, ncp).astype(BF16), kvc_ref[0, :, LANES:2 * LANES])
        imp = jnp.sum(p.reshape(N_KV, Q_PER_KV, t, ncp), axis=1).reshape(N_KV * t, ncp)
        sel = _select_blocks_iter(_block_scores(imp, sel_lanes), qpos_gt, n_blocks, min(N_SELECT, n_blocks))
        for w in range(sel_sc.shape[0]):
            sel_sc[w] = sel[:, w * LANES:(w + 1) * LANES]
        wb = win_ref.shape[2]
        win = win_ref[0].astype(BF16)
        kvw_new = kvw_new_ref[0].astype(BF16)
        s_old = _dot(qb, win[0:half_cols]).reshape(N_HEADS, t, wb)
        s_new = _dot_nt(qb, kvw_new[:, 0:half_cols]).reshape(N_HEADS, t, t)
        kpos_old = past - wb + _iota((1, wb), 1)
        kpos_new = past + _iota((1, t), 1)
        mask_old = (kpos_old <= qpos) & (kpos_old > qpos - WINDOW) & (kpos_old >= 0)
        mask_new = (kpos_new <= qpos) & (kpos_new > qpos - WINDOW)
        s_old = jnp.where(mask_old, s_old, NEG)
        s_new = jnp.where(mask_new, s_new, NEG)
        m = jnp.maximum(jnp.max(s_old, axis=-1, keepdims=True), jnp.max(s_new, axis=-1, keepdims=True))
        e_old = jnp.where(mask_old, jnp.exp(s_old - m), 0.0)
        e_new = jnp.where(mask_new, jnp.exp(s_new - m), 0.0)
        l = jnp.sum(e_old, axis=-1, keepdims=True) + jnp.sum(e_new, axis=-1, keepdims=True)
        o_win = (_dot_nt(e_old.reshape(rows, wb).astype(BF16), win[half_cols:])
                 + _dot(e_new.reshape(rows, t).astype(BF16), kvw_new[:, half_cols:]))
        o_win = o_win.reshape(N_HEADS, t, LANES) / jnp.maximum(l, 1e-30)
        gts = gates_ref[0]
        gc = jnp.stack([jnp.broadcast_to(gts[:, h:h + 1], (t, LANES)) for h in range(N_HEADS)])
        gw = jnp.stack([jnp.broadcast_to(gts[:, 2 * N_HEADS + h:2 * N_HEADS + h + 1], (t, LANES))
                        for h in range(N_HEADS)])
        ocw_sc[...] = gc * o_cmp.reshape(N_HEADS, t, LANES) + gw * o_win
        jn = (past + _iota((1, LANES), 1)) >> 6
        seln = jnp.zeros((N_KV * t, LANES), F32)
        for jb in range(n_past_blocks, n_blocks):
            seln = jnp.where(jn == jb, sel[:, jb:jb + 1], seln)
        npos = past + _iota((1, LANES), 1)
        nmask = (seln > 0.5) & (npos <= qpos_gt) & (npos < past + t)
        s_n = masked_scores(_dot_nt(qb, new_rows_padded()[:, 0:half_cols]), nmask)
        snew_sc[...] = s_n
        mp_sc[...] = s_n

    slot = _await_pages(pt_ref, pool_ref, buf, sem, n_steps)
    k_t = jnp.concatenate([buf[slot, p, 0:half_cols, :] for p in range(g_pages)], axis=1).astype(BF16)
    v_sc[kt] = jnp.concatenate([buf[slot, p, half_cols:, :] for p in range(g_pages)], axis=1).astype(BF16)
    b0 = kt * blocks_per_tile
    selw = sel_sc[b0 // LANES].astype(BF16)
    kpos = kt * tile + _iota((1, tile), 1)
    mask = (_dot(selw, exp_ref[0]) > 0.5) & (kpos <= qpos_gt)
    s = masked_scores(_dot(q_sc[...], k_t), mask)
    s_sc[kt] = s
    mp_sc[...] = jnp.maximum(mp_sc[...], functools.reduce(jnp.maximum, _lane_tiles(s)))
    _refill_pages(pt_ref, pool_ref, buf, sem)

    @pl.when(kt == n_steps - 1)
    def _():
        m = jnp.max(mp_sc[...], axis=-1, keepdims=True)
        e = jnp.exp(snew_sc[...] - m)
        lp = e
        acc = _dot(e.astype(BF16), new_rows_padded()[:, half_cols:])
        for k2 in range(n_steps):
            e = jnp.exp(s_sc[k2] - m)
            lp = lp + functools.reduce(jnp.add, _lane_tiles(e))
            acc = acc + _dot_nt(e.astype(BF16), v_sc[k2])
        l = jnp.sum(lp, axis=-1, keepdims=True)
        gts = gates_ref[0]
        gs = jnp.stack([jnp.broadcast_to(gts[:, N_HEADS + h:N_HEADS + h + 1], (t, LANES))
                        for h in range(N_HEADS)])
        o_slc = (acc / jnp.maximum(l, 1e-30)).reshape(N_HEADS, t, LANES)
        o = jnp.where(own_rows.reshape(N_HEADS, t, LANES), ocw_sc[...] + gs * o_slc, 0.0)
        for h in range(N_HEADS):
            o_ref[0, :, h * LANES:(h + 1) * LANES] = o[h]

    _drain_pages(pool_ref, buf, sem, n_steps)


def _attn_sample(qz, kv_cmpr, gates, kvs_new, slc_pool_t, win_t, kvw_new, page_table):
    bn, t, _ = qz.shape
    n_pages = page_table.shape[1]
    page = slc_pool_t.shape[2]
    past = n_pages * page
    g = PAGES_PER_STEP
    n_steps = n_pages // g
    tile = g * page
    blocks_per_tile = tile // SEL_BLOCK
    assert n_pages % g == 0 and page % SEL_BLOCK == 0 and LANES % blocks_per_tile == 0 and t % 8 == 0
    n_blocks = past // SEL_BLOCK + -(-t // SEL_BLOCK)
    sel_groups = -(-n_blocks // LANES)
    ncp = kv_cmpr.shape[1]
    wb = win_t.shape[2]
    n_var = LANES // blocks_per_tile
    expand = _expand_matrix(n_var, tile)

    per_b = lambda n, cols: pl.BlockSpec((1, n, cols), lambda i, j, pt: (i, 0, 0))
    grid_spec = pltpu.PrefetchScalarGridSpec(
        num_scalar_prefetch=1,
        grid=(bn, n_steps),
        in_specs=[pl.BlockSpec(memory_space=pl.ANY),
                  per_b(t, QZ_DIM), per_b(ncp, KV_COLS), per_b(t, LANES), per_b(t, KV_COLS), per_b(KV_COLS, wb),
                  per_b(t, KV_COLS), pl.BlockSpec((1, LANES, tile), lambda i, j, pt: (j % n_var, 0, 0))],
        out_specs=per_b(t, QZ_DIM),
        scratch_shapes=[pltpu.VMEM((N_HEADS * t, LANES), BF16),
                        pltpu.VMEM((sel_groups, N_KV * t, LANES), F32),
                        pltpu.VMEM((n_steps, N_HEADS * t, tile), F32), pltpu.VMEM((n_steps, LANES, tile), BF16),
                        pltpu.VMEM((N_HEADS * t, LANES), F32), pltpu.VMEM((N_HEADS * t, LANES), F32),
                        pltpu.VMEM((N_HEADS, t, LANES), F32)] + _page_stream_scratch(page, n_steps),
    )
    return pl.pallas_call(
        functools.partial(_attn_sample_kernel, past=past, n_steps=n_steps),
        grid_spec=grid_spec,
        out_shape=jax.ShapeDtypeStruct((bn, t, QZ_DIM), F32),
        compiler_params=_cparams(2),
        name="attn_sample",
    )(page_table, slc_pool_t, qz, kv_cmpr, gates, kvs_new, win_t, kvw_new, expand)


def _layernorm(x, g, b):
    mu = jnp.mean(x, axis=-1, keepdims=True)
    xc = x - mu
    var = jnp.mean(xc * xc, axis=-1, keepdims=True)
    return xc * lax.rsqrt(var + EPS) * g + b


def _post_kernel(x_ref, o_ref, u_ref, uprev_ref, gb_ref, st1_ref, st2_ref, g1_ref, sh2_ref, sc2_ref, g2_ref,
                 convw_ref, gconv_ref, gnsa_ref, woc_ref, won_ref, ln1g_ref, ln1b_ref, ln2g_ref, ln2b_ref,
                 wg_ref, wu_ref, wd_ref, y_ref, *, seq):
    tm = x_ref.shape[1]
    u = u_ref[0]
    ext = jnp.concatenate([uprev_ref[0], u], axis=0)
    if tm <= seq:
        pos = (pl.program_id(1) * tm) % seq + _iota((tm, 1), 0)
    else:
        pos = lax.rem(_iota((tm, 1), 0), seq)
    p1 = jnp.where(pos >= 1, ext[7:7 + tm], st1_ref[0])
    p2 = jnp.where(pos >= 2, ext[6:6 + tm], st2_ref[0])
    cw = convw_ref[...]
    y_c = gb_ref[0] * (cw[0:1] * p2 + cw[1:2] * p1 + cw[2:3] * u)
    yn = y_c * lax.rsqrt(jnp.mean(y_c * y_c, axis=-1, keepdims=True) + EPS) * gconv_ref[...]
    o = o_ref[0]
    on = o * lax.rsqrt(jnp.sum(o * o, axis=-1, keepdims=True) * (1.0 / NSA_DIM) + EPS) * gnsa_ref[...]
    mix = _dot(yn.astype(BF16), woc_ref[...]) + _dot(on.astype(BF16), won_ref[...])
    x1 = _layernorm(ALPHA * x_ref[0] + g1_ref[0] * mix, ln1g_ref[...], ln1b_ref[...])
    h = (x1 * (1.0 + sc2_ref[0]) + sh2_ref[0]).astype(BF16)
    a = _dot(h, wg_ref[...])
    f = (a * jax.nn.sigmoid(a)) * _dot(h, wu_ref[...])
    f = _dot(f.astype(BF16), wd_ref[...])
    y_ref[0] = _layernorm(ALPHA * x1 + g2_ref[0] * f, ln2g_ref[...], ln2b_ref[...])


def _post(x, o, u, gb, st1, st2, mods, w, tm, seq):
    b, t, d = x.shape
    row = lambda cols: pl.BlockSpec((1, tm, cols), lambda i, j: (i, j, 0))

    def bcast(a):
        if a.shape[1] == t:
            return pl.BlockSpec((1, tm, a.shape[2]), lambda i, j: (i, j, 0))
        return pl.BlockSpec((1, 1, a.shape[2]), lambda i, j: (i, 0, 0))

    prev = pl.BlockSpec((1, 8, CONV_DIM), lambda i, j: (i, jnp.maximum(j * (tm // 8) - 1, 0), 0))
    consts = [w["conv_w"], w["g_conv"], w["g_nsa"], w["wo_c"], w["wo_n"], w["ln1_g"], w["ln1_b"], w["ln2_g"],
              w["ln2_b"], w["w_gate"], w["w_up"], w["w_down"]]
    return pl.pallas_call(
        functools.partial(_post_kernel, seq=seq),
        grid=(b, t // tm),
        in_specs=[row(d), row(QZ_DIM), row(CONV_DIM), prev, row(CONV_DIM), bcast(st1), bcast(st2)]
        + [bcast(m) for m in mods] + [_const_spec(c.shape) for c in consts],
        out_specs=row(d),
        out_shape=jax.ShapeDtypeStruct((b, t, d), F32),
        compiler_params=_cparams(2),
        name="post_block",
    )(x, o, u, u, gb, st1, st2, *mods, *consts)


def _prep_w_in(w_in):
    d = w_in.shape[0]
    c3 = 3 * CONV_DIM
    splits = (CONV_DIM, 2 * CONV_DIM, c3, c3 + NSA_DIM, c3 + NSA_DIM + KV_COLS, c3 + NSA_DIM + 2 * KV_COLS,
              c3 + NSA_DIM + 3 * KV_COLS)
    hc, gb, gc, q, kvc, kvs, kvw, gl = jnp.split(w_in, splits, axis=1)
    q4 = q.reshape(d, N_KV, Q_PER_KV, HEAD_DIM) * SCALE
    qz = jnp.einsum("dgrh,gk->dgrkh", q4, jnp.eye(N_KV, dtype=w_in.dtype)).reshape(d, QZ_DIM)
    glp = jnp.pad(gl, ((0, 0), (0, LANES - gl.shape[1])))
    return jnp.concatenate([hc, gb, gc, qz, kvc, kvs, kvw, glp], axis=1).astype(BF16)


def _prep_cmp(cmp_pe, cmp_w1, cmp_b1, cmp_w2):
    eye = jnp.eye(N_KV, dtype=cmp_w1.dtype)
    eye2 = jnp.eye(2, dtype=cmp_w1.dtype)
    w1r = cmp_w1.reshape(2, 2, CMP_STRIDE, HEAD_DIM, CMP_HID)
    wbig = jnp.einsum("etjdh,ef,gk->jegdtfkh", w1r, eye2, eye).reshape(CHUNK_COLS, TB_COLS).astype(BF16)
    w2big = jnp.einsum("ehd,ef,gk->eghfkd", cmp_w2, eye2, eye).reshape(TB_COLS // 2, KV_COLS).astype(BF16)
    w1p = cmp_w1.reshape(2, 2, CMP_STRIDE // 2, 2, HEAD_DIM, CMP_HID)
    wpair = jnp.einsum("etpjdh,gk->epjgdtkh", w1p, eye).reshape(
        2, CMP_STRIDE // 2, 2 * N_KV * HEAD_DIM, 2 * N_KV * CMP_HID).astype(BF16)
    w2pair = jnp.einsum("ehd,gk->eghkd", cmp_w2, eye).reshape(2, N_KV * CMP_HID, N_KV * HEAD_DIM).astype(BF16)
    pe2 = cmp_pe.reshape(2, CMP_BLOCK * HEAD_DIM)
    w1f = cmp_w1.reshape(2, CMP_BLOCK * HEAD_DIM, CMP_HID).astype(BF16)
    dense = (wbig, pe2, w1f, cmp_b1, w2big)
    paged = (wpair, wbig[:, TB_COLS // 2:], pe2, w1f, cmp_b1, w2pair)
    return dense, paged


def _prep_post(conv_w, g_conv, g_nsa, w_o, ln1_g, ln1_b, ln2_g, ln2_b, w_gate, w_up, w_down):
    d = w_o.shape[1]
    own = jnp.eye(N_KV, dtype=w_o.dtype)
    g_nsa_z = jnp.einsum("grh,gk->grkh", g_nsa.reshape(N_KV, Q_PER_KV, HEAD_DIM), own).reshape(1, QZ_DIM)
    wo_n = jnp.einsum("grhd,gk->grkhd", w_o[CONV_DIM:].reshape(N_KV, Q_PER_KV, HEAD_DIM, d), own)
    row = lambda a: a.reshape(1, -1)
    return dict(conv_w=conv_w, g_conv=row(g_conv), g_nsa=g_nsa_z, wo_c=w_o[:CONV_DIM].astype(BF16),
                wo_n=wo_n.reshape(QZ_DIM, d).astype(BF16), ln1_g=row(ln1_g), ln1_b=row(ln1_b),
                ln2_g=row(ln2_g), ln2_b=row(ln2_b), w_gate=w_gate.astype(BF16), w_up=w_up.astype(BF16),
                w_down=w_down.astype(BF16))


def _feature_major(cache):
    n, rows = cache.shape[:2]
    return jnp.transpose(cache, (0, 2, 3, 4, 1)).reshape(n, KV_COLS, rows)


def kernel(x_prompt, x_sample, cache_cmp_kv, cache_slc_kv, cache_win_kv, state_conv, page_table, c_prompt,
           c_sample, w_ada, b_ada, w_in, conv_w, cmp_pe, cmp_w1, cmp_b1, cmp_w2, g_conv_out, g_nsa_out, w_o,
           ln1_g, ln1_b, ln2_g, ln2_b, w_ffn_gate, w_ffn_up, w_ffn_down):
    assert w_ada.shape[0] == DEPTH
    bp, s, d = x_prompt.shape
    bs, t, _ = x_sample.shape
    kv_shape = (2, N_KV, HEAD_DIM)

    w_cat = _prep_w_in(w_in[0])
    cmp_dense_w, cmp_paged_w = _prep_cmp(cmp_pe[0], cmp_w1[0], cmp_b1[0], cmp_w2[0])
    post_w = _prep_post(conv_w[0], g_conv_out[0], g_nsa_out[0], w_o[0], ln1_g[0], ln1_b[0], ln2_g[0], ln2_b[0],
                        w_ffn_gate[0], w_ffn_up[0], w_ffn_down[0])

    mod = _modulation(jnp.concatenate([c_prompt, c_sample], axis=0), w_ada[0], b_ada[0])
    mods_p = [m[:, None, :] for m in jnp.split(mod[:bp], 6, axis=-1)]
    mods_s = [jnp.repeat(m, t, axis=0)[None] for m in jnp.split(mod[bp:], 6, axis=-1)]

    u, gb, qz, kvc, kvs, kvw, kvs_b, kvw_b, gates = _in_proj(x_prompt, mods_p[1], mods_p[0], w_cat, 512, BF16)
    kv_cmpr = _compress_dense(kvc, *cmp_dense_w)
    o = _attn_prompt(qz, kvs_b, kvw_b, kv_cmpr, gates)
    zero_state = jnp.zeros((bp, 1, CONV_DIM), F32)
    y_prompt = _post(x_prompt, o, u, gb, zero_state, zero_state, mods_p[2:], post_w, 512, s)
    w_keep = min(WINDOW, s)
    cmp_p = kvc.reshape(1, bp, s, *kv_shape)
    slc_p = kvs.reshape(1, bp, s, *kv_shape)
    win_p = kvw[:, s - w_keep:].reshape(1, bp, w_keep, *kv_shape)
    conv_p = u[:, s - (CONV_W - 1):][None]

    rows = bs * t
    us, gbs, qzs, kvcs, kvss, kvws, _, _, gates_s = _in_proj(
        x_sample.reshape(1, rows, d), mods_s[1], mods_s[0], w_cat, rows, F32)
    per_b = lambda a: a.reshape(bs, t, a.shape[-1])
    kvcs, kvss, kvws = per_b(kvcs), per_b(kvss), per_b(kvws)
    xnew = jnp.pad(kvcs, ((0, 0), (0, (-t) % CMP_STRIDE), (0, 0))).reshape(bs, 1, CHUNK_COLS)
    kv_cmpr_s = _compress_paged(_feature_major(cache_cmp_kv[0]), page_table, xnew, *cmp_paged_w)
    o_s = _attn_sample(per_b(qzs), kv_cmpr_s, per_b(gates_s), kvss, _feature_major(cache_slc_kv[0]),
                       _feature_major(cache_win_kv[0]), kvws, page_table)
    st = state_conv[0]
    tpos = jnp.arange(t)[None, :, None]
    st1 = jnp.where(tpos == 0, st[:, 1:2], 0.0).reshape(1, rows, CONV_DIM)
    st2 = jnp.where(tpos == 0, st[:, 0:1], jnp.where(tpos == 1, st[:, 1:2], 0.0)).reshape(1, rows, CONV_DIM)
    y_sample = _post(x_sample.reshape(1, rows, d), o_s.reshape(1, rows, QZ_DIM), us, gbs, st1, st2, mods_s[2:],
                     post_w, rows, t).reshape(bs, t, d)
    cmp_s = kvcs.reshape(1, bs, t, *kv_shape)
    slc_s = kvss.reshape(1, bs, t, *kv_shape)
    win_all = jnp.concatenate([cache_win_kv[0], kvws.reshape(bs, t, *kv_shape)], axis=1)
    win_s = win_all[:, t:][None]
    conv_s = jnp.concatenate([st, per_b(us)], axis=1)[:, t:][None]
    return (y_prompt, y_sample, cmp_p, slc_p, win_p, conv_p, cmp_s, slc_s, win_s, conv_s)
```

```python
import functools

import jax
import jax.numpy as jnp
from jax import lax
from jax.experimental import pallas as pl
from jax.experimental.pallas import tpu as pltpu

HEAD_DIM = 64
N_KV = 2
Q_PER_KV = 6
N_HEADS = N_KV * Q_PER_KV
CONV_DIM = 256
CONV_W = 3
NSA_DIM = N_HEADS * HEAD_DIM
KV_COLS = 2 * N_KV * HEAD_DIM
CMP_BLOCK = 32
CMP_STRIDE = 16
CMP_HID = 2 * HEAD_DIM
SEL_BLOCK = 64
CMP_PER_SEL = SEL_BLOCK // CMP_STRIDE
N_SELECT = 16
WINDOW = 512
Q_BLOCK = 128
DEPTH = 1
ALPHA = (2 * DEPTH) ** 0.25
EPS = 1e-5
SCALE = HEAD_DIM ** -0.5

LANES = 128
QZ_DIM = N_HEADS * LANES
CHUNK_COLS = CMP_STRIDE * KV_COLS
TB_COLS = 2 * 2 * N_KV * CMP_HID
NEG = -1e30
VMEM_LIMIT = 56 * 1024 * 1024

F32 = jnp.float32
BF16 = jnp.bfloat16


def _cparams(n_grid):
    return pltpu.CompilerParams(dimension_semantics=("arbitrary",) * n_grid, vmem_limit_bytes=VMEM_LIMIT)


def _const_spec(shape):
    nd = len(shape)
    return pl.BlockSpec(shape, lambda *_: (0,) * nd, pipeline_mode=pl.Buffered(1))


def _dot(a, b):
    return jnp.dot(a, b, preferred_element_type=F32)


def _dot_nt(a, b):
    return lax.dot_general(a, b, (((1,), (1,)), ((), ())), preferred_element_type=F32)


def _iota(shape, dim):
    return lax.broadcasted_iota(jnp.int32, shape, dim)


def _lane_tiles(x):
    return [x[:, w * LANES:(w + 1) * LANES] for w in range(x.shape[1] // LANES)]


def _mod_kernel(c_ref, w_ref, b_ref, o_ref):
    c = c_ref[...]
    a = (c * jax.nn.sigmoid(c)).astype(BF16)
    o_ref[...] = _dot(a, w_ref[...].astype(BF16)) + b_ref[...]


def _modulation(c_all, w_ada, b_ada):
    m, d = c_all.shape
    n = w_ada.shape[1]
    tn = 512
    return pl.pallas_call(
        _mod_kernel,
        grid=(n // tn,),
        in_specs=[pl.BlockSpec((m, d), lambda j: (0, 0)),
                  pl.BlockSpec((d, tn), lambda j: (0, j)),
                  pl.BlockSpec((1, tn), lambda j: (0, j))],
        out_specs=pl.BlockSpec((m, tn), lambda j: (0, j)),
        out_shape=jax.ShapeDtypeStruct((m, n), F32),
        compiler_params=_cparams(1),
        name="modulation",
    )(c_all, w_ada, b_ada.reshape(1, n))


_C_HC, _C_GB, _C_GC, _C_Q = 0, CONV_DIM, 2 * CONV_DIM, 3 * CONV_DIM
_C_KVC = _C_Q + QZ_DIM
_C_KVS = _C_KVC + KV_COLS
_C_KVW = _C_KVS + KV_COLS
_C_GL = _C_KVW + KV_COLS
W_IN_COLS = _C_GL + LANES


def _inproj_kernel(x_ref, sc_ref, sh_ref, w_ref, u_ref, gb_ref, qz_ref, kvc_ref, kvs_ref, kvw_ref,
                   kvsb_ref, kvwb_ref, gates_ref):
    h = (x_ref[0] * (1.0 + sc_ref[0]) + sh_ref[0]).astype(BF16)
    hc = _dot(h, w_ref[:, _C_HC:_C_GB])
    gb_ref[0] = _dot(h, w_ref[:, _C_GB:_C_GC])
    gc = _dot(h, w_ref[:, _C_GC:_C_Q])
    u_ref[0] = gc * hc
    qz_ref[0] = _dot(h, w_ref[:, _C_Q:_C_KVC]).astype(qz_ref.dtype)
    kvc_ref[0] = _dot(h, w_ref[:, _C_KVC:_C_KVS])
    kvs = _dot(h, w_ref[:, _C_KVS:_C_KVW])
    kvs_ref[0] = kvs
    kvsb_ref[0] = kvs.astype(BF16)
    kvw = _dot(h, w_ref[:, _C_KVW:_C_GL])
    kvw_ref[0] = kvw
    kvwb_ref[0] = kvw.astype(BF16)
    gates_ref[0] = jax.nn.sigmoid(_dot(h, w_ref[:, _C_GL:W_IN_COLS]))


def _in_proj(x, scale, shift, w_cat, tm, qz_dtype):
    b, t, d = x.shape
    rm = scale.shape[1]
    mod_block = (1, tm, d) if rm == t else (1, 1, d)
    mod_map = (lambda i, j: (i, j, 0)) if rm == t else (lambda i, j: (i, 0, 0))
    row = lambda cols: pl.BlockSpec((1, tm, cols), lambda i, j: (i, j, 0))
    sds = lambda cols, dt: jax.ShapeDtypeStruct((b, t, cols), dt)
    return pl.pallas_call(
        _inproj_kernel,
        grid=(b, t // tm),
        in_specs=[row(d), pl.BlockSpec(mod_block, mod_map), pl.BlockSpec(mod_block, mod_map),
                  _const_spec((d, W_IN_COLS))],
        out_specs=[row(CONV_DIM), row(CONV_DIM), row(QZ_DIM), row(KV_COLS), row(KV_COLS), row(KV_COLS),
                   row(KV_COLS), row(KV_COLS), row(LANES)],
        out_shape=[sds(CONV_DIM, F32), sds(CONV_DIM, F32), sds(QZ_DIM, qz_dtype), sds(KV_COLS, F32),
                   sds(KV_COLS, F32), sds(KV_COLS, F32), sds(KV_COLS, BF16), sds(KV_COLS, BF16),
                   sds(LANES, F32)],
        compiler_params=_cparams(2),
        name="in_proj",
    )(x, scale, shift, w_cat)


def _gelu_tanh(x):
    return jax.nn.gelu(x, approximate=True)


def _cmp_bias_e(pe_ref, w1f_ref, b1_ref, e):
    pe = jnp.broadcast_to(pe_ref[e:e + 1, :], (8, pe_ref.shape[1])).astype(BF16)
    return _dot(pe, w1f_ref[e])[0:1, :] + b1_ref[e:e + 1, :]


def _compress_dense_kernel(x_ref, wbig_ref, pe_ref, w1f_ref, b1_ref, w2_ref, o_ref):
    half = TB_COLS // 2
    tb = _dot(x_ref[0].astype(BF16), wbig_ref[...])
    n = tb.shape[0]
    top = tb[:, :half]
    bot_next = pltpu.roll(tb[:, half:], n - 1, 0)
    bias = jnp.concatenate([_cmp_bias_e(pe_ref, w1f_ref, b1_ref, e) for e in (0, 0, 1, 1)], axis=1)
    hid = _gelu_tanh(top + bot_next + bias)
    o_ref[0] = _dot(hid.astype(BF16), w2_ref[...]).astype(o_ref.dtype)


def _compress_dense(kvc, wbig, pe2, w1f, b1, w2big):
    b, s, _ = kvc.shape
    nch = s // CMP_STRIDE
    x = kvc.reshape(b, nch, CHUNK_COLS)
    return pl.pallas_call(
        _compress_dense_kernel,
        grid=(b,),
        in_specs=[pl.BlockSpec((1, nch, CHUNK_COLS), lambda i: (i, 0, 0)),
                  _const_spec(wbig.shape), _const_spec(pe2.shape), _const_spec(w1f.shape),
                  _const_spec(b1.shape), _const_spec(w2big.shape)],
        out_specs=pl.BlockSpec((1, nch, KV_COLS), lambda i: (i, 0, 0)),
        out_shape=jax.ShapeDtypeStruct((b, nch, KV_COLS), BF16),
        compiler_params=_cparams(1),
        name="compress_prompt",
    )(x, wbig, pe2, w1f, b1, w2big)


PAGES_PER_STEP = 32
CHUNK_PITCH = 24


def _page_copy(pool_ref, page_index, buf, sem, slot, p):
    return pltpu.make_async_copy(pool_ref.at[page_index], buf.at[slot, p], sem.at[slot, p])


def _await_pages(pt_ref, pool_ref, buf, sem, n_steps):
    b, kt = pl.program_id(0), pl.program_id(1)

    @pl.when((b == 0) & (kt == 0))
    def _():
        for k in range(n_steps):
            for p in range(PAGES_PER_STEP):
                _page_copy(pool_ref, pt_ref[0, k * PAGES_PER_STEP + p], buf, sem, k, p).start(priority=p % 2)

    for p in range(PAGES_PER_STEP):
        _page_copy(pool_ref, 0, buf, sem, kt, p).wait()
    return kt


def _refill_pages(pt_ref, pool_ref, buf, sem):
    kt = pl.program_id(1)
    nb = jnp.minimum(pl.program_id(0) + 1, pl.num_programs(0) - 1)
    for p in range(PAGES_PER_STEP):
        _page_copy(pool_ref, pt_ref[nb, kt * PAGES_PER_STEP + p], buf, sem, kt, p).start(priority=p % 2)


def _drain_pages(pool_ref, buf, sem, n_steps):
    @pl.when((pl.program_id(0) == pl.num_programs(0) - 1) & (pl.program_id(1) == n_steps - 1))
    def _():
        for k in range(n_steps):
            for p in range(PAGES_PER_STEP):
                _page_copy(pool_ref, 0, buf, sem, k, p).wait()


def _compress_paged_kernel(pt_ref, pool_ref, *refs, n_steps):
    g_pages = PAGES_PER_STEP
    (xnew_ref, wpair_ref, wbot_ref, pe_ref, w1f_ref, b1_ref, w2_ref, o_ref, r_sc, tb_sc, buf, sem) = refs
    kt = pl.program_id(1)
    page = buf.shape[3]
    slot = _await_pages(pt_ref, pool_ref, buf, sem, n_steps)
    cpp = page // CMP_STRIDE
    rows_step = g_pages * cpp
    half_cols = N_KV * HEAD_DIM
    hid_cols = N_KV * CMP_HID

    row0 = pl.multiple_of(kt * rows_step, rows_step)
    for e in range(2):
        for p in range(g_pages):
            rows_t = buf[slot, p, e * half_cols:(e + 1) * half_cols, :].T
            for c in range(cpp):
                r_sc[e, pl.ds((p * cpp + c) * CHUNK_PITCH, CMP_STRIDE), :] = rows_t[c * CMP_STRIDE:(c + 1) * CMP_STRIDE]
        acc = None
        for jp in range(CMP_STRIDE // 2):
            xs = jnp.concatenate([r_sc[e, pl.ds(2 * jp + jj, rows_step, stride=CHUNK_PITCH), :] for jj in (0, 1)],
                                 axis=1).astype(BF16)
            part = _dot(xs, wpair_ref[e, jp])
            acc = part if acc is None else acc + part
        tb_sc[e, pl.ds(row0, rows_step), :] = acc
    _refill_pages(pt_ref, pool_ref, buf, sem)

    @pl.when(kt == n_steps - 1)
    def _():
        n = tb_sc.shape[1]
        xn = jnp.broadcast_to(xnew_ref[0], (8, CHUNK_COLS)).astype(BF16)
        bot_new = _dot(xn, wbot_ref[...])[0:1, :]
        last = _iota((n, 1), 0) == n - 1
        outs = []
        for e in range(2):
            top, bot = tb_sc[e, :, :hid_cols], tb_sc[e, :, hid_cols:]
            bot_next = jnp.where(last, bot_new[:, e * hid_cols:(e + 1) * hid_cols], pltpu.roll(bot, n - 1, 0))
            bias = jnp.concatenate([_cmp_bias_e(pe_ref, w1f_ref, b1_ref, e)] * N_KV, axis=1)
            hid = _gelu_tanh(top + bot_next + bias)
            outs.append(_dot(hid.astype(BF16), w2_ref[e]))
        o_ref[0] = jnp.concatenate(outs, axis=1).astype(o_ref.dtype)

    _drain_pages(pool_ref, buf, sem, n_steps)


def _page_stream_scratch(page, n_steps):
    return [pltpu.VMEM((n_steps, PAGES_PER_STEP, KV_COLS, page), F32),
            pltpu.SemaphoreType.DMA((n_steps, PAGES_PER_STEP))]


def _compress_paged(pool_t, page_table, xnew, wpair, wbot, pe2, w1f, b1, w2pair):
    bn, n_pages = page_table.shape
    page = pool_t.shape[2]
    g = PAGES_PER_STEP
    n_steps = n_pages // g
    assert n_pages % g == 0 and page % CMP_STRIDE == 0 and page == LANES
    n = n_pages * page // CMP_STRIDE

    def const(shape):
        nd = len(shape)
        return pl.BlockSpec(shape, lambda i, j, pt: (0,) * nd, pipeline_mode=pl.Buffered(1))

    grid_spec = pltpu.PrefetchScalarGridSpec(
        num_scalar_prefetch=1,
        grid=(bn, n_steps),
        in_specs=[pl.BlockSpec(memory_space=pl.ANY),
                  pl.BlockSpec((1, 1, CHUNK_COLS), lambda i, j, pt: (i, 0, 0)), const(wpair.shape), const(wbot.shape),
                  const(pe2.shape), const(w1f.shape), const(b1.shape), const(w2pair.shape)],
        out_specs=pl.BlockSpec((1, n, KV_COLS), lambda i, j, pt: (i, 0, 0)),
        scratch_shapes=[pltpu.VMEM((2, g * (page // CMP_STRIDE) * CHUNK_PITCH, LANES), F32),
                        pltpu.VMEM((2, n, 2 * N_KV * CMP_HID), F32)] + _page_stream_scratch(page, n_steps),
    )
    return pl.pallas_call(
        functools.partial(_compress_paged_kernel, n_steps=n_steps),
        grid_spec=grid_spec,
        out_shape=jax.ShapeDtypeStruct((bn, n, KV_COLS), BF16),
        compiler_params=_cparams(2),
        name="compress_sample",
    )(page_table, pool_t, xnew, wpair, wbot, pe2, w1f, b1, w2pair)


def _split3_dot(x, m01):
    h1 = x.astype(BF16)
    r1 = x - h1.astype(F32)
    h2 = r1.astype(BF16)
    h3 = (r1 - h2.astype(F32)).astype(BF16)
    return _dot(h1, m01) + _dot(h2, m01) + _dot(h3, m01)


def _block_scores(imp, n_lanes):
    ncp = imp.shape[1]
    c = _iota((ncp, n_lanes), 0)
    j = _iota((ncp, n_lanes), 1)
    a = jnp.where((c >= CMP_PER_SEL * j - 1) & (c <= CMP_PER_SEL * j + CMP_PER_SEL - 1), 1.0, 0.0).astype(BF16)
    return _split3_dot(imp, a)


def _ranked_scores(score, qpos, n_blocks):
    j = _iota(score.shape, 1)
    cur = qpos >> 6
    valid = j * SEL_BLOCK <= qpos
    forced = (j == 0) | (j == cur) | (j == cur - 1)
    return jnp.where(forced, -NEG, jnp.where(valid, score, NEG))


def _select_blocks_iter(score, qpos, n_blocks, n_select):
    r, l = score.shape
    removed = 3.0 * NEG
    jf = _iota((r, l), 1).astype(F32)
    sc = jnp.where(jf < n_blocks, _ranked_scores(score, qpos, n_blocks), removed)

    sel = jnp.zeros((r, l), F32)
    for _ in range(n_select):
        mx = jnp.max(sc, axis=-1, keepdims=True)
        first = jnp.min(jnp.where(sc == mx, jf, 1e9), axis=-1, keepdims=True)
        pick = jf == first
        sc, sel = jnp.where(pick, removed, sc), jnp.where(pick, 1.0, sel)
    return sel


def _select_blocks_rank(score, qpos, n_blocks, n_select):
    r, l = score.shape
    nb8 = -(-n_blocks // 8) * 8
    st = _ranked_scores(score, qpos, n_blocks).T[:nb8]
    jrow = _iota((nb8, r), 0)
    rank = jnp.zeros((nb8, r), F32)
    for jp in range(n_blocks):
        row = st[jp:jp + 1, :]
        beats = (row > st) | ((row == st) & (jrow > jp))
        rank = rank + jnp.where(beats, 1.0, 0.0)
    sel_t = jnp.where((rank < n_select) & (jrow < n_blocks), 1.0, 0.0)
    if nb8 < l:
        sel_t = jnp.concatenate([sel_t, jnp.zeros((l - nb8, r), F32)], axis=0)
    return sel_t.T


def _softmax_parts(s, mask):
    s = jnp.where(mask, s, NEG)
    m = jnp.max(s, axis=-1, keepdims=True)
    e = jnp.where(mask, jnp.exp(s - m), 0.0)
    return e, jnp.sum(e, axis=-1, keepdims=True)


SEL_TILE = 512


def _attn_prompt_kernel(qz_ref, kvs_ref, kvw_ref, kvc_ref, gates_ref, exp_ref, o_ref,
                        s_sc, p_sc, pw_sc, m_sc, acc_sc, ocw_sc, *, seq):
    qb = Q_BLOCK
    tk = SEL_TILE
    start = pl.program_id(1) * qb
    qpos = start + _iota((qb, 1), 0)
    ncp = kvc_ref.shape[1]
    n_blocks = -(-seq // SEL_BLOCK)
    rows = Q_PER_KV * qb
    gates = gates_ref[0]
    n_tiles = (start + qb + tk - 1) // tk
    head_rows = [slice(r * qb, (r + 1) * qb) for r in range(Q_PER_KV)]
    kv_heads = range(N_KV)

    def q_rows(g):
        return jnp.concatenate(
            [qz_ref[0, :, (g * Q_PER_KV + r) * LANES:(g * Q_PER_KV + r + 1) * LANES] for r in range(Q_PER_KV)],
            axis=0)


    def values_and_ones(g, vv):
        return jnp.where((_iota((1, LANES), 1) >> 6) == g, vv, jnp.ones_like(vv))

    def normalised(acc):
        return acc / jnp.maximum(pltpu.roll(acc, LANES // 2, 1), 1e-30)

    o_cmp, sel = [], []
    c = _iota((qb, ncp), 1)
    cmask = (CMP_STRIDE * c + (CMP_BLOCK - 1) <= qpos) & (c < ncp - 1)
    for g in kv_heads:
        s = _dot_nt(q_rows(g), kvc_ref[0, :, 0:LANES]).reshape(Q_PER_KV, qb, ncp)
        e, l = _softmax_parts(s, cmask)
        p = e / jnp.maximum(l, 1e-30)
        o_cmp.append(_dot(p.reshape(rows, ncp).astype(BF16), kvc_ref[0, :, LANES:2 * LANES]))
        imp = jnp.sum(p, axis=0)
        sel.append(_select_blocks_rank(_block_scores(imp, LANES), qpos, n_blocks,
                                       min(N_SELECT, n_blocks)).astype(BF16))

    wlen = WINDOW + qb
    w0 = pl.multiple_of(jnp.maximum(start - WINDOW, 0), qb)
    kpos = w0 + _iota((1, wlen), 1)
    wmask = (kpos <= qpos) & (kpos > qpos - WINDOW)
    for g in kv_heads:
        s = _dot_nt(q_rows(g), kvw_ref[0, pl.ds(w0, wlen), 0:LANES])
        for rs in head_rows:
            sr = jnp.where(wmask, s[rs], NEG)
            m = jnp.max(functools.reduce(jnp.maximum, _lane_tiles(sr)), axis=-1, keepdims=True)
            pw_sc[g, rs, :] = jnp.exp((sr - m).astype(BF16))
        o_win = normalised(_dot(pw_sc[g], values_and_ones(g, kvw_ref[0, pl.ds(w0, wlen), LANES:2 * LANES])))
        for r, rs in enumerate(head_rows):
            h = g * Q_PER_KV + r
            ocw_sc[g, rs, :] = (gates[:, h:h + 1] * o_cmp[g][rs]
                                + gates[:, 2 * N_HEADS + h:2 * N_HEADS + h + 1] * o_win[rs])

    m_sc[...] = jnp.full(m_sc.shape, NEG, F32)

    def pass1(kt, _):
        k0 = pl.multiple_of(kt * tk, tk)
        causal = (k0 + _iota((1, tk), 1)) <= qpos
        for g in kv_heads:
            s = _dot_nt(q_rows(g), kvs_ref[0, pl.ds(k0, tk), 0:LANES])
            mask = (_dot(sel[g], exp_ref[kt]) > 0.5) & causal
            for rs in head_rows:
                sr = jnp.where(mask, s[rs], NEG)
                s_sc[g, kt, rs, :] = sr
                m_sc[g, rs, :] = jnp.maximum(m_sc[g, rs, :], functools.reduce(jnp.maximum, _lane_tiles(sr)))
        return 0

    lax.fori_loop(0, n_tiles, pass1, 0)
    m_sc[...] = jnp.broadcast_to(jnp.max(m_sc[...], axis=-1, keepdims=True), m_sc.shape)
    acc_sc[...] = jnp.zeros(acc_sc.shape, F32)

    def pass2(kt, _):
        k0 = pl.multiple_of(kt * tk, tk)
        for g in kv_heads:
            for rs in head_rows:
                x = s_sc[g, kt, rs, :] - jnp.concatenate([m_sc[g, rs, :]] * (tk // LANES), axis=1)
                p_sc[g, rs, :] = jnp.exp(x.astype(BF16))
            vv = values_and_ones(g, kvs_ref[0, pl.ds(k0, tk), LANES:2 * LANES])
            acc_sc[g] = acc_sc[g] + _dot(p_sc[g], vv)
        return 0

    lax.fori_loop(0, n_tiles, pass2, 0)

    for g in kv_heads:
        o_slc = normalised(acc_sc[g])
        own = (_iota((qb, LANES), 1) >> 6) == g
        for r, rs in enumerate(head_rows):
            h = g * Q_PER_KV + r
            o = ocw_sc[g, rs, :] + gates[:, N_HEADS + h:N_HEADS + h + 1] * o_slc[rs]
            o_ref[0, :, h * LANES:(h + 1) * LANES] = jnp.where(own, o, 0.0)


def _expand_matrix(n_tiles, tile):
    t = jnp.arange(n_tiles)[:, None, None]
    j = jnp.arange(LANES)[None, :, None]
    k = jnp.arange(tile)[None, None, :]
    return (j == (t * tile + k) // SEL_BLOCK).astype(BF16)


def _attn_prompt(qz, kvs_b, kvw_b, kv_cmpr, gates):
    b, s, _ = qz.shape
    assert s % Q_BLOCK == 0 and s % SEL_TILE == 0 and s >= WINDOW + Q_BLOCK and s // SEL_BLOCK <= LANES
    ncp = kv_cmpr.shape[1]
    n_tiles = s // SEL_TILE
    expand = _expand_matrix(n_tiles, SEL_TILE)
    rows = Q_PER_KV * Q_BLOCK
    full = lambda n: pl.BlockSpec((1, n, KV_COLS), lambda i, j: (i, 0, 0))
    return pl.pallas_call(
        functools.partial(_attn_prompt_kernel, seq=s),
        grid=(b, s // Q_BLOCK),
        in_specs=[pl.BlockSpec((1, Q_BLOCK, QZ_DIM), lambda i, j: (i, j, 0)), full(s), full(s), full(ncp),
                  pl.BlockSpec((1, Q_BLOCK, LANES), lambda i, j: (i, j, 0)), _const_spec(expand.shape)],
        out_specs=pl.BlockSpec((1, Q_BLOCK, QZ_DIM), lambda i, j: (i, j, 0)),
        out_shape=jax.ShapeDtypeStruct((b, s, QZ_DIM), F32),
        scratch_shapes=[pltpu.VMEM((N_KV, n_tiles, rows, SEL_TILE), F32), pltpu.VMEM((N_KV, rows, SEL_TILE), BF16),
                        pltpu.VMEM((N_KV, rows, WINDOW + Q_BLOCK), BF16), pltpu.VMEM((N_KV, rows, LANES), F32),
                        pltpu.VMEM((N_KV, rows, LANES), F32), pltpu.VMEM((N_KV, rows, LANES), F32)],
        compiler_params=_cparams(2),
        name="attn_prompt",
    )(qz, kvs_b, kvw_b, kv_cmpr, gates, expand)


def _attn_sample_kernel(pt_ref, pool_ref, *refs, past, n_steps):
    g_pages = PAGES_PER_STEP
    (qz_ref, kvc_ref, gates_ref, kvs_new_ref, win_ref, kvw_new_ref, exp_ref,
     o_ref, q_sc, sel_sc, s_sc, v_sc, snew_sc, mp_sc, ocw_sc, buf, sem) = refs
    kt = pl.program_id(1)
    t = qz_ref.shape[1]
    rows = N_HEADS * t
    page = buf.shape[3]
    tile = g_pages * page
    blocks_per_tile = tile // SEL_BLOCK
    n_past_blocks = past // SEL_BLOCK
    n_blocks = n_past_blocks + -(-t // SEL_BLOCK)
    sel_lanes = sel_sc.shape[0] * LANES
    half_cols = N_KV * HEAD_DIM
    qpos = past + _iota((t, 1), 0)
    qpos_gt = jnp.concatenate([qpos] * N_KV, axis=0)
    own_rows = _iota((N_KV, Q_PER_KV, t, LANES), 0) == (_iota((N_KV, Q_PER_KV, t, LANES), 3) >> 6)

    def new_rows_padded():
        kvn = kvs_new_ref[0]
        return jnp.concatenate([kvn, jnp.zeros((LANES - t, KV_COLS), F32)], axis=0).astype(BF16)

    def masked_scores(s, mask):
        k = s.shape[1]
        return jnp.where(mask.reshape(N_KV, 1, t, k), s.reshape(N_KV, Q_PER_KV, t, k), NEG).reshape(rows, k)

    @pl.when(kt == 0)
    def _():
        q = jnp.concatenate([qz_ref[0, :, h * LANES:(h + 1) * LANES] for h in range(N_HEADS)], axis=0)
        q_sc[...] = q.astype(BF16)
        qb = q_sc[...]
        ncp = kvc_ref.shape[1]
        s = _dot_nt(qb, kvc_ref[0, :, 0:LANES]).reshape(N_HEADS, t, ncp)
        c = _iota((t, ncp), 1)
        cmask = CMP_STRIDE * c + (CMP_BLOCK - 1) <= qpos
        e, l = _softmax_parts(s, cmask)
        p = e / jnp.maximum(l, 1e-30)
        o_cmp = _dot(p.reshape(rows, ncp).astype(BF16), kvc_ref[0, :, LANES:2 * LANES])
        imp = jnp.sum(p.reshape(N_KV, Q_PER_KV, t, ncp), axis=1).reshape(N_KV * t, ncp)
        sel = _select_blocks_iter(_block_scores(imp, sel_lanes), qpos_gt, n_blocks, min(N_SELECT, n_blocks))
        for w in range(sel_sc.shape[0]):
            sel_sc[w] = sel[:, w * LANES:(w + 1) * LANES]
        wb = win_ref.shape[2]
        win = win_ref[0].astype(BF16)
        kvw_new = kvw_new_ref[0].astype(BF16)
        s_old = _dot(qb, win[0:half_cols]).reshape(N_HEADS, t, wb)
        s_new = _dot_nt(qb, kvw_new[:, 0:half_cols]).reshape(N_HEADS, t, t)
        kpos_old = past - wb + _iota((1, wb), 1)
        kpos_new = past + _iota((1, t), 1)
        mask_old = (kpos_old <= qpos) & (kpos_old > qpos - WINDOW) & (kpos_old >= 0)
        mask_new = (kpos_new <= qpos) & (kpos_new > qpos - WINDOW)
        s_old = jnp.where(mask_old, s_old, NEG)
        s_new = jnp.where(mask_new, s_new, NEG)
        m = jnp.maximum(jnp.max(s_old, axis=-1, keepdims=True), jnp.max(s_new, axis=-1, keepdims=True))
        e_old = jnp.where(mask_old, jnp.exp(s_old - m), 0.0)
        e_new = jnp.where(mask_new, jnp.exp(s_new - m), 0.0)
        l = jnp.sum(e_old, axis=-1, keepdims=True) + jnp.sum(e_new, axis=-1, keepdims=True)
        o_win = (_dot_nt(e_old.reshape(rows, wb).astype(BF16), win[half_cols:])
                 + _dot(e_new.reshape(rows, t).astype(BF16), kvw_new[:, half_cols:]))
        o_win = o_win.reshape(N_HEADS, t, LANES) / jnp.maximum(l, 1e-30)
        gts = gates_ref[0]
        gc = jnp.stack([jnp.broadcast_to(gts[:, h:h + 1], (t, LANES)) for h in range(N_HEADS)])
        gw = jnp.stack([jnp.broadcast_to(gts[:, 2 * N_HEADS + h:2 * N_HEADS + h + 1], (t, LANES))
                        for h in range(N_HEADS)])
        ocw_sc[...] = gc * o_cmp.reshape(N_HEADS, t, LANES) + gw * o_win
        jn = (past + _iota((1, LANES), 1)) >> 6
        seln = jnp.zeros((N_KV * t, LANES), F32)
        for jb in range(n_past_blocks, n_blocks):
            seln = jnp.where(jn == jb, sel[:, jb:jb + 1], seln)
        npos = past + _iota((1, LANES), 1)
        nmask = (seln > 0.5) & (npos <= qpos_gt) & (npos < past + t)
        s_n = masked_scores(_dot_nt(qb, new_rows_padded()[:, 0:half_cols]), nmask)
        snew_sc[...] = s_n
        mp_sc[...] = s_n

    slot = _await_pages(pt_ref, pool_ref, buf, sem, n_steps)
    k_t = jnp.concatenate([buf[slot, p, 0:half_cols, :] for p in range(g_pages)], axis=1).astype(BF16)
    v_sc[kt] = jnp.concatenate([buf[slot, p, half_cols:, :] for p in range(g_pages)], axis=1).astype(BF16)
    b0 = kt * blocks_per_tile
    selw = sel_sc[b0 // LANES].astype(BF16)
    kpos = kt * tile + _iota((1, tile), 1)
    mask = (_dot(selw, exp_ref[0]) > 0.5) & (kpos <= qpos_gt)
    s = masked_scores(_dot(q_sc[...], k_t), mask)
    s_sc[kt] = s
    mp_sc[...] = jnp.maximum(mp_sc[...], functools.reduce(jnp.maximum, _lane_tiles(s)))
    _refill_pages(pt_ref, pool_ref, buf, sem)

    @pl.when(kt == n_steps - 1)
    def _():
        m = jnp.max(mp_sc[...], axis=-1, keepdims=True)
        e = jnp.exp(snew_sc[...] - m)
        lp = e
        acc = _dot(e.astype(BF16), new_rows_padded()[:, half_cols:])
        for k2 in range(n_steps):
            e = jnp.exp(s_sc[k2] - m)
            lp = lp + functools.reduce(jnp.add, _lane_tiles(e))
            acc = acc + _dot_nt(e.astype(BF16), v_sc[k2])
        l = jnp.sum(lp, axis=-1, keepdims=True)
        gts = gates_ref[0]
        gs = jnp.stack([jnp.broadcast_to(gts[:, N_HEADS + h:N_HEADS + h + 1], (t, LANES))
                        for h in range(N_HEADS)])
        o_slc = (acc / jnp.maximum(l, 1e-30)).reshape(N_HEADS, t, LANES)
        o = jnp.where(own_rows.reshape(N_HEADS, t, LANES), ocw_sc[...] + gs * o_slc, 0.0)
        for h in range(N_HEADS):
            o_ref[0, :, h * LANES:(h + 1) * LANES] = o[h]

    _drain_pages(pool_ref, buf, sem, n_steps)


def _attn_sample(qz, kv_cmpr, gates, kvs_new, slc_pool_t, win_t, kvw_new, page_table):
    bn, t, _ = qz.shape
    n_pages = page_table.shape[1]
    page = slc_pool_t.shape[2]
    past = n_pages * page
    g = PAGES_PER_STEP
    n_steps = n_pages // g
    tile = g * page
    blocks_per_tile = tile // SEL_BLOCK
    assert n_pages % g == 0 and page % SEL_BLOCK == 0 and LANES % blocks_per_tile == 0 and t % 8 == 0
    n_blocks = past // SEL_BLOCK + -(-t // SEL_BLOCK)
    sel_groups = -(-n_blocks // LANES)
    ncp = kv_cmpr.shape[1]
    wb = win_t.shape[2]
    n_var = LANES // blocks_per_tile
    expand = _expand_matrix(n_var, tile)

    per_b = lambda n, cols: pl.BlockSpec((1, n, cols), lambda i, j, pt: (i, 0, 0))
    grid_spec = pltpu.PrefetchScalarGridSpec(
        num_scalar_prefetch=1,
        grid=(bn, n_steps),
        in_specs=[pl.BlockSpec(memory_space=pl.ANY),
                  per_b(t, QZ_DIM), per_b(ncp, KV_COLS), per_b(t, LANES), per_b(t, KV_COLS), per_b(KV_COLS, wb),
                  per_b(t, KV_COLS), pl.BlockSpec((1, LANES, tile), lambda i, j, pt: (j % n_var, 0, 0))],
        out_specs=per_b(t, QZ_DIM),
        scratch_shapes=[pltpu.VMEM((N_HEADS * t, LANES), BF16),
                        pltpu.VMEM((sel_groups, N_KV * t, LANES), F32),
                        pltpu.VMEM((n_steps, N_HEADS * t, tile), F32), pltpu.VMEM((n_steps, LANES, tile), BF16),
                        pltpu.VMEM((N_HEADS * t, LANES), F32), pltpu.VMEM((N_HEADS * t, LANES), F32),
                        pltpu.VMEM((N_HEADS, t, LANES), F32)] + _page_stream_scratch(page, n_steps),
    )
    return pl.pallas_call(
        functools.partial(_attn_sample_kernel, past=past, n_steps=n_steps),
        grid_spec=grid_spec,
        out_shape=jax.ShapeDtypeStruct((bn, t, QZ_DIM), F32),
        compiler_params=_cparams(2),
        name="attn_sample",
    )(page_table, slc_pool_t, qz, kv_cmpr, gates, kvs_new, win_t, kvw_new, expand)


def _layernorm(x, g, b):
    mu = jnp.mean(x, axis=-1, keepdims=True)
    xc = x - mu
    var = jnp.mean(xc * xc, axis=-1, keepdims=True)
    return xc * lax.rsqrt(var + EPS) * g + b


def _post_kernel(x_ref, o_ref, u_ref, uprev_ref, gb_ref, st1_ref, st2_ref, g1_ref, sh2_ref, sc2_ref, g2_ref,
                 convw_ref, gconv_ref, gnsa_ref, woc_ref, won_ref, ln1g_ref, ln1b_ref, ln2g_ref, ln2b_ref,
                 wg_ref, wu_ref, wd_ref, y_ref, *, seq):
    tm = x_ref.shape[1]
    u = u_ref[0]
    ext = jnp.concatenate([uprev_ref[0], u], axis=0)
    if tm <= seq:
        pos = (pl.program_id(1) * tm) % seq + _iota((tm, 1), 0)
    else:
        pos = lax.rem(_iota((tm, 1), 0), seq)
    p1 = jnp.where(pos >= 1, ext[7:7 + tm], st1_ref[0])
    p2 = jnp.where(pos >= 2, ext[6:6 + tm], st2_ref[0])
    cw = convw_ref[...]
    y_c = gb_ref[0] * (cw[0:1] * p2 + cw[1:2] * p1 + cw[2:3] * u)
    yn = y_c * lax.rsqrt(jnp.mean(y_c * y_c, axis=-1, keepdims=True) + EPS) * gconv_ref[...]
    o = o_ref[0]
    on = o * lax.rsqrt(jnp.sum(o * o, axis=-1, keepdims=True) * (1.0 / NSA_DIM) + EPS) * gnsa_ref[...]
    mix = _dot(yn.astype(BF16), woc_ref[...]) + _dot(on.astype(BF16), won_ref[...])
    x1 = _layernorm(ALPHA * x_ref[0] + g1_ref[0] * mix, ln1g_ref[...], ln1b_ref[...])
    h = (x1 * (1.0 + sc2_ref[0]) + sh2_ref[0]).astype(BF16)
    a = _dot(h, wg_ref[...])
    f = (a * jax.nn.sigmoid(a)) * _dot(h, wu_ref[...])
    f = _dot(f.astype(BF16), wd_ref[...])
    y_ref[0] = _layernorm(ALPHA * x1 + g2_ref[0] * f, ln2g_ref[...], ln2b_ref[...])


def _post(x, o, u, gb, st1, st2, mods, w, tm, seq):
    b, t, d = x.shape
    row = lambda cols: pl.BlockSpec((1, tm, cols), lambda i, j: (i, j, 0))

    def bcast(a):
        if a.shape[1] == t:
            return pl.BlockSpec((1, tm, a.shape[2]), lambda i, j: (i, j, 0))
        return pl.BlockSpec((1, 1, a.shape[2]), lambda i, j: (i, 0, 0))

    prev = pl.BlockSpec((1, 8, CONV_DIM), lambda i, j: (i, jnp.maximum(j * (tm // 8) - 1, 0), 0))
    consts = [w["conv_w"], w["g_conv"], w["g_nsa"], w["wo_c"], w["wo_n"], w["ln1_g"], w["ln1_b"], w["ln2_g"],
              w["ln2_b"], w["w_gate"], w["w_up"], w["w_down"]]
    return pl.pallas_call(
        functools.partial(_post_kernel, seq=seq),
        grid=(b, t // tm),
        in_specs=[row(d), row(QZ_DIM), row(CONV_DIM), prev, row(CONV_DIM), bcast(st1), bcast(st2)]
        + [bcast(m) for m in mods] + [_const_spec(c.shape) for c in consts],
        out_specs=row(d),
        out_shape=jax.ShapeDtypeStruct((b, t, d), F32),
        compiler_params=_cparams(2),
        name="post_block",
    )(x, o, u, u, gb, st1, st2, *mods, *consts)


def _prep_w_in(w_in):
    d = w_in.shape[0]
    c3 = 3 * CONV_DIM
    splits = (CONV_DIM, 2 * CONV_DIM, c3, c3 + NSA_DIM, c3 + NSA_DIM + KV_COLS, c3 + NSA_DIM + 2 * KV_COLS,
              c3 + NSA_DIM + 3 * KV_COLS)
    hc, gb, gc, q, kvc, kvs, kvw, gl = jnp.split(w_in, splits, axis=1)
    q4 = q.reshape(d, N_KV, Q_PER_KV, HEAD_DIM) * SCALE
    qz = jnp.einsum("dgrh,gk->dgrkh", q4, jnp.eye(N_KV, dtype=w_in.dtype)).reshape(d, QZ_DIM)
    glp = jnp.pad(gl, ((0, 0), (0, LANES - gl.shape[1])))
    return jnp.concatenate([hc, gb, gc, qz, kvc, kvs, kvw, glp], axis=1).astype(BF16)


def _prep_cmp(cmp_pe, cmp_w1, cmp_b1, cmp_w2):
    eye = jnp.eye(N_KV, dtype=cmp_w1.dtype)
    eye2 = jnp.eye(2, dtype=cmp_w1.dtype)
    w1r = cmp_w1.reshape(2, 2, CMP_STRIDE, HEAD_DIM, CMP_HID)
    wbig = jnp.einsum("etjdh,ef,gk->jegdtfkh", w1r, eye2, eye).reshape(CHUNK_COLS, TB_COLS).astype(BF16)
    w2big = jnp.einsum("ehd,ef,gk->eghfkd", cmp_w2, eye2, eye).reshape(TB_COLS // 2, KV_COLS).astype(BF16)
    w1p = cmp_w1.reshape(2, 2, CMP_STRIDE // 2, 2, HEAD_DIM, CMP_HID)
    wpair = jnp.einsum("etpjdh,gk->epjgdtkh", w1p, eye).reshape(
        2, CMP_STRIDE // 2, 2 * N_KV * HEAD_DIM, 2 * N_KV * CMP_HID).astype(BF16)
    w2pair = jnp.einsum("ehd,gk->eghkd", cmp_w2, eye).reshape(2, N_KV * CMP_HID, N_KV * HEAD_DIM).astype(BF16)
    pe2 = cmp_pe.reshape(2, CMP_BLOCK * HEAD_DIM)
    w1f = cmp_w1.reshape(2, CMP_BLOCK * HEAD_DIM, CMP_HID).astype(BF16)
    dense = (wbig, pe2, w1f, cmp_b1, w2big)
    paged = (wpair, wbig[:, TB_COLS // 2:], pe2, w1f, cmp_b1, w2pair)
    return dense, paged


def _prep_post(conv_w, g_conv, g_nsa, w_o, ln1_g, ln1_b, ln2_g, ln2_b, w_gate, w_up, w_down):
    d = w_o.shape[1]
    own = jnp.eye(N_KV, dtype=w_o.dtype)
    g_nsa_z = jnp.einsum("grh,gk->grkh", g_nsa.reshape(N_KV, Q_PER_KV, HEAD_DIM), own).reshape(1, QZ_DIM)
    wo_n = jnp.einsum("grhd,gk->grkhd", w_o[CONV_DIM:].reshape(N_KV, Q_PER_KV, HEAD_DIM, d), own)
    row = lambda a: a.reshape(1, -1)
    return dict(conv_w=conv_w, g_conv=row(g_conv), g_nsa=g_nsa_z, wo_c=w_o[:CONV_DIM].astype(BF16),
                wo_n=wo_n.reshape(QZ_DIM, d).astype(BF16), ln1_g=row(ln1_g), ln1_b=row(ln1_b),
                ln2_g=row(ln2_g), ln2_b=row(ln2_b), w_gate=w_gate.astype(BF16), w_up=w_up.astype(BF16),
                w_down=w_down.astype(BF16))


def _feature_major(cache):
    n, rows = cache.shape[:2]
    return jnp.transpose(cache, (0, 2, 3, 4, 1)).reshape(n, KV_COLS, rows)


def kernel(x_prompt, x_sample, cache_cmp_kv, cache_slc_kv, cache_win_kv, state_conv, page_table, c_prompt,
           c_sample, w_ada, b_ada, w_in, conv_w, cmp_pe, cmp_w1, cmp_b1, cmp_w2, g_conv_out, g_nsa_out, w_o,
           ln1_g, ln1_b, ln2_g, ln2_b, w_ffn_gate, w_ffn_up, w_ffn_down):
    assert w_ada.shape[0] == DEPTH
    bp, s, d = x_prompt.shape
    bs, t, _ = x_sample.shape
    kv_shape = (2, N_KV, HEAD_DIM)

    w_cat = _prep_w_in(w_in[0])
    cmp_dense_w, cmp_paged_w = _prep_cmp(cmp_pe[0], cmp_w1[0], cmp_b1[0], cmp_w2[0])
    post_w = _prep_post(conv_w[0], g_conv_out[0], g_nsa_out[0], w_o[0], ln1_g[0], ln1_b[0], ln2_g[0], ln2_b[0],
                        w_ffn_gate[0], w_ffn_up[0], w_ffn_down[0])

    mod = _modulation(jnp.concatenate([c_prompt, c_sample], axis=0), w_ada[0], b_ada[0])
    mods_p = [m[:, None, :] for m in jnp.split(mod[:bp], 6, axis=-1)]
    mods_s = [jnp.repeat(m, t, axis=0)[None] for m in jnp.split(mod[bp:], 6, axis=-1)]

    u, gb, qz, kvc, kvs, kvw, kvs_b, kvw_b, gates = _in_proj(x_prompt, mods_p[1], mods_p[0], w_cat, 512, BF16)
    kv_cmpr = _compress_dense(kvc, *cmp_dense_w)
    o = _attn_prompt(qz, kvs_b, kvw_b, kv_cmpr, gates)
    zero_state = jnp.zeros((bp, 1, CONV_DIM), F32)
    y_prompt = _post(x_prompt, o, u, gb, zero_state, zero_state, mods_p[2:], post_w, 512, s)
    w_keep = min(WINDOW, s)
    cmp_p = kvc.reshape(1, bp, s, *kv_shape)
    slc_p = kvs.reshape(1, bp, s, *kv_shape)
    win_p = kvw[:, s - w_keep:].reshape(1, bp, w_keep, *kv_shape)
    conv_p = u[:, s - (CONV_W - 1):][None]

    rows = bs * t
    us, gbs, qzs, kvcs, kvss, kvws, _, _, gates_s = _in_proj(
        x_sample.reshape(1, rows, d), mods_s[1], mods_s[0], w_cat, rows, F32)
    per_b = lambda a: a.reshape(bs, t, a.shape[-1])
    kvcs, kvss, kvws = per_b(kvcs), per_b(kvss), per_b(kvws)
    xnew = jnp.pad(kvcs, ((0, 0), (0, (-t) % CMP_STRIDE), (0, 0))).reshape(bs, 1, CHUNK_COLS)
    kv_cmpr_s = _compress_paged(_feature_major(cache_cmp_kv[0]), page_table, xnew, *cmp_paged_w)
    o_s = _attn_sample(per_b(qzs), kv_cmpr_s, per_b(gates_s), kvss, _feature_major(cache_slc_kv[0]),
                       _feature_major(cache_win_kv[0]), kvws, page_table)
    st = state_conv[0]
    tpos = jnp.arange(t)[None, :, None]
    st1 = jnp.where(tpos == 0, st[:, 1:2], 0.0).reshape(1, rows, CONV_DIM)
    st2 = jnp.where(tpos == 0, st[:, 0:1], jnp.where(tpos == 1, st[:, 1:2], 0.0)).reshape(1, rows, CONV_DIM)
    y_sample = _post(x_sample.reshape(1, rows, d), o_s.reshape(1, rows, QZ_DIM), us, gbs, st1, st2, mods_s[2:],
                     post_w, rows, t).reshape(bs, t, d)
    cmp_s = kvcs.reshape(1, bs, t, *kv_shape)
    slc_s = kvss.reshape(1, bs, t, *kv_shape)
    win_all = jnp.concatenate([cache_win_kv[0], kvws.reshape(bs, t, *kv_shape)], axis=1)
    win_s = win_all[:, t:][None]
    conv_s = jnp.concatenate([st, per_b(us)], axis=1)[:, t:][None]
    return (y_prompt, y_sample, cmp_p, slc_p, win_p, conv_p, cmp_s, slc_s, win_s, conv_s)
```

```python
import functools

import jax
import jax.numpy as jnp
from jax import lax
from jax.experimental import pallas as pl
from jax.experimental.pallas import tpu as pltpu

HEAD_DIM = 64
N_KV = 2
Q_PER_KV = 6
N_HEADS = N_KV * Q_PER_KV
CONV_DIM = 256
CONV_W = 3
NSA_DIM = N_HEADS * HEAD_DIM
KV_COLS = 2 * N_KV * HEAD_DIM
CMP_BLOCK = 32
CMP_STRIDE = 16
CMP_HID = 2 * HEAD_DIM
SEL_BLOCK = 64
CMP_PER_SEL = SEL_BLOCK // CMP_STRIDE
N_SELECT = 16
WINDOW = 512
Q_BLOCK = 128
DEPTH = 1
ALPHA = (2 * DEPTH) ** 0.25
EPS = 1e-5
SCALE = HEAD_DIM ** -0.5

LANES = 128
QZ_DIM = N_HEADS * LANES
CHUNK_COLS = CMP_STRIDE * KV_COLS
NEG = -1e30
VMEM_LIMIT = 56 * 1024 * 1024

F32 = jnp.float32
BF16 = jnp.bfloat16


def _cparams(n_grid):
    return pltpu.CompilerParams(dimension_semantics=("arbitrary",) * n_grid, vmem_limit_bytes=VMEM_LIMIT)


def _const_spec(shape):
    nd = len(shape)
    return pl.BlockSpec(shape, lambda *_: (0,) * nd, pipeline_mode=pl.Buffered(1))


def _dot(a, b):
    return jnp.dot(a, b, preferred_element_type=F32)


def _dot_nt(a, b):
    return lax.dot_general(a, b, (((1,), (1,)), ((), ())), preferred_element_type=F32)


def _iota(shape, dim):
    return lax.broadcasted_iota(jnp.int32, shape, dim)


def _lane_tiles(x):
    return [x[:, w * LANES:(w + 1) * LANES] for w in range(x.shape[1] // LANES)]


def _mod_kernel(c_ref, w_ref, b_ref, o_ref):
    c = c_ref[...]
    a = (c * jax.nn.sigmoid(c)).astype(BF16)
    o_ref[...] = _dot(a, w_ref[...].astype(BF16)) + b_ref[...]


def _modulation(c_all, w_ada, b_ada):
    m, d = c_all.shape
    n = w_ada.shape[1]
    tn = 512
    return pl.pallas_call(
        _mod_kernel,
        grid=(n // tn,),
        in_specs=[pl.BlockSpec((m, d), lambda j: (0, 0)),
                  pl.BlockSpec((d, tn), lambda j: (0, j)),
                  pl.BlockSpec((1, tn), lambda j: (0, j))],
        out_specs=pl.BlockSpec((m, tn), lambda j: (0, j)),
        out_shape=jax.ShapeDtypeStruct((m, n), F32),
        compiler_params=_cparams(1),
        name="modulation",
    )(c_all, w_ada, b_ada.reshape(1, n))


_C_HC, _C_GB, _C_GC, _C_Q = 0, CONV_DIM, 2 * CONV_DIM, 3 * CONV_DIM
_C_KVC = _C_Q + QZ_DIM
_C_KVS = _C_KVC + KV_COLS
_C_KVW = _C_KVS + KV_COLS
_C_GL = _C_KVW + KV_COLS
W_IN_COLS = _C_GL + LANES


def _inproj_kernel(x_ref, sc_ref, sh_ref, w_ref, u_ref, gb_ref, qz_ref, kvc_ref, kvs_ref, kvw_ref,
                   kvsb_ref, kvwb_ref, gates_ref):
    h = (x_ref[0] * (1.0 + sc_ref[0]) + sh_ref[0]).astype(BF16)
    hc = _dot(h, w_ref[:, _C_HC:_C_GB])
    gb_ref[0] = _dot(h, w_ref[:, _C_GB:_C_GC])
    gc = _dot(h, w_ref[:, _C_GC:_C_Q])
    u_ref[0] = gc * hc
    qz_ref[0] = _dot(h, w_ref[:, _C_Q:_C_KVC]).astype(qz_ref.dtype)
    kvc_ref[0] = _dot(h, w_ref[:, _C_KVC:_C_KVS])
    kvs = _dot(h, w_ref[:, _C_KVS:_C_KVW])
    kvs_ref[0] = kvs
    kvsb_ref[0] = kvs.astype(BF16)
    kvw = _dot(h, w_ref[:, _C_KVW:_C_GL])
    kvw_ref[0] = kvw
    kvwb_ref[0] = kvw.astype(BF16)
    gates_ref[0] = jax.nn.sigmoid(_dot(h, w_ref[:, _C_GL:W_IN_COLS]))


def _in_proj(x, scale, shift, w_cat, tm, qz_dtype):
    b, t, d = x.shape
    rm = scale.shape[1]
    mod_block = (1, tm, d) if rm == t else (1, 1, d)
    mod_map = (lambda i, j: (i, j, 0)) if rm == t else (lambda i, j: (i, 0, 0))
    row = lambda cols: pl.BlockSpec((1, tm, cols), lambda i, j: (i, j, 0))
    sds = lambda cols, dt: jax.ShapeDtypeStruct((b, t, cols), dt)
    return pl.pallas_call(
        _inproj_kernel,
        grid=(b, t // tm),
        in_specs=[row(d), pl.BlockSpec(mod_block, mod_map), pl.BlockSpec(mod_block, mod_map),
                  _const_spec((d, W_IN_COLS))],
        out_specs=[row(CONV_DIM), row(CONV_DIM), row(QZ_DIM), row(KV_COLS), row(KV_COLS), row(KV_COLS),
                   row(KV_COLS), row(KV_COLS), row(LANES)],
        out_shape=[sds(CONV_DIM, F32), sds(CONV_DIM, F32), sds(QZ_DIM, qz_dtype), sds(KV_COLS, F32),
                   sds(KV_COLS, F32), sds(KV_COLS, F32), sds(KV_COLS, BF16), sds(KV_COLS, BF16),
                   sds(LANES, F32)],
        compiler_params=_cparams(2),
        name="in_proj",
    )(x, scale, shift, w_cat)


def _gelu_tanh(x):
    return jax.nn.gelu(x, approximate=True)


def _cmp_bias_e(pe_ref, w1f_ref, b1_ref, e):
    pe = jnp.broadcast_to(pe_ref[e:e + 1, :], (8, pe_ref.shape[1])).astype(BF16)
    return _dot(pe, w1f_ref[e])[0:1, :] + b1_ref[e:e + 1, :]


def _chunk_partials(row_pairs, wpair_ref, e):
    acc = None
    for jp in range(CMP_STRIDE // 2):
        part = _dot(row_pairs(jp), wpair_ref[e, jp])
        acc = part if acc is None else acc + part
    return acc


def _compress_mlp(top, bot_next, pe_ref, w1f_ref, b1_ref, w2_ref, e):
    bias = jnp.concatenate([_cmp_bias_e(pe_ref, w1f_ref, b1_ref, e)] * N_KV, axis=1)
    return _dot(_gelu_tanh(top + bot_next + bias).astype(BF16), w2_ref[e])


def _compress_dense_kernel(x_ref, wpair_ref, pe_ref, w1f_ref, b1_ref, w2_ref, o_ref, r_sc):
    n = o_ref.shape[1]
    half_cols = N_KV * HEAD_DIM
    hid_cols = N_KV * CMP_HID
    outs = []
    for e in range(2):
        r_sc[e] = x_ref[0, :, e * half_cols:(e + 1) * half_cols]
        acc = _chunk_partials(
            lambda jp: jnp.concatenate([r_sc[e, pl.ds(2 * jp + jj, n, stride=CMP_STRIDE), :] for jj in (0, 1)],
                                       axis=1).astype(BF16), wpair_ref, e)
        bot_next = pltpu.roll(acc[:, hid_cols:], n - 1, 0)
        outs.append(_compress_mlp(acc[:, :hid_cols], bot_next, pe_ref, w1f_ref, b1_ref, w2_ref, e))
    o_ref[0] = jnp.concatenate(outs, axis=1).astype(o_ref.dtype)


def _compress_dense(kvc, wpair, pe2, w1f, b1, w2pair):
    b, s, _ = kvc.shape
    nch = s // CMP_STRIDE
    return pl.pallas_call(
        _compress_dense_kernel,
        grid=(b,),
        in_specs=[pl.BlockSpec((1, s, KV_COLS), lambda i: (i, 0, 0)),
                  _const_spec(wpair.shape), _const_spec(pe2.shape), _const_spec(w1f.shape),
                  _const_spec(b1.shape), _const_spec(w2pair.shape)],
        out_specs=pl.BlockSpec((1, nch, KV_COLS), lambda i: (i, 0, 0)),
        out_shape=jax.ShapeDtypeStruct((b, nch, KV_COLS), BF16),
        scratch_shapes=[pltpu.VMEM((2, s, LANES), F32)],
        compiler_params=_cparams(1),
        name="compress_prompt",
    )(kvc, wpair, pe2, w1f, b1, w2pair)


PAGES_PER_STEP = 32
CHUNK_PITCH = 24


def _page_copy(pool_ref, page_index, buf, sem, slot, p):
    return pltpu.make_async_copy(pool_ref.at[page_index], buf.at[slot, p], sem.at[slot, p])


def _await_pages(pt_ref, pool_ref, buf, sem, n_steps):
    b, kt = pl.program_id(0), pl.program_id(1)

    @pl.when((b == 0) & (kt == 0))
    def _():
        for k in range(n_steps):
            for p in range(PAGES_PER_STEP):
                _page_copy(pool_ref, pt_ref[0, k * PAGES_PER_STEP + p], buf, sem, k, p).start(priority=p % 2)

    for p in range(PAGES_PER_STEP):
        _page_copy(pool_ref, 0, buf, sem, kt, p).wait()
    return kt


def _refill_pages(pt_ref, pool_ref, buf, sem):
    kt = pl.program_id(1)
    nb = jnp.minimum(pl.program_id(0) + 1, pl.num_programs(0) - 1)
    for p in range(PAGES_PER_STEP):
        _page_copy(pool_ref, pt_ref[nb, kt * PAGES_PER_STEP + p], buf, sem, kt, p).start(priority=p % 2)


def _drain_pages(pool_ref, buf, sem, n_steps):
    @pl.when((pl.program_id(0) == pl.num_programs(0) - 1) & (pl.program_id(1) == n_steps - 1))
    def _():
        for k in range(n_steps):
            for p in range(PAGES_PER_STEP):
                _page_copy(pool_ref, 0, buf, sem, k, p).wait()


def _compress_paged_kernel(pt_ref, pool_ref, *refs, n_steps):
    g_pages = PAGES_PER_STEP
    (xnew_ref, wpair_ref, pe_ref, w1f_ref, b1_ref, w2_ref, o_ref, r_sc, tb_sc, buf, sem) = refs
    kt = pl.program_id(1)
    page = buf.shape[3]
    slot = _await_pages(pt_ref, pool_ref, buf, sem, n_steps)
    cpp = page // CMP_STRIDE
    rows_step = g_pages * cpp
    half_cols = N_KV * HEAD_DIM
    hid_cols = N_KV * CMP_HID

    row0 = pl.multiple_of(kt * rows_step, rows_step)
    for e in range(2):
        for p in range(g_pages):
            rows_t = buf[slot, p, e * half_cols:(e + 1) * half_cols, :].T
            for c in range(cpp):
                r_sc[e, pl.ds((p * cpp + c) * CHUNK_PITCH, CMP_STRIDE), :] = rows_t[c * CMP_STRIDE:(c + 1) * CMP_STRIDE]
        tb_sc[e, pl.ds(row0, rows_step), :] = _chunk_partials(
            lambda jp: jnp.concatenate([r_sc[e, pl.ds(2 * jp + jj, rows_step, stride=CHUNK_PITCH), :]
                                        for jj in (0, 1)], axis=1).astype(BF16), wpair_ref, e)
    _refill_pages(pt_ref, pool_ref, buf, sem)

    @pl.when(kt == n_steps - 1)
    def _():
        n = tb_sc.shape[1]
        xn = jnp.broadcast_to(xnew_ref[0], (8, CHUNK_COLS)).astype(BF16)
        last = _iota((n, 1), 0) == n - 1
        outs = []
        for e in range(2):
            new_pair = lambda jp: jnp.concatenate(
                [xn[:, (2 * jp + jj) * KV_COLS + e * half_cols:(2 * jp + jj) * KV_COLS + (e + 1) * half_cols]
                 for jj in (0, 1)], axis=1)
            bot_new = _chunk_partials(new_pair, wpair_ref, e)[0:1, hid_cols:]
            top, bot = tb_sc[e, :, :hid_cols], tb_sc[e, :, hid_cols:]
            bot_next = jnp.where(last, bot_new, pltpu.roll(bot, n - 1, 0))
            outs.append(_compress_mlp(top, bot_next, pe_ref, w1f_ref, b1_ref, w2_ref, e))
        o_ref[0] = jnp.concatenate(outs, axis=1).astype(o_ref.dtype)

    _drain_pages(pool_ref, buf, sem, n_steps)


def _page_stream_scratch(page, n_steps):
    return [pltpu.VMEM((n_steps, PAGES_PER_STEP, KV_COLS, page), F32),
            pltpu.SemaphoreType.DMA((n_steps, PAGES_PER_STEP))]


def _compress_paged(pool_t, page_table, xnew, wpair, pe2, w1f, b1, w2pair):
    bn, n_pages = page_table.shape
    page = pool_t.shape[2]
    g = PAGES_PER_STEP
    n_steps = n_pages // g
    assert n_pages % g == 0 and page % CMP_STRIDE == 0 and page == LANES
    n = n_pages * page // CMP_STRIDE

    def const(shape):
        nd = len(shape)
        return pl.BlockSpec(shape, lambda i, j, pt: (0,) * nd, pipeline_mode=pl.Buffered(1))

    grid_spec = pltpu.PrefetchScalarGridSpec(
        num_scalar_prefetch=1,
        grid=(bn, n_steps),
        in_specs=[pl.BlockSpec(memory_space=pl.ANY),
                  pl.BlockSpec((1, 1, CHUNK_COLS), lambda i, j, pt: (i, 0, 0)), const(wpair.shape),
                  const(pe2.shape), const(w1f.shape), const(b1.shape), const(w2pair.shape)],
        out_specs=pl.BlockSpec((1, n, KV_COLS), lambda i, j, pt: (i, 0, 0)),
        scratch_shapes=[pltpu.VMEM((2, g * (page // CMP_STRIDE) * CHUNK_PITCH, LANES), F32),
                        pltpu.VMEM((2, n, 2 * N_KV * CMP_HID), F32)] + _page_stream_scratch(page, n_steps),
    )
    return pl.pallas_call(
        functools.partial(_compress_paged_kernel, n_steps=n_steps),
        grid_spec=grid_spec,
        out_shape=jax.ShapeDtypeStruct((bn, n, KV_COLS), BF16),
        compiler_params=_cparams(2),
        name="compress_sample",
    )(page_table, pool_t, xnew, wpair, pe2, w1f, b1, w2pair)


def _split3_dot(x, m01):
    h1 = x.astype(BF16)
    r1 = x - h1.astype(F32)
    h2 = r1.astype(BF16)
    h3 = (r1 - h2.astype(F32)).astype(BF16)
    return _dot(h1, m01) + _dot(h2, m01) + _dot(h3, m01)


def _block_scores(imp, n_lanes):
    ncp = imp.shape[1]
    c = _iota((ncp, n_lanes), 0)
    j = _iota((ncp, n_lanes), 1)
    a = jnp.where((c >= CMP_PER_SEL * j - 1) & (c <= CMP_PER_SEL * j + CMP_PER_SEL - 1), 1.0, 0.0).astype(BF16)
    return _split3_dot(imp, a)


def _ranked_scores(score, qpos, n_blocks):
    j = _iota(score.shape, 1)
    cur = qpos >> 6
    valid = j * SEL_BLOCK <= qpos
    forced = (j == 0) | (j == cur) | (j == cur - 1)
    return jnp.where(forced, -NEG, jnp.where(valid, score, NEG))


def _select_blocks_iter(score, qpos, n_blocks, n_select):
    r, l = score.shape
    removed = 3.0 * NEG
    jf = _iota((r, l), 1).astype(F32)
    sc = jnp.where(jf < n_blocks, _ranked_scores(score, qpos, n_blocks), removed)

    sel = jnp.zeros((r, l), F32)
    for _ in range(n_select):
        mx = jnp.max(sc, axis=-1, keepdims=True)
        first = jnp.min(jnp.where(sc == mx, jf, 1e9), axis=-1, keepdims=True)
        pick = jf == first
        sc, sel = jnp.where(pick, removed, sc), jnp.where(pick, 1.0, sel)
    return sel


def _select_blocks_rank(score, qpos, n_blocks, n_select):
    r, l = score.shape
    nb8 = -(-n_blocks // 8) * 8
    st = _ranked_scores(score, qpos, n_blocks).T[:nb8]
    jrow = _iota((nb8, r), 0)
    rank = jnp.zeros((nb8, r), F32)
    for jp in range(n_blocks):
        row = st[jp:jp + 1, :]
        beats = (row > st) | ((row == st) & (jrow > jp))
        rank = rank + jnp.where(beats, 1.0, 0.0)
    sel_t = jnp.where((rank < n_select) & (jrow < n_blocks), 1.0, 0.0)
    if nb8 < l:
        sel_t = jnp.concatenate([sel_t, jnp.zeros((l - nb8, r), F32)], axis=0)
    return sel_t.T


def _softmax_parts(s, mask):
    s = jnp.where(mask, s, NEG)
    m = jnp.max(s, axis=-1, keepdims=True)
    e = jnp.where(mask, jnp.exp(s - m), 0.0)
    return e, jnp.sum(e, axis=-1, keepdims=True)


SEL_TILE = 512


def _attn_prompt_kernel(qz_ref, kvs_ref, kvw_ref, kvc_ref, gates_ref, exp_ref, o_ref,
                        s_sc, p_sc, pw_sc, m_sc, acc_sc, ocw_sc, *, seq):
    qb = Q_BLOCK
    tk = SEL_TILE
    start = pl.program_id(1) * qb
    qpos = start + _iota((qb, 1), 0)
    ncp = kvc_ref.shape[1]
    n_blocks = -(-seq // SEL_BLOCK)
    rows = Q_PER_KV * qb
    gates = gates_ref[0]
    n_tiles = (start + qb + tk - 1) // tk
    head_rows = [slice(r * qb, (r + 1) * qb) for r in range(Q_PER_KV)]
    kv_heads = range(N_KV)

    def q_rows(g):
        return jnp.concatenate(
            [qz_ref[0, :, (g * Q_PER_KV + r) * LANES:(g * Q_PER_KV + r + 1) * LANES] for r in range(Q_PER_KV)],
            axis=0)


    def values_and_ones(g, vv):
        return jnp.where((_iota((1, LANES), 1) >> 6) == g, vv, jnp.ones_like(vv))

    def normalised(acc):
        return acc / jnp.maximum(pltpu.roll(acc, LANES // 2, 1), 1e-30)

    o_cmp, sel = [], []
    c = _iota((qb, ncp), 1)
    cmask = (CMP_STRIDE * c + (CMP_BLOCK - 1) <= qpos) & (c < ncp - 1)
    for g in kv_heads:
        s = _dot_nt(q_rows(g), kvc_ref[0, :, 0:LANES]).reshape(Q_PER_KV, qb, ncp)
        e, l = _softmax_parts(s, cmask)
        p = e / jnp.maximum(l, 1e-30)
        o_cmp.append(_dot(p.reshape(rows, ncp).astype(BF16), kvc_ref[0, :, LANES:2 * LANES]))
        imp = jnp.sum(p, axis=0)
        sel.append(_select_blocks_rank(_block_scores(imp, LANES), qpos, n_blocks,
                                       min(N_SELECT, n_blocks)).astype(BF16))

    wlen = WINDOW + qb
    w0 = pl.multiple_of(jnp.maximum(start - WINDOW, 0), qb)
    kpos = w0 + _iota((1, wlen), 1)
    wmask = (kpos <= qpos) & (kpos > qpos - WINDOW)
    for g in kv_heads:
        s = _dot_nt(q_rows(g), kvw_ref[0, pl.ds(w0, wlen), 0:LANES])
        for rs in head_rows:
            sr = jnp.where(wmask, s[rs], NEG)
            m = jnp.max(functools.reduce(jnp.maximum, _lane_tiles(sr)), axis=-1, keepdims=True)
            pw_sc[g, rs, :] = jnp.exp((sr - m).astype(BF16))
        o_win = normalised(_dot(pw_sc[g], values_and_ones(g, kvw_ref[0, pl.ds(w0, wlen), LANES:2 * LANES])))
        for r, rs in enumerate(head_rows):
            h = g * Q_PER_KV + r
            ocw_sc[g, rs, :] = (gates[:, h:h + 1] * o_cmp[g][rs]
                                + gates[:, 2 * N_HEADS + h:2 * N_HEADS + h + 1] * o_win[rs])

    m_sc[...] = jnp.full(m_sc.shape, NEG, F32)

    def pass1(kt, _):
        k0 = pl.multiple_of(kt * tk, tk)
        causal = (k0 + _iota((1, tk), 1)) <= qpos
        for g in kv_heads:
            s = _dot_nt(q_rows(g), kvs_ref[0, pl.ds(k0, tk), 0:LANES])
            mask = (_dot(sel[g], exp_ref[kt]) > 0.5) & causal
            for rs in head_rows:
                sr = jnp.where(mask, s[rs], NEG)
                s_sc[g, kt, rs, :] = sr
                m_sc[g, rs, :] = jnp.maximum(m_sc[g, rs, :], functools.reduce(jnp.maximum, _lane_tiles(sr)))
        return 0

    lax.fori_loop(0, n_tiles, pass1, 0)
    m_sc[...] = jnp.broadcast_to(jnp.max(m_sc[...], axis=-1, keepdims=True), m_sc.shape)
    acc_sc[...] = jnp.zeros(acc_sc.shape, F32)

    def pass2(kt, _):
        k0 = pl.multiple_of(kt * tk, tk)
        for g in kv_heads:
            for rs in head_rows:
                x = s_sc[g, kt, rs, :] - jnp.concatenate([m_sc[g, rs, :]] * (tk // LANES), axis=1)
                p_sc[g, rs, :] = jnp.exp(x.astype(BF16))
            vv = values_and_ones(g, kvs_ref[0, pl.ds(k0, tk), LANES:2 * LANES])
            acc_sc[g] = acc_sc[g] + _dot(p_sc[g], vv)
        return 0

    lax.fori_loop(0, n_tiles, pass2, 0)

    for g in kv_heads:
        o_slc = normalised(acc_sc[g])
        own = (_iota((qb, LANES), 1) >> 6) == g
        for r, rs in enumerate(head_rows):
            h = g * Q_PER_KV + r
            o = ocw_sc[g, rs, :] + gates[:, N_HEADS + h:N_HEADS + h + 1] * o_slc[rs]
            o_ref[0, :, h * LANES:(h + 1) * LANES] = jnp.where(own, o, 0.0)


def _expand_matrix(n_tiles, tile):
    t = jnp.arange(n_tiles)[:, None, None]
    j = jnp.arange(LANES)[None, :, None]
    k = jnp.arange(tile)[None, None, :]
    return (j == (t * tile + k) // SEL_BLOCK).astype(BF16)


def _attn_prompt(qz, kvs_b, kvw_b, kv_cmpr, gates):
    b, s, _ = qz.shape
    assert s % Q_BLOCK == 0 and s % SEL_TILE == 0 and s >= WINDOW + Q_BLOCK and s // SEL_BLOCK <= LANES
    ncp = kv_cmpr.shape[1]
    n_tiles = s // SEL_TILE
    expand = _expand_matrix(n_tiles, SEL_TILE)
    rows = Q_PER_KV * Q_BLOCK
    full = lambda n: pl.BlockSpec((1, n, KV_COLS), lambda i, j: (i, 0, 0))
    return pl.pallas_call(
        functools.partial(_attn_prompt_kernel, seq=s),
        grid=(b, s // Q_BLOCK),
        in_specs=[pl.BlockSpec((1, Q_BLOCK, QZ_DIM), lambda i, j: (i, j, 0)), full(s), full(s), full(ncp),
                  pl.BlockSpec((1, Q_BLOCK, LANES), lambda i, j: (i, j, 0)), _const_spec(expand.shape)],
        out_specs=pl.BlockSpec((1, Q_BLOCK, QZ_DIM), lambda i, j: (i, j, 0)),
        out_shape=jax.ShapeDtypeStruct((b, s, QZ_DIM), F32),
        scratch_shapes=[pltpu.VMEM((N_KV, n_tiles, rows, SEL_TILE), F32), pltpu.VMEM((N_KV, rows, SEL_TILE), BF16),
                        pltpu.VMEM((N_KV, rows, WINDOW + Q_BLOCK), BF16), pltpu.VMEM((N_KV, rows, LANES), F32),
                        pltpu.VMEM((N_KV, rows, LANES), F32), pltpu.VMEM((N_KV, rows, LANES), F32)],
        compiler_params=_cparams(2),
        name="attn_prompt",
    )(qz, kvs_b, kvw_b, kv_cmpr, gates, expand)


def _attn_sample_kernel(pt_ref, pool_ref, *refs, past, n_steps):
    g_pages = PAGES_PER_STEP
    (qz_ref, kvc_ref, gates_ref, kvs_new_ref, win_ref, kvw_new_ref, exp_ref,
     o_ref, q_sc, sel_sc, s_sc, v_sc, snew_sc, mp_sc, ocw_sc, buf, sem) = refs
    kt = pl.program_id(1)
    t = qz_ref.shape[1]
    rows = N_HEADS * t
    page = buf.shape[3]
    tile = g_pages * page
    blocks_per_tile = tile // SEL_BLOCK
    n_past_blocks = past // SEL_BLOCK
    n_blocks = n_past_blocks + -(-t // SEL_BLOCK)
    sel_lanes = sel_sc.shape[0] * LANES
    half_cols = N_KV * HEAD_DIM
    qpos = past + _iota((t, 1), 0)
    qpos_gt = jnp.concatenate([qpos] * N_KV, axis=0)
    own_rows = _iota((N_KV, Q_PER_KV, t, LANES), 0) == (_iota((N_KV, Q_PER_KV, t, LANES), 3) >> 6)

    def new_rows_padded():
        kvn = kvs_new_ref[0]
        return jnp.concatenate([kvn, jnp.zeros((LANES - t, KV_COLS), F32)], axis=0).astype(BF16)

    def masked_scores(s, mask):
        k = s.shape[1]
        return jnp.where(mask.reshape(N_KV, 1, t, k), s.reshape(N_KV, Q_PER_KV, t, k), NEG).reshape(rows, k)

    @pl.when(kt == 0)
    def _():
        q = jnp.concatenate([qz_ref[0, :, h * LANES:(h + 1) * LANES] for h in range(N_HEADS)], axis=0)
        q_sc[...] = q.astype(BF16)
        qb = q_sc[...]
        ncp = kvc_ref.shape[1]
        s = _dot_nt(qb, kvc_ref[0, :, 0:LANES]).reshape(N_HEADS, t, ncp)
        c = _iota((t, ncp), 1)
        cmask = CMP_STRIDE * c + (CMP_BLOCK - 1) <= qpos
        e, l = _softmax_parts(s, cmask)
        p = e / jnp.maximum(l, 1e-30)
        o_cmp = _dot(p.reshape(rows, ncp).astype(BF16), kvc_ref[0, :, LANES:2 * LANES])
        imp = jnp.sum(p.reshape(N_KV, Q_PER_KV, t, ncp), axis=1).reshape(N_KV * t, ncp)
        sel = _select_blocks_iter(_block_scores(imp, sel_lanes), qpos_gt, n_blocks, min(N_SELECT, n_blocks))
        for w in range(sel_sc.shape[0]):
            sel_sc[w] = sel[:, w * LANES:(w + 1) * LANES]
        wb = win_ref.shape[2]
        win = win_ref[0].astype(BF16)
        kvw_new = kvw_new_ref[0].astype(BF16)
        s_old = _dot(qb, win[0:half_cols]).reshape(N_HEADS, t, wb)
        s_new = _dot_nt(qb, kvw_new[:, 0:half_cols]).reshape(N_HEADS, t, t)
        kpos_old = past - wb + _iota((1, wb), 1)
        kpos_new = past + _iota((1, t), 1)
        mask_old = (kpos_old <= qpos) & (kpos_old > qpos - WINDOW) & (kpos_old >= 0)
        mask_new = (kpos_new <= qpos) & (kpos_new > qpos - WINDOW)
        s_old = jnp.where(mask_old, s_old, NEG)
        s_new = jnp.where(mask_new, s_new, NEG)
        m = jnp.maximum(jnp.max(s_old, axis=-1, keepdims=True), jnp.max(s_new, axis=-1, keepdims=True))
        e_old = jnp.where(mask_old, jnp.exp(s_old - m), 0.0)
        e_new = jnp.where(mask_new, jnp.exp(s_new - m), 0.0)
        l = jnp.sum(e_old, axis=-1, keepdims=True) + jnp.sum(e_new, axis=-1, keepdims=True)
        o_win = (_dot_nt(e_old.reshape(rows, wb).astype(BF16), win[half_cols:])
                 + _dot(e_new.reshape(rows, t).astype(BF16), kvw_new[:, half_cols:]))
        o_win = o_win.reshape(N_HEADS, t, LANES) / jnp.maximum(l, 1e-30)
        gts = gates_ref[0]
        gc = jnp.stack([jnp.broadcast_to(gts[:, h:h + 1], (t, LANES)) for h in range(N_HEADS)])
        gw = jnp.stack([jnp.broadcast_to(gts[:, 2 * N_HEADS + h:2 * N_HEADS + h + 1], (t, LANES))
                        for h in range(N_HEADS)])
        ocw_sc[...] = gc * o_cmp.reshape(N_HEADS, t, LANES) + gw * o_win
        jn = (past + _iota((1, LANES), 1)) >> 6
        seln = jnp.zeros((N_KV * t, LANES), F32)
        for jb in range(n_past_blocks, n_blocks):
            seln = jnp.where(jn == jb, sel[:, jb:jb + 1], seln)
        npos = past + _iota((1, LANES), 1)
        nmask = (seln > 0.5) & (npos <= qpos_gt) & (npos < past + t)
        s_n = masked_scores(_dot_nt(qb, new_rows_padded()[:, 0:half_cols]), nmask)
        snew_sc[...] = s_n
        mp_sc[...] = s_n

    slot = _await_pages(pt_ref, pool_ref, buf, sem, n_steps)
    k_t = jnp.concatenate([buf[slot, p, 0:half_cols, :] for p in range(g_pages)], axis=1).astype(BF16)
    v_sc[kt] = jnp.concatenate([buf[slot, p, half_cols:, :] for p in range(g_pages)], axis=1).astype(BF16)
    b0 = kt * blocks_per_tile
    selw = sel_sc[b0 // LANES].astype(BF16)
    kpos = kt * tile + _iota((1, tile), 1)
    mask = (_dot(selw, exp_ref[0]) > 0.5) & (kpos <= qpos_gt)
    s = masked_scores(_dot(q_sc[...], k_t), mask)
    s_sc[kt] = s
    mp_sc[...] = jnp.maximum(mp_sc[...], functools.reduce(jnp.maximum, _lane_tiles(s)))
    _refill_pages(pt_ref, pool_ref, buf, sem)

    @pl.when(kt == n_steps - 1)
    def _():
        m = jnp.max(mp_sc[...], axis=-1, keepdims=True)
        e = jnp.exp(snew_sc[...] - m)
        lp = e
        acc = _dot(e.astype(BF16), new_rows_padded()[:, half_cols:])
        for k2 in range(n_steps):
            e = jnp.exp(s_sc[k2] - m)
            lp = lp + functools.reduce(jnp.add, _lane_tiles(e))
            acc = acc + _dot_nt(e.astype(BF16), v_sc[k2])
        l = jnp.sum(lp, axis=-1, keepdims=True)
        gts = gates_ref[0]
        gs = jnp.stack([jnp.broadcast_to(gts[:, N_HEADS + h:N_HEADS + h + 1], (t, LANES))
                        for h in range(N_HEADS)])
        o_slc = (acc / jnp.maximum(l, 1e-30)).reshape(N_HEADS, t, LANES)
        o = jnp.where(own_rows.reshape(N_HEADS, t, LANES), ocw_sc[...] + gs * o_slc, 0.0)
        for h in range(N_HEADS):
            o_ref[0, :, h * LANES:(h + 1) * LANES] = o[h]

    _drain_pages(pool_ref, buf, sem, n_steps)


def _attn_sample(qz, kv_cmpr, gates, kvs_new, slc_pool_t, win_t, kvw_new, page_table):
    bn, t, _ = qz.shape
    n_pages = page_table.shape[1]
    page = slc_pool_t.shape[2]
    past = n_pages * page
    g = PAGES_PER_STEP
    n_steps = n_pages // g
    tile = g * page
    blocks_per_tile = tile // SEL_BLOCK
    assert n_pages % g == 0 and page % SEL_BLOCK == 0 and LANES % blocks_per_tile == 0 and t % 8 == 0
    n_blocks = past // SEL_BLOCK + -(-t // SEL_BLOCK)
    sel_groups = -(-n_blocks // LANES)
    ncp = kv_cmpr.shape[1]
    wb = win_t.shape[2]
    n_var = LANES // blocks_per_tile
    expand = _expand_matrix(n_var, tile)

    per_b = lambda n, cols: pl.BlockSpec((1, n, cols), lambda i, j, pt: (i, 0, 0))
    grid_spec = pltpu.PrefetchScalarGridSpec(
        num_scalar_prefetch=1,
        grid=(bn, n_steps),
        in_specs=[pl.BlockSpec(memory_space=pl.ANY),
                  per_b(t, QZ_DIM), per_b(ncp, KV_COLS), per_b(t, LANES), per_b(t, KV_COLS), per_b(KV_COLS, wb),
                  per_b(t, KV_COLS), pl.BlockSpec((1, LANES, tile), lambda i, j, pt: (j % n_var, 0, 0))],
        out_specs=per_b(t, QZ_DIM),
        scratch_shapes=[pltpu.VMEM((N_HEADS * t, LANES), BF16),
                        pltpu.VMEM((sel_groups, N_KV * t, LANES), F32),
                        pltpu.VMEM((n_steps, N_HEADS * t, tile), F32), pltpu.VMEM((n_steps, LANES, tile), BF16),
                        pltpu.VMEM((N_HEADS * t, LANES), F32), pltpu.VMEM((N_HEADS * t, LANES), F32),
                        pltpu.VMEM((N_HEADS, t, LANES), F32)] + _page_stream_scratch(page, n_steps),
    )
    return pl.pallas_call(
        functools.partial(_attn_sample_kernel, past=past, n_steps=n_steps),
        grid_spec=grid_spec,
        out_shape=jax.ShapeDtypeStruct((bn, t, QZ_DIM), F32),
        compiler_params=_cparams(2),
        name="attn_sample",
    )(page_table, slc_pool_t, qz, kv_cmpr, gates, kvs_new, win_t, kvw_new, expand)


def _layernorm(x, g, b):
    mu = jnp.mean(x, axis=-1, keepdims=True)
    xc = x - mu
    var = jnp.mean(xc * xc, axis=-1, keepdims=True)
    return xc * lax.rsqrt(var + EPS) * g + b


def _post_kernel(x_ref, o_ref, u_ref, uprev_ref, gb_ref, st1_ref, st2_ref, g1_ref, sh2_ref, sc2_ref, g2_ref,
                 convw_ref, gconv_ref, gnsa_ref, woc_ref, won_ref, ln1g_ref, ln1b_ref, ln2g_ref, ln2b_ref,
                 wg_ref, wu_ref, wd_ref, y_ref, *, seq):
    tm = x_ref.shape[1]
    u = u_ref[0]
    ext = jnp.concatenate([uprev_ref[0], u], axis=0)
    if tm <= seq:
        pos = (pl.program_id(1) * tm) % seq + _iota((tm, 1), 0)
    else:
        pos = lax.rem(_iota((tm, 1), 0), seq)
    p1 = jnp.where(pos >= 1, ext[7:7 + tm], st1_ref[0])
    p2 = jnp.where(pos >= 2, ext[6:6 + tm], st2_ref[0])
    cw = convw_ref[...]
    y_c = gb_ref[0] * (cw[0:1] * p2 + cw[1:2] * p1 + cw[2:3] * u)
    yn = y_c * lax.rsqrt(jnp.mean(y_c * y_c, axis=-1, keepdims=True) + EPS) * gconv_ref[...]
    o = o_ref[0]
    on = o * lax.rsqrt(jnp.sum(o * o, axis=-1, keepdims=True) * (1.0 / NSA_DIM) + EPS) * gnsa_ref[...]
    mix = _dot(yn.astype(BF16), woc_ref[...]) + _dot(on.astype(BF16), won_ref[...])
    x1 = _layernorm(ALPHA * x_ref[0] + g1_ref[0] * mix, ln1g_ref[...], ln1b_ref[...])
    h = (x1 * (1.0 + sc2_ref[0]) + sh2_ref[0]).astype(BF16)
    a = _dot(h, wg_ref[...])
    f = (a * jax.nn.sigmoid(a)) * _dot(h, wu_ref[...])
    f = _dot(f.astype(BF16), wd_ref[...])
    y_ref[0] = _layernorm(ALPHA * x1 + g2_ref[0] * f, ln2g_ref[...], ln2b_ref[...])


def _post(x, o, u, gb, st1, st2, mods, w, tm, seq):
    b, t, d = x.shape
    row = lambda cols: pl.BlockSpec((1, tm, cols), lambda i, j: (i, j, 0))

    def bcast(a):
        if a.shape[1] == t:
            return pl.BlockSpec((1, tm, a.shape[2]), lambda i, j: (i, j, 0))
        return pl.BlockSpec((1, 1, a.shape[2]), lambda i, j: (i, 0, 0))

    prev = pl.BlockSpec((1, 8, CONV_DIM), lambda i, j: (i, jnp.maximum(j * (tm // 8) - 1, 0), 0))
    consts = [w["conv_w"], w["g_conv"], w["g_nsa"], w["wo_c"], w["wo_n"], w["ln1_g"], w["ln1_b"], w["ln2_g"],
              w["ln2_b"], w["w_gate"], w["w_up"], w["w_down"]]
    return pl.pallas_call(
        functools.partial(_post_kernel, seq=seq),
        grid=(b, t // tm),
        in_specs=[row(d), row(QZ_DIM), row(CONV_DIM), prev, row(CONV_DIM), bcast(st1), bcast(st2)]
        + [bcast(m) for m in mods] + [_const_spec(c.shape) for c in consts],
        out_specs=row(d),
        out_shape=jax.ShapeDtypeStruct((b, t, d), F32),
        compiler_params=_cparams(2),
        name="post_block",
    )(x, o, u, u, gb, st1, st2, *mods, *consts)


def _prep_w_in(w_in):
    d = w_in.shape[0]
    c3 = 3 * CONV_DIM
    splits = (CONV_DIM, 2 * CONV_DIM, c3, c3 + NSA_DIM, c3 + NSA_DIM + KV_COLS, c3 + NSA_DIM + 2 * KV_COLS,
              c3 + NSA_DIM + 3 * KV_COLS)
    hc, gb, gc, q, kvc, kvs, kvw, gl = jnp.split(w_in, splits, axis=1)
    q4 = q.reshape(d, N_KV, Q_PER_KV, HEAD_DIM) * SCALE
    qz = jnp.einsum("dgrh,gk->dgrkh", q4, jnp.eye(N_KV, dtype=w_in.dtype)).reshape(d, QZ_DIM)
    glp = jnp.pad(gl, ((0, 0), (0, LANES - gl.shape[1])))
    return jnp.concatenate([hc, gb, gc, qz, kvc, kvs, kvw, glp], axis=1).astype(BF16)


def _prep_cmp(cmp_pe, cmp_w1, cmp_b1, cmp_w2):
    eye = jnp.eye(N_KV, dtype=cmp_w1.dtype)
    w1p = cmp_w1.reshape(2, 2, CMP_STRIDE // 2, 2, HEAD_DIM, CMP_HID)
    wpair = jnp.einsum("etpjdh,gk->epjgdtkh", w1p, eye).reshape(
        2, CMP_STRIDE // 2, 2 * N_KV * HEAD_DIM, 2 * N_KV * CMP_HID).astype(BF16)
    w2pair = jnp.einsum("ehd,gk->eghkd", cmp_w2, eye).reshape(2, N_KV * CMP_HID, N_KV * HEAD_DIM).astype(BF16)
    pe2 = cmp_pe.reshape(2, CMP_BLOCK * HEAD_DIM)
    w1f = cmp_w1.reshape(2, CMP_BLOCK * HEAD_DIM, CMP_HID).astype(BF16)
    return wpair, pe2, w1f, cmp_b1, w2pair


def _prep_post(conv_w, g_conv, g_nsa, w_o, ln1_g, ln1_b, ln2_g, ln2_b, w_gate, w_up, w_down):
    d = w_o.shape[1]
    own = jnp.eye(N_KV, dtype=w_o.dtype)
    g_nsa_z = jnp.einsum("grh,gk->grkh", g_nsa.reshape(N_KV, Q_PER_KV, HEAD_DIM), own).reshape(1, QZ_DIM)
    wo_n = jnp.einsum("grhd,gk->grkhd", w_o[CONV_DIM:].reshape(N_KV, Q_PER_KV, HEAD_DIM, d), own)
    row = lambda a: a.reshape(1, -1)
    return dict(conv_w=conv_w, g_conv=row(g_conv), g_nsa=g_nsa_z, wo_c=w_o[:CONV_DIM].astype(BF16),
                wo_n=wo_n.reshape(QZ_DIM, d).astype(BF16), ln1_g=row(ln1_g), ln1_b=row(ln1_b),
                ln2_g=row(ln2_g), ln2_b=row(ln2_b), w_gate=w_gate.astype(BF16), w_up=w_up.astype(BF16),
                w_down=w_down.astype(BF16))


def _feature_major(cache):
    n, rows = cache.shape[:2]
    return jnp.transpose(cache, (0, 2, 3, 4, 1)).reshape(n, KV_COLS, rows)


def kernel(x_prompt, x_sample, cache_cmp_kv, cache_slc_kv, cache_win_kv, state_conv, page_table, c_prompt,
           c_sample, w_ada, b_ada, w_in, conv_w, cmp_pe, cmp_w1, cmp_b1, cmp_w2, g_conv_out, g_nsa_out, w_o,
           ln1_g, ln1_b, ln2_g, ln2_b, w_ffn_gate, w_ffn_up, w_ffn_down):
    assert w_ada.shape[0] == DEPTH
    bp, s, d = x_prompt.shape
    bs, t, _ = x_sample.shape
    kv_shape = (2, N_KV, HEAD_DIM)

    w_cat = _prep_w_in(w_in[0])
    cmp_w = _prep_cmp(cmp_pe[0], cmp_w1[0], cmp_b1[0], cmp_w2[0])
    post_w = _prep_post(conv_w[0], g_conv_out[0], g_nsa_out[0], w_o[0], ln1_g[0], ln1_b[0], ln2_g[0], ln2_b[0],
                        w_ffn_gate[0], w_ffn_up[0], w_ffn_down[0])

    mod = _modulation(jnp.concatenate([c_prompt, c_sample], axis=0), w_ada[0], b_ada[0])
    mods_p = [m[:, None, :] for m in jnp.split(mod[:bp], 6, axis=-1)]
    mods_s = [jnp.repeat(m, t, axis=0)[None] for m in jnp.split(mod[bp:], 6, axis=-1)]

    u, gb, qz, kvc, kvs, kvw, kvs_b, kvw_b, gates = _in_proj(x_prompt, mods_p[1], mods_p[0], w_cat, 512, BF16)
    kv_cmpr = _compress_dense(kvc, *cmp_w)
    o = _attn_prompt(qz, kvs_b, kvw_b, kv_cmpr, gates)
    zero_state = jnp.zeros((bp, 1, CONV_DIM), F32)
    y_prompt = _post(x_prompt, o, u, gb, zero_state, zero_state, mods_p[2:], post_w, 512, s)
    w_keep = min(WINDOW, s)
    cmp_p = kvc.reshape(1, bp, s, *kv_shape)
    slc_p = kvs.reshape(1, bp, s, *kv_shape)
    win_p = kvw[:, s - w_keep:].reshape(1, bp, w_keep, *kv_shape)
    conv_p = u[:, s - (CONV_W - 1):][None]

    rows = bs * t
    us, gbs, qzs, kvcs, kvss, kvws, _, _, gates_s = _in_proj(
        x_sample.reshape(1, rows, d), mods_s[1], mods_s[0], w_cat, rows, F32)
    per_b = lambda a: a.reshape(bs, t, a.shape[-1])
    kvcs, kvss, kvws = per_b(kvcs), per_b(kvss), per_b(kvws)
    xnew = jnp.pad(kvcs, ((0, 0), (0, (-t) % CMP_STRIDE), (0, 0))).reshape(bs, 1, CHUNK_COLS)
    kv_cmpr_s = _compress_paged(_feature_major(cache_cmp_kv[0]), page_table, xnew, *cmp_w)
    o_s = _attn_sample(per_b(qzs), kv_cmpr_s, per_b(gates_s), kvss, _feature_major(cache_slc_kv[0]),
                       _feature_major(cache_win_kv[0]), kvws, page_table)
    st = state_conv[0]
    tpos = jnp.arange(t)[None, :, None]
    st1 = jnp.where(tpos == 0, st[:, 1:2], 0.0).reshape(1, rows, CONV_DIM)
    st2 = jnp.where(tpos == 0, st[:, 0:1], jnp.where(tpos == 1, st[:, 1:2], 0.0)).reshape(1, rows, CONV_DIM)
    y_sample = _post(x_sample.reshape(1, rows, d), o_s.reshape(1, rows, QZ_DIM), us, gbs, st1, st2, mods_s[2:],
                     post_w, rows, t).reshape(bs, t, d)
    cmp_s = kvcs.reshape(1, bs, t, *kv_shape)
    slc_s = kvss.reshape(1, bs, t, *kv_shape)
    win_all = jnp.concatenate([cache_win_kv[0], kvws.reshape(bs, t, *kv_shape)], axis=1)
    win_s = win_all[:, t:][None]
    conv_s = jnp.concatenate([st, per_b(us)], axis=1)[:, t:][None]
    return (y_prompt, y_sample, cmp_p, slc_p, win_p, conv_p, cmp_s, slc_s, win_s, conv_s)
```

```python
import functools

import jax
import jax.numpy as jnp
from jax import lax
from jax.experimental import pallas as pl
from jax.experimental.pallas import tpu as pltpu

HEAD_DIM = 64
N_KV = 2
Q_PER_KV = 6
N_HEADS = N_KV * Q_PER_KV
CONV_DIM = 256
CONV_W = 3
NSA_DIM = N_HEADS * HEAD_DIM
KV_COLS = 2 * N_KV * HEAD_DIM
CMP_BLOCK = 32
CMP_STRIDE = 16
CMP_HID = 2 * HEAD_DIM
SEL_BLOCK = 64
CMP_PER_SEL = SEL_BLOCK // CMP_STRIDE
N_SELECT = 16
WINDOW = 512
Q_BLOCK = 128
DEPTH = 1
ALPHA = (2 * DEPTH) ** 0.25
EPS = 1e-5
SCALE = HEAD_DIM ** -0.5

LANES = 128
QZ_DIM = N_HEADS * LANES
CHUNK_COLS = CMP_STRIDE * KV_COLS
SEL_SHIFT = SEL_BLOCK.bit_length() - 1
HEAD_SHIFT = HEAD_DIM.bit_length() - 1
assert 1 << SEL_SHIFT == SEL_BLOCK and 1 << HEAD_SHIFT == HEAD_DIM
NEG = -1e30
VMEM_LIMIT = 56 * 1024 * 1024
MOD_COL_TILE = 512
IN_PROJ_ROWS = 512
POST_ROWS = 512

F32 = jnp.float32
BF16 = jnp.bfloat16


def _cparams(n_grid):
    return pltpu.CompilerParams(dimension_semantics=("arbitrary",) * n_grid, vmem_limit_bytes=VMEM_LIMIT)


def _const_spec(shape):
    nd = len(shape)
    return pl.BlockSpec(shape, lambda *_: (0,) * nd, pipeline_mode=pl.Buffered(1))


def _dot(a, b):
    return jnp.dot(a, b, preferred_element_type=F32)


def _dot_nt(a, b):
    return lax.dot_general(a, b, (((1,), (1,)), ((), ())), preferred_element_type=F32)


def _iota(shape, dim):
    return lax.broadcasted_iota(jnp.int32, shape, dim)


def _lane_tiles(x):
    return [x[:, w * LANES:(w + 1) * LANES] for w in range(x.shape[1] // LANES)]


def _mod_kernel(c_ref, w_ref, b_ref, o_ref):
    c = c_ref[...]
    a = (c * jax.nn.sigmoid(c)).astype(BF16)
    o_ref[...] = _dot(a, w_ref[...].astype(BF16)) + b_ref[...]


def _modulation(c_all, w_ada, b_ada):
    m, d = c_all.shape
    n = w_ada.shape[1]
    tn = MOD_COL_TILE
    return pl.pallas_call(
        _mod_kernel,
        grid=(n // tn,),
        in_specs=[pl.BlockSpec((m, d), lambda j: (0, 0)),
                  pl.BlockSpec((d, tn), lambda j: (0, j)),
                  pl.BlockSpec((1, tn), lambda j: (0, j))],
        out_specs=pl.BlockSpec((m, tn), lambda j: (0, j)),
        out_shape=jax.ShapeDtypeStruct((m, n), F32),
        compiler_params=_cparams(1),
        name="modulation",
    )(c_all, w_ada, b_ada.reshape(1, n))


_C_HC, _C_GB, _C_GC, _C_Q = 0, CONV_DIM, 2 * CONV_DIM, 3 * CONV_DIM
_C_KVC = _C_Q + QZ_DIM
_C_KVS = _C_KVC + KV_COLS
_C_KVW = _C_KVS + KV_COLS
_C_GL = _C_KVW + KV_COLS
W_IN_COLS = _C_GL + LANES


def _inproj_kernel(x_ref, sc_ref, sh_ref, w_ref, u_ref, gb_ref, qz_ref, kvc_ref, kvs_ref, kvw_ref,
                   kvsb_ref, kvwb_ref, gates_ref):
    h = (x_ref[0] * (1.0 + sc_ref[0]) + sh_ref[0]).astype(BF16)
    hc = _dot(h, w_ref[:, _C_HC:_C_GB])
    gb_ref[0] = _dot(h, w_ref[:, _C_GB:_C_GC])
    gc = _dot(h, w_ref[:, _C_GC:_C_Q])
    u_ref[0] = gc * hc
    qz_ref[0] = _dot(h, w_ref[:, _C_Q:_C_KVC]).astype(qz_ref.dtype)
    kvc_ref[0] = _dot(h, w_ref[:, _C_KVC:_C_KVS])
    kvs = _dot(h, w_ref[:, _C_KVS:_C_KVW])
    kvs_ref[0] = kvs
    kvsb_ref[0] = kvs.astype(BF16)
    kvw = _dot(h, w_ref[:, _C_KVW:_C_GL])
    kvw_ref[0] = kvw
    kvwb_ref[0] = kvw.astype(BF16)
    gates_ref[0] = jax.nn.sigmoid(_dot(h, w_ref[:, _C_GL:W_IN_COLS]))


def _in_proj(x, scale, shift, w_cat, tm, qz_dtype):
    b, t, d = x.shape
    rm = scale.shape[1]
    mod_block = (1, tm, d) if rm == t else (1, 1, d)
    mod_map = (lambda i, j: (i, j, 0)) if rm == t else (lambda i, j: (i, 0, 0))
    row = lambda cols: pl.BlockSpec((1, tm, cols), lambda i, j: (i, j, 0))
    sds = lambda cols, dt: jax.ShapeDtypeStruct((b, t, cols), dt)
    return pl.pallas_call(
        _inproj_kernel,
        grid=(b, t // tm),
        in_specs=[row(d), pl.BlockSpec(mod_block, mod_map), pl.BlockSpec(mod_block, mod_map),
                  _const_spec((d, W_IN_COLS))],
        out_specs=[row(CONV_DIM), row(CONV_DIM), row(QZ_DIM), row(KV_COLS), row(KV_COLS), row(KV_COLS),
                   row(KV_COLS), row(KV_COLS), row(LANES)],
        out_shape=[sds(CONV_DIM, F32), sds(CONV_DIM, F32), sds(QZ_DIM, qz_dtype), sds(KV_COLS, F32),
                   sds(KV_COLS, F32), sds(KV_COLS, F32), sds(KV_COLS, BF16), sds(KV_COLS, BF16),
                   sds(LANES, F32)],
        compiler_params=_cparams(2),
        name="in_proj",
    )(x, scale, shift, w_cat)


def _gelu_tanh(x):
    return jax.nn.gelu(x, approximate=True)


def _cmp_bias_e(pe_ref, w1f_ref, b1_ref, e):
    pe = jnp.broadcast_to(pe_ref[e:e + 1, :], (8, pe_ref.shape[1])).astype(BF16)
    return _dot(pe, w1f_ref[e])[0:1, :] + b1_ref[e:e + 1, :]


def _chunk_partials(row_pairs, wpair_ref, e):
    acc = None
    for jp in range(CMP_STRIDE // 2):
        part = _dot(row_pairs(jp), wpair_ref[e, jp])
        acc = part if acc is None else acc + part
    return acc


def _compress_mlp(top, bot_next, pe_ref, w1f_ref, b1_ref, w2_ref, e):
    bias = jnp.concatenate([_cmp_bias_e(pe_ref, w1f_ref, b1_ref, e)] * N_KV, axis=1)
    return _dot(_gelu_tanh(top + bot_next + bias).astype(BF16), w2_ref[e])


def _compress_dense_kernel(x_ref, wpair_ref, pe_ref, w1f_ref, b1_ref, w2_ref, o_ref, r_sc):
    n = o_ref.shape[1]
    half_cols = N_KV * HEAD_DIM
    hid_cols = N_KV * CMP_HID
    outs = []
    for e in range(2):
        r_sc[e] = x_ref[0, :, e * half_cols:(e + 1) * half_cols]
        acc = _chunk_partials(
            lambda jp: jnp.concatenate([r_sc[e, pl.ds(2 * jp + jj, n, stride=CMP_STRIDE), :] for jj in (0, 1)],
                                       axis=1).astype(BF16), wpair_ref, e)
        bot_next = pltpu.roll(acc[:, hid_cols:], n - 1, 0)
        outs.append(_compress_mlp(acc[:, :hid_cols], bot_next, pe_ref, w1f_ref, b1_ref, w2_ref, e))
    o_ref[0] = jnp.concatenate(outs, axis=1).astype(o_ref.dtype)


def _compress_dense(kvc, wpair, pe2, w1f, b1, w2pair):
    b, s, _ = kvc.shape
    nch = s // CMP_STRIDE
    return pl.pallas_call(
        _compress_dense_kernel,
        grid=(b,),
        in_specs=[pl.BlockSpec((1, s, KV_COLS), lambda i: (i, 0, 0)),
                  _const_spec(wpair.shape), _const_spec(pe2.shape), _const_spec(w1f.shape),
                  _const_spec(b1.shape), _const_spec(w2pair.shape)],
        out_specs=pl.BlockSpec((1, nch, KV_COLS), lambda i: (i, 0, 0)),
        out_shape=jax.ShapeDtypeStruct((b, nch, KV_COLS), BF16),
        scratch_shapes=[pltpu.VMEM((2, s, LANES), F32)],
        compiler_params=_cparams(1),
        name="compress_prompt",
    )(kvc, wpair, pe2, w1f, b1, w2pair)


PAGES_PER_STEP = 32
CHUNK_PITCH = 24


def _page_copy(pool_ref, page_index, buf, sem, slot, p):
    return pltpu.make_async_copy(pool_ref.at[page_index], buf.at[slot, p], sem.at[slot, p])


def _await_pages(pt_ref, pool_ref, buf, sem, n_steps):
    b, kt = pl.program_id(0), pl.program_id(1)

    @pl.when((b == 0) & (kt == 0))
    def _():
        for k in range(n_steps):
            for p in range(PAGES_PER_STEP):
                _page_copy(pool_ref, pt_ref[0, k * PAGES_PER_STEP + p], buf, sem, k, p).start(priority=p % 2)

    for p in range(PAGES_PER_STEP):
        _page_copy(pool_ref, 0, buf, sem, kt, p).wait()
    return kt


def _refill_pages(pt_ref, pool_ref, buf, sem):
    kt = pl.program_id(1)
    nb = jnp.minimum(pl.program_id(0) + 1, pl.num_programs(0) - 1)
    for p in range(PAGES_PER_STEP):
        _page_copy(pool_ref, pt_ref[nb, kt * PAGES_PER_STEP + p], buf, sem, kt, p).start(priority=p % 2)


def _drain_pages(pool_ref, buf, sem, n_steps):
    @pl.when((pl.program_id(0) == pl.num_programs(0) - 1) & (pl.program_id(1) == n_steps - 1))
    def _():
        for k in range(n_steps):
            for p in range(PAGES_PER_STEP):
                _page_copy(pool_ref, 0, buf, sem, k, p).wait()


def _compress_paged_kernel(pt_ref, pool_ref, *refs, n_steps):
    g_pages = PAGES_PER_STEP
    (xnew_ref, wpair_ref, pe_ref, w1f_ref, b1_ref, w2_ref, o_ref, r_sc, tb_sc, buf, sem) = refs
    kt = pl.program_id(1)
    page = buf.shape[3]
    slot = _await_pages(pt_ref, pool_ref, buf, sem, n_steps)
    cpp = page // CMP_STRIDE
    rows_step = g_pages * cpp
    half_cols = N_KV * HEAD_DIM
    hid_cols = N_KV * CMP_HID

    row0 = pl.multiple_of(kt * rows_step, rows_step)
    for e in range(2):
        for p in range(g_pages):
            rows_t = buf[slot, p, e * half_cols:(e + 1) * half_cols, :].T
            for c in range(cpp):
                r_sc[e, pl.ds((p * cpp + c) * CHUNK_PITCH, CMP_STRIDE), :] = rows_t[c * CMP_STRIDE:(c + 1) * CMP_STRIDE]
        tb_sc[e, pl.ds(row0, rows_step), :] = _chunk_partials(
            lambda jp: jnp.concatenate([r_sc[e, pl.ds(2 * jp + jj, rows_step, stride=CHUNK_PITCH), :]
                                        for jj in (0, 1)], axis=1).astype(BF16), wpair_ref, e)
    _refill_pages(pt_ref, pool_ref, buf, sem)

    @pl.when(kt == n_steps - 1)
    def _():
        n = tb_sc.shape[1]
        xn = jnp.broadcast_to(xnew_ref[0], (8, CHUNK_COLS)).astype(BF16)
        last = _iota((n, 1), 0) == n - 1
        outs = []
        for e in range(2):
            new_pair = lambda jp: jnp.concatenate(
                [xn[:, (2 * jp + jj) * KV_COLS + e * half_cols:(2 * jp + jj) * KV_COLS + (e + 1) * half_cols]
                 for jj in (0, 1)], axis=1)
            bot_new = _chunk_partials(new_pair, wpair_ref, e)[0:1, hid_cols:]
            top, bot = tb_sc[e, :, :hid_cols], tb_sc[e, :, hid_cols:]
            bot_next = jnp.where(last, bot_new, pltpu.roll(bot, n - 1, 0))
            outs.append(_compress_mlp(top, bot_next, pe_ref, w1f_ref, b1_ref, w2_ref, e))
        o_ref[0] = jnp.concatenate(outs, axis=1).astype(o_ref.dtype)

    _drain_pages(pool_ref, buf, sem, n_steps)


def _page_stream_scratch(page, n_steps):
    return [pltpu.VMEM((n_steps, PAGES_PER_STEP, KV_COLS, page), F32),
            pltpu.SemaphoreType.DMA((n_steps, PAGES_PER_STEP))]


def _compress_paged(pool_t, page_table, xnew, wpair, pe2, w1f, b1, w2pair):
    bn, n_pages = page_table.shape
    page = pool_t.shape[2]
    g = PAGES_PER_STEP
    n_steps = n_pages // g
    assert n_pages % g == 0 and page % CMP_STRIDE == 0 and page == LANES
    n = n_pages * page // CMP_STRIDE

    def const(shape):
        nd = len(shape)
        return pl.BlockSpec(shape, lambda i, j, pt: (0,) * nd, pipeline_mode=pl.Buffered(1))

    grid_spec = pltpu.PrefetchScalarGridSpec(
        num_scalar_prefetch=1,
        grid=(bn, n_steps),
        in_specs=[pl.BlockSpec(memory_space=pl.ANY),
                  pl.BlockSpec((1, 1, CHUNK_COLS), lambda i, j, pt: (i, 0, 0)), const(wpair.shape),
                  const(pe2.shape), const(w1f.shape), const(b1.shape), const(w2pair.shape)],
        out_specs=pl.BlockSpec((1, n, KV_COLS), lambda i, j, pt: (i, 0, 0)),
        scratch_shapes=[pltpu.VMEM((2, g * (page // CMP_STRIDE) * CHUNK_PITCH, LANES), F32),
                        pltpu.VMEM((2, n, 2 * N_KV * CMP_HID), F32)] + _page_stream_scratch(page, n_steps),
    )
    return pl.pallas_call(
        functools.partial(_compress_paged_kernel, n_steps=n_steps),
        grid_spec=grid_spec,
        out_shape=jax.ShapeDtypeStruct((bn, n, KV_COLS), BF16),
        compiler_params=_cparams(2),
        name="compress_sample",
    )(page_table, pool_t, xnew, wpair, pe2, w1f, b1, w2pair)


def _split3_dot(x, m01):
    h1 = x.astype(BF16)
    r1 = x - h1.astype(F32)
    h2 = r1.astype(BF16)
    h3 = (r1 - h2.astype(F32)).astype(BF16)
    return _dot(h1, m01) + _dot(h2, m01) + _dot(h3, m01)


def _block_scores(imp, n_lanes):
    ncp = imp.shape[1]
    c = _iota((ncp, n_lanes), 0)
    j = _iota((ncp, n_lanes), 1)
    a = jnp.where((c >= CMP_PER_SEL * j - 1) & (c <= CMP_PER_SEL * j + CMP_PER_SEL - 1), 1.0, 0.0).astype(BF16)
    return _split3_dot(imp, a)


def _ranked_scores(score, qpos, n_blocks):
    j = _iota(score.shape, 1)
    cur = qpos >> SEL_SHIFT
    valid = j * SEL_BLOCK <= qpos
    forced = (j == 0) | (j == cur) | (j == cur - 1)
    return jnp.where(forced, -NEG, jnp.where(valid, score, NEG))


def _select_blocks_pairwise(score, qpos, n_blocks, n_select):
    r, l = score.shape
    nb8 = -(-n_blocks // 8) * 8
    ranked = _ranked_scores(score, qpos, n_blocks)
    ranked_t = jnp.concatenate([ranked, jnp.zeros((LANES - r, l), F32)], axis=0).T
    lower_index = jnp.where(_iota((nb8, l), 0) < _iota((nb8, l), 1), 1.0, 0.0)
    in_range = _iota((1, l), 1) < n_blocks
    rows = []
    for v in range(r):
        col = jnp.broadcast_to(ranked_t[:nb8, v:v + 1], (nb8, l))
        row = ranked[v:v + 1, :]
        beats = jnp.where(col > row, 1.0, jnp.where(col == row, lower_index, 0.0))
        rank = jnp.sum(beats, axis=0, keepdims=True)
        rows.append(jnp.where((rank < n_select) & in_range, 1.0, 0.0))
    return jnp.concatenate(rows, axis=0)


def _select_blocks_rank(score, qpos, n_blocks, n_select):
    r, l = score.shape
    nb8 = -(-n_blocks // 8) * 8
    st = _ranked_scores(score, qpos, n_blocks).T[:nb8]
    jrow = _iota((nb8, r), 0)
    rank = jnp.zeros((nb8, r), F32)
    for jp in range(n_blocks):
        row = st[jp:jp + 1, :]
        beats = (row > st) | ((row == st) & (jrow > jp))
        rank = rank + jnp.where(beats, 1.0, 0.0)
    sel_t = jnp.where((rank < n_select) & (jrow < n_blocks), 1.0, 0.0)
    if nb8 < l:
        sel_t = jnp.concatenate([sel_t, jnp.zeros((l - nb8, r), F32)], axis=0)
    return sel_t.T


def _softmax_parts(s, mask):
    s = jnp.where(mask, s, NEG)
    m = jnp.max(s, axis=-1, keepdims=True)
    e = jnp.where(mask, jnp.exp(s - m), 0.0)
    return e, jnp.sum(e, axis=-1, keepdims=True)


SEL_TILE = 512


def _attn_prompt_kernel(qz_ref, kvs_ref, kvw_ref, kvc_ref, gates_ref, exp_ref, o_ref,
                        s_sc, p_sc, pw_sc, m_sc, acc_sc, ocw_sc, *, seq):
    qb = Q_BLOCK
    tk = SEL_TILE
    start = pl.program_id(1) * qb
    qpos = start + _iota((qb, 1), 0)
    ncp = kvc_ref.shape[1]
    n_blocks = -(-seq // SEL_BLOCK)
    rows = Q_PER_KV * qb
    gates = gates_ref[0]
    n_tiles = (start + qb + tk - 1) // tk
    head_rows = [slice(r * qb, (r + 1) * qb) for r in range(Q_PER_KV)]
    kv_heads = range(N_KV)

    def q_rows(g):
        return jnp.concatenate(
            [qz_ref[0, :, (g * Q_PER_KV + r) * LANES:(g * Q_PER_KV + r + 1) * LANES] for r in range(Q_PER_KV)],
            axis=0)


    def values_and_ones(g, vv):
        return jnp.where((_iota((1, LANES), 1) >> HEAD_SHIFT) == g, vv, jnp.ones_like(vv))

    def normalised(acc):
        return acc / jnp.maximum(pltpu.roll(acc, LANES // 2, 1), 1e-30)

    o_cmp, sel = [], []
    c = _iota((qb, ncp), 1)
    cmask = (CMP_STRIDE * c + (CMP_BLOCK - 1) <= qpos) & (c < ncp - 1)
    for g in kv_heads:
        s = _dot_nt(q_rows(g), kvc_ref[0, :, 0:LANES]).reshape(Q_PER_KV, qb, ncp)
        e, l = _softmax_parts(s, cmask)
        p = e / jnp.maximum(l, 1e-30)
        o_cmp.append(_dot(p.reshape(rows, ncp).astype(BF16), kvc_ref[0, :, LANES:2 * LANES]))
        imp = jnp.sum(p, axis=0)
        sel.append(_select_blocks_rank(_block_scores(imp, LANES), qpos, n_blocks,
                                       min(N_SELECT, n_blocks)).astype(BF16))

    wlen = WINDOW + qb
    w0 = pl.multiple_of(jnp.maximum(start - WINDOW, 0), qb)
    kpos = w0 + _iota((1, wlen), 1)
    wmask = (kpos <= qpos) & (kpos > qpos - WINDOW)
    for g in kv_heads:
        s = _dot_nt(q_rows(g), kvw_ref[0, pl.ds(w0, wlen), 0:LANES])
        for rs in head_rows:
            sr = jnp.where(wmask, s[rs], NEG)
            m = jnp.max(functools.reduce(jnp.maximum, _lane_tiles(sr)), axis=-1, keepdims=True)
            pw_sc[g, rs, :] = jnp.exp((sr - m).astype(BF16))
        o_win = normalised(_dot(pw_sc[g], values_and_ones(g, kvw_ref[0, pl.ds(w0, wlen), LANES:2 * LANES])))
        for r, rs in enumerate(head_rows):
            h = g * Q_PER_KV + r
            ocw_sc[g, rs, :] = (gates[:, h:h + 1] * o_cmp[g][rs]
                                + gates[:, 2 * N_HEADS + h:2 * N_HEADS + h + 1] * o_win[rs])

    m_sc[...] = jnp.full(m_sc.shape, NEG, F32)

    def pass1(kt, _):
        k0 = pl.multiple_of(kt * tk, tk)
        causal = (k0 + _iota((1, tk), 1)) <= qpos
        for g in kv_heads:
            s = _dot_nt(q_rows(g), kvs_ref[0, pl.ds(k0, tk), 0:LANES])
            mask = (_dot(sel[g], exp_ref[kt]) > 0.5) & causal
            for rs in head_rows:
                sr = jnp.where(mask, s[rs], NEG)
                s_sc[g, kt, rs, :] = sr
                m_sc[g, rs, :] = jnp.maximum(m_sc[g, rs, :], functools.reduce(jnp.maximum, _lane_tiles(sr)))
        return 0

    lax.fori_loop(0, n_tiles, pass1, 0)
    m_sc[...] = jnp.broadcast_to(jnp.max(m_sc[...], axis=-1, keepdims=True), m_sc.shape)
    acc_sc[...] = jnp.zeros(acc_sc.shape, F32)

    def pass2(kt, _):
        k0 = pl.multiple_of(kt * tk, tk)
        for g in kv_heads:
            for rs in head_rows:
                x = s_sc[g, kt, rs, :] - jnp.concatenate([m_sc[g, rs, :]] * (tk // LANES), axis=1)
                p_sc[g, rs, :] = jnp.exp(x.astype(BF16))
            vv = values_and_ones(g, kvs_ref[0, pl.ds(k0, tk), LANES:2 * LANES])
            acc_sc[g] = acc_sc[g] + _dot(p_sc[g], vv)
        return 0

    lax.fori_loop(0, n_tiles, pass2, 0)

    for g in kv_heads:
        o_slc = normalised(acc_sc[g])
        own = (_iota((qb, LANES), 1) >> HEAD_SHIFT) == g
        for r, rs in enumerate(head_rows):
            h = g * Q_PER_KV + r
            o = ocw_sc[g, rs, :] + gates[:, N_HEADS + h:N_HEADS + h + 1] * o_slc[rs]
            o_ref[0, :, h * LANES:(h + 1) * LANES] = jnp.where(own, o, 0.0)


def _expand_matrix(n_tiles, tile):
    t = jnp.arange(n_tiles)[:, None, None]
    j = jnp.arange(LANES)[None, :, None]
    k = jnp.arange(tile)[None, None, :]
    return (j == (t * tile + k) // SEL_BLOCK).astype(BF16)


def _attn_prompt(qz, kvs_b, kvw_b, kv_cmpr, gates):
    b, s, _ = qz.shape
    assert s % Q_BLOCK == 0 and s % SEL_TILE == 0 and s >= WINDOW + Q_BLOCK and s // SEL_BLOCK <= LANES
    ncp = kv_cmpr.shape[1]
    n_tiles = s // SEL_TILE
    expand = _expand_matrix(n_tiles, SEL_TILE)
    rows = Q_PER_KV * Q_BLOCK
    full = lambda n: pl.BlockSpec((1, n, KV_COLS), lambda i, j: (i, 0, 0))
    return pl.pallas_call(
        functools.partial(_attn_prompt_kernel, seq=s),
        grid=(b, s // Q_BLOCK),
        in_specs=[pl.BlockSpec((1, Q_BLOCK, QZ_DIM), lambda i, j: (i, j, 0)), full(s), full(s), full(ncp),
                  pl.BlockSpec((1, Q_BLOCK, LANES), lambda i, j: (i, j, 0)), _const_spec(expand.shape)],
        out_specs=pl.BlockSpec((1, Q_BLOCK, QZ_DIM), lambda i, j: (i, j, 0)),
        out_shape=jax.ShapeDtypeStruct((b, s, QZ_DIM), F32),
        scratch_shapes=[pltpu.VMEM((N_KV, n_tiles, rows, SEL_TILE), F32), pltpu.VMEM((N_KV, rows, SEL_TILE), BF16),
                        pltpu.VMEM((N_KV, rows, WINDOW + Q_BLOCK), BF16), pltpu.VMEM((N_KV, rows, LANES), F32),
                        pltpu.VMEM((N_KV, rows, LANES), F32), pltpu.VMEM((N_KV, rows, LANES), F32)],
        compiler_params=_cparams(2),
        name="attn_prompt",
    )(qz, kvs_b, kvw_b, kv_cmpr, gates, expand)


def _attn_sample_kernel(pt_ref, pool_ref, *refs, past, n_steps):
    g_pages = PAGES_PER_STEP
    (qz_ref, kvc_ref, gates_ref, kvs_new_ref, win_ref, kvw_new_ref, exp_ref,
     o_ref, q_sc, sel_sc, s_sc, v_sc, snew_sc, mp_sc, ocw_sc, buf, sem) = refs
    kt = pl.program_id(1)
    t = qz_ref.shape[1]
    rows = N_HEADS * t
    page = buf.shape[3]
    tile = g_pages * page
    blocks_per_tile = tile // SEL_BLOCK
    n_past_blocks = past // SEL_BLOCK
    n_blocks = n_past_blocks + -(-t // SEL_BLOCK)
    sel_lanes = sel_sc.shape[0] * LANES
    half_cols = N_KV * HEAD_DIM
    qpos = past + _iota((t, 1), 0)
    qpos_gt = jnp.concatenate([qpos] * N_KV, axis=0)
    own_rows = _iota((N_KV, Q_PER_KV, t, LANES), 0) == (_iota((N_KV, Q_PER_KV, t, LANES), 3) >> HEAD_SHIFT)

    def new_rows_padded():
        kvn = kvs_new_ref[0]
        return jnp.concatenate([kvn, jnp.zeros((LANES - t, KV_COLS), F32)], axis=0).astype(BF16)

    def masked_scores(s, mask):
        k = s.shape[1]
        return jnp.where(mask.reshape(N_KV, 1, t, k), s.reshape(N_KV, Q_PER_KV, t, k), NEG).reshape(rows, k)

    @pl.when(kt == 0)
    def _():
        q = jnp.concatenate([qz_ref[0, :, h * LANES:(h + 1) * LANES] for h in range(N_HEADS)], axis=0)
        q_sc[...] = q.astype(BF16)
        qb = q_sc[...]
        ncp = kvc_ref.shape[1]
        s = _dot_nt(qb, kvc_ref[0, :, 0:LANES]).reshape(N_HEADS, t, ncp)
        c = _iota((t, ncp), 1)
        cmask = CMP_STRIDE * c + (CMP_BLOCK - 1) <= qpos
        e, l = _softmax_parts(s, cmask)
        p = e / jnp.maximum(l, 1e-30)
        o_cmp = _dot(p.reshape(rows, ncp).astype(BF16), kvc_ref[0, :, LANES:2 * LANES])
        imp = jnp.sum(p.reshape(N_KV, Q_PER_KV, t, ncp), axis=1).reshape(N_KV * t, ncp)
        sel = _select_blocks_pairwise(_block_scores(imp, sel_lanes), qpos_gt, n_blocks, min(N_SELECT, n_blocks))
        for w in range(sel_sc.shape[0]):
            sel_sc[w] = sel[:, w * LANES:(w + 1) * LANES]
        wb = win_ref.shape[2]
        win = win_ref[0].astype(BF16)
        kvw_new = kvw_new_ref[0].astype(BF16)
        s_old = _dot(qb, win[0:half_cols]).reshape(N_HEADS, t, wb)
        s_new = _dot_nt(qb, kvw_new[:, 0:half_cols]).reshape(N_HEADS, t, t)
        kpos_old = past - wb + _iota((1, wb), 1)
        kpos_new = past + _iota((1, t), 1)
        mask_old = (kpos_old <= qpos) & (kpos_old > qpos - WINDOW) & (kpos_old >= 0)
        mask_new = (kpos_new <= qpos) & (kpos_new > qpos - WINDOW)
        s_old = jnp.where(mask_old, s_old, NEG)
        s_new = jnp.where(mask_new, s_new, NEG)
        m = jnp.maximum(jnp.max(s_old, axis=-1, keepdims=True), jnp.max(s_new, axis=-1, keepdims=True))
        e_old = jnp.where(mask_old, jnp.exp(s_old - m), 0.0)
        e_new = jnp.where(mask_new, jnp.exp(s_new - m), 0.0)
        l = jnp.sum(e_old, axis=-1, keepdims=True) + jnp.sum(e_new, axis=-1, keepdims=True)
        o_win = (_dot_nt(e_old.reshape(rows, wb).astype(BF16), win[half_cols:])
                 + _dot(e_new.reshape(rows, t).astype(BF16), kvw_new[:, half_cols:]))
        o_win = o_win.reshape(N_HEADS, t, LANES) / jnp.maximum(l, 1e-30)
        gts = gates_ref[0]
        gc = jnp.stack([jnp.broadcast_to(gts[:, h:h + 1], (t, LANES)) for h in range(N_HEADS)])
        gw = jnp.stack([jnp.broadcast_to(gts[:, 2 * N_HEADS + h:2 * N_HEADS + h + 1], (t, LANES))
                        for h in range(N_HEADS)])
        ocw_sc[...] = gc * o_cmp.reshape(N_HEADS, t, LANES) + gw * o_win
        jn = (past + _iota((1, LANES), 1)) >> SEL_SHIFT
        seln = jnp.zeros((N_KV * t, LANES), F32)
        for jb in range(n_past_blocks, n_blocks):
            seln = jnp.where(jn == jb, sel[:, jb:jb + 1], seln)
        npos = past + _iota((1, LANES), 1)
        nmask = (seln > 0.5) & (npos <= qpos_gt) & (npos < past + t)
        s_n = masked_scores(_dot_nt(qb, new_rows_padded()[:, 0:half_cols]), nmask)
        snew_sc[...] = s_n
        mp_sc[...] = s_n

    slot = _await_pages(pt_ref, pool_ref, buf, sem, n_steps)
    k_t = jnp.concatenate([buf[slot, p, 0:half_cols, :] for p in range(g_pages)], axis=1).astype(BF16)
    v_sc[kt] = jnp.concatenate([buf[slot, p, half_cols:, :] for p in range(g_pages)], axis=1).astype(BF16)
    b0 = kt * blocks_per_tile
    selw = sel_sc[b0 // LANES].astype(BF16)
    kpos = kt * tile + _iota((1, tile), 1)
    mask = (_dot(selw, exp_ref[0]) > 0.5) & (kpos <= qpos_gt)
    s = masked_scores(_dot(q_sc[...], k_t), mask)
    s_sc[kt] = s
    mp_sc[...] = jnp.maximum(mp_sc[...], functools.reduce(jnp.maximum, _lane_tiles(s)))
    _refill_pages(pt_ref, pool_ref, buf, sem)

    @pl.when(kt == n_steps - 1)
    def _():
        m = jnp.max(mp_sc[...], axis=-1, keepdims=True)
        e = jnp.exp(snew_sc[...] - m)
        lp = e
        acc = _dot(e.astype(BF16), new_rows_padded()[:, half_cols:])
        for k2 in range(n_steps):
            e = jnp.exp(s_sc[k2] - m)
            lp = lp + functools.reduce(jnp.add, _lane_tiles(e))
            acc = acc + _dot_nt(e.astype(BF16), v_sc[k2])
        l = jnp.sum(lp, axis=-1, keepdims=True)
        gts = gates_ref[0]
        gs = jnp.stack([jnp.broadcast_to(gts[:, N_HEADS + h:N_HEADS + h + 1], (t, LANES))
                        for h in range(N_HEADS)])
        o_slc = (acc / jnp.maximum(l, 1e-30)).reshape(N_HEADS, t, LANES)
        o = jnp.where(own_rows.reshape(N_HEADS, t, LANES), ocw_sc[...] + gs * o_slc, 0.0)
        for h in range(N_HEADS):
            o_ref[0, :, h * LANES:(h + 1) * LANES] = o[h]

    _drain_pages(pool_ref, buf, sem, n_steps)


def _attn_sample(qz, kv_cmpr, gates, kvs_new, slc_pool_t, win_t, kvw_new, page_table):
    bn, t, _ = qz.shape
    n_pages = page_table.shape[1]
    page = slc_pool_t.shape[2]
    past = n_pages * page
    g = PAGES_PER_STEP
    n_steps = n_pages // g
    tile = g * page
    blocks_per_tile = tile // SEL_BLOCK
    assert n_pages % g == 0 and page % SEL_BLOCK == 0 and LANES % blocks_per_tile == 0 and t % 8 == 0
    n_blocks = past // SEL_BLOCK + -(-t // SEL_BLOCK)
    sel_groups = -(-n_blocks // LANES)
    ncp = kv_cmpr.shape[1]
    wb = win_t.shape[2]
    n_var = LANES // blocks_per_tile
    expand = _expand_matrix(n_var, tile)

    per_b = lambda n, cols: pl.BlockSpec((1, n, cols), lambda i, j, pt: (i, 0, 0))
    grid_spec = pltpu.PrefetchScalarGridSpec(
        num_scalar_prefetch=1,
        grid=(bn, n_steps),
        in_specs=[pl.BlockSpec(memory_space=pl.ANY),
                  per_b(t, QZ_DIM), per_b(ncp, KV_COLS), per_b(t, LANES), per_b(t, KV_COLS), per_b(KV_COLS, wb),
                  per_b(t, KV_COLS), pl.BlockSpec((1, LANES, tile), lambda i, j, pt: (j % n_var, 0, 0))],
        out_specs=per_b(t, QZ_DIM),
        scratch_shapes=[pltpu.VMEM((N_HEADS * t, LANES), BF16),
                        pltpu.VMEM((sel_groups, N_KV * t, LANES), F32),
                        pltpu.VMEM((n_steps, N_HEADS * t, tile), F32), pltpu.VMEM((n_steps, LANES, tile), BF16),
                        pltpu.VMEM((N_HEADS * t, LANES), F32), pltpu.VMEM((N_HEADS * t, LANES), F32),
                        pltpu.VMEM((N_HEADS, t, LANES), F32)] + _page_stream_scratch(page, n_steps),
    )
    return pl.pallas_call(
        functools.partial(_attn_sample_kernel, past=past, n_steps=n_steps),
        grid_spec=grid_spec,
        out_shape=jax.ShapeDtypeStruct((bn, t, QZ_DIM), F32),
        compiler_params=_cparams(2),
        name="attn_sample",
    )(page_table, slc_pool_t, qz, kv_cmpr, gates, kvs_new, win_t, kvw_new, expand)


def _layernorm(x, g, b):
    mu = jnp.mean(x, axis=-1, keepdims=True)
    xc = x - mu
    var = jnp.mean(xc * xc, axis=-1, keepdims=True)
    return xc * lax.rsqrt(var + EPS) * g + b


def _post_kernel(x_ref, o_ref, u_ref, uprev_ref, gb_ref, st1_ref, st2_ref, g1_ref, sh2_ref, sc2_ref, g2_ref,
                 convw_ref, gconv_ref, gnsa_ref, woc_ref, won_ref, ln1g_ref, ln1b_ref, ln2g_ref, ln2b_ref,
                 wg_ref, wu_ref, wd_ref, y_ref, *, seq):
    tm = x_ref.shape[1]
    u = u_ref[0]
    ext = jnp.concatenate([uprev_ref[0], u], axis=0)
    if tm <= seq:
        pos = (pl.program_id(1) * tm) % seq + _iota((tm, 1), 0)
    else:
        pos = lax.rem(_iota((tm, 1), 0), seq)
    p1 = jnp.where(pos >= 1, ext[7:7 + tm], st1_ref[0])
    p2 = jnp.where(pos >= 2, ext[6:6 + tm], st2_ref[0])
    cw = convw_ref[...]
    y_c = gb_ref[0] * (cw[0:1] * p2 + cw[1:2] * p1 + cw[2:3] * u)
    yn = y_c * lax.rsqrt(jnp.mean(y_c * y_c, axis=-1, keepdims=True) + EPS) * gconv_ref[...]
    o = o_ref[0]
    on = o * lax.rsqrt(jnp.sum(o * o, axis=-1, keepdims=True) * (1.0 / NSA_DIM) + EPS) * gnsa_ref[...]
    mix = _dot(yn.astype(BF16), woc_ref[...]) + _dot(on.astype(BF16), won_ref[...])
    x1 = _layernorm(ALPHA * x_ref[0] + g1_ref[0] * mix, ln1g_ref[...], ln1b_ref[...])
    h = (x1 * (1.0 + sc2_ref[0]) + sh2_ref[0]).astype(BF16)
    a = _dot(h, wg_ref[...])
    f = (a * jax.nn.sigmoid(a)) * _dot(h, wu_ref[...])
    f = _dot(f.astype(BF16), wd_ref[...])
    y_ref[0] = _layernorm(ALPHA * x1 + g2_ref[0] * f, ln2g_ref[...], ln2b_ref[...])


def _post(x, o, u, gb, st1, st2, mods, w, tm, seq):
    b, t, d = x.shape
    row = lambda cols: pl.BlockSpec((1, tm, cols), lambda i, j: (i, j, 0))

    def bcast(a):
        if a.shape[1] == t:
            return pl.BlockSpec((1, tm, a.shape[2]), lambda i, j: (i, j, 0))
        return pl.BlockSpec((1, 1, a.shape[2]), lambda i, j: (i, 0, 0))

    prev = pl.BlockSpec((1, 8, CONV_DIM), lambda i, j: (i, jnp.maximum(j * (tm // 8) - 1, 0), 0))
    consts = [w["conv_w"], w["g_conv"], w["g_nsa"], w["wo_c"], w["wo_n"], w["ln1_g"], w["ln1_b"], w["ln2_g"],
              w["ln2_b"], w["w_gate"], w["w_up"], w["w_down"]]
    return pl.pallas_call(
        functools.partial(_post_kernel, seq=seq),
        grid=(b, t // tm),
        in_specs=[row(d), row(QZ_DIM), row(CONV_DIM), prev, row(CONV_DIM), bcast(st1), bcast(st2)]
        + [bcast(m) for m in mods] + [_const_spec(c.shape) for c in consts],
        out_specs=row(d),
        out_shape=jax.ShapeDtypeStruct((b, t, d), F32),
        compiler_params=_cparams(2),
        name="post_block",
    )(x, o, u, u, gb, st1, st2, *mods, *consts)


def _prep_w_in(w_in):
    d = w_in.shape[0]
    c3 = 3 * CONV_DIM
    splits = (CONV_DIM, 2 * CONV_DIM, c3, c3 + NSA_DIM, c3 + NSA_DIM + KV_COLS, c3 + NSA_DIM + 2 * KV_COLS,
              c3 + NSA_DIM + 3 * KV_COLS)
    hc, gb, gc, q, kvc, kvs, kvw, gl = jnp.split(w_in, splits, axis=1)
    q4 = q.reshape(d, N_KV, Q_PER_KV, HEAD_DIM) * SCALE
    qz = jnp.einsum("dgrh,gk->dgrkh", q4, jnp.eye(N_KV, dtype=w_in.dtype)).reshape(d, QZ_DIM)
    glp = jnp.pad(gl, ((0, 0), (0, LANES - gl.shape[1])))
    return jnp.concatenate([hc, gb, gc, qz, kvc, kvs, kvw, glp], axis=1).astype(BF16)


def _prep_cmp(cmp_pe, cmp_w1, cmp_b1, cmp_w2):
    eye = jnp.eye(N_KV, dtype=cmp_w1.dtype)
    w1p = cmp_w1.reshape(2, 2, CMP_STRIDE // 2, 2, HEAD_DIM, CMP_HID)
    wpair = jnp.einsum("etpjdh,gk->epjgdtkh", w1p, eye).reshape(
        2, CMP_STRIDE // 2, 2 * N_KV * HEAD_DIM, 2 * N_KV * CMP_HID).astype(BF16)
    w2pair = jnp.einsum("ehd,gk->eghkd", cmp_w2, eye).reshape(2, N_KV * CMP_HID, N_KV * HEAD_DIM).astype(BF16)
    pe2 = cmp_pe.reshape(2, CMP_BLOCK * HEAD_DIM)
    w1f = cmp_w1.reshape(2, CMP_BLOCK * HEAD_DIM, CMP_HID).astype(BF16)
    return wpair, pe2, w1f, cmp_b1, w2pair


def _prep_post(conv_w, g_conv, g_nsa, w_o, ln1_g, ln1_b, ln2_g, ln2_b, w_gate, w_up, w_down):
    d = w_o.shape[1]
    own = jnp.eye(N_KV, dtype=w_o.dtype)
    g_nsa_z = jnp.einsum("grh,gk->grkh", g_nsa.reshape(N_KV, Q_PER_KV, HEAD_DIM), own).reshape(1, QZ_DIM)
    wo_n = jnp.einsum("grhd,gk->grkhd", w_o[CONV_DIM:].reshape(N_KV, Q_PER_KV, HEAD_DIM, d), own)
    row = lambda a: a.reshape(1, -1)
    return dict(conv_w=conv_w, g_conv=row(g_conv), g_nsa=g_nsa_z, wo_c=w_o[:CONV_DIM].astype(BF16),
                wo_n=wo_n.reshape(QZ_DIM, d).astype(BF16), ln1_g=row(ln1_g), ln1_b=row(ln1_b),
                ln2_g=row(ln2_g), ln2_b=row(ln2_b), w_gate=w_gate.astype(BF16), w_up=w_up.astype(BF16),
                w_down=w_down.astype(BF16))


def _feature_major(cache):
    n, rows = cache.shape[:2]
    return jnp.transpose(cache, (0, 2, 3, 4, 1)).reshape(n, KV_COLS, rows)


def kernel(x_prompt, x_sample, cache_cmp_kv, cache_slc_kv, cache_win_kv, state_conv, page_table, c_prompt,
           c_sample, w_ada, b_ada, w_in, conv_w, cmp_pe, cmp_w1, cmp_b1, cmp_w2, g_conv_out, g_nsa_out, w_o,
           ln1_g, ln1_b, ln2_g, ln2_b, w_ffn_gate, w_ffn_up, w_ffn_down):
    assert w_ada.shape[0] == DEPTH
    bp, s, d = x_prompt.shape
    bs, t, _ = x_sample.shape
    kv_shape = (2, N_KV, HEAD_DIM)

    w_cat = _prep_w_in(w_in[0])
    cmp_w = _prep_cmp(cmp_pe[0], cmp_w1[0], cmp_b1[0], cmp_w2[0])
    post_w = _prep_post(conv_w[0], g_conv_out[0], g_nsa_out[0], w_o[0], ln1_g[0], ln1_b[0], ln2_g[0], ln2_b[0],
                        w_ffn_gate[0], w_ffn_up[0], w_ffn_down[0])

    mod = _modulation(jnp.concatenate([c_prompt, c_sample], axis=0), w_ada[0], b_ada[0])
    mods_p = [m[:, None, :] for m in jnp.split(mod[:bp], 6, axis=-1)]
    mods_s = [jnp.repeat(m, t, axis=0)[None] for m in jnp.split(mod[bp:], 6, axis=-1)]

    u, gb, qz, kvc, kvs, kvw, kvs_b, kvw_b, gates = _in_proj(x_prompt, mods_p[1], mods_p[0], w_cat, IN_PROJ_ROWS, BF16)
    kv_cmpr = _compress_dense(kvc, *cmp_w)
    o = _attn_prompt(qz, kvs_b, kvw_b, kv_cmpr, gates)
    zero_state = jnp.zeros((bp, 1, CONV_DIM), F32)
    y_prompt = _post(x_prompt, o, u, gb, zero_state, zero_state, mods_p[2:], post_w, POST_ROWS, s)
    w_keep = min(WINDOW, s)
    cmp_p = kvc.reshape(1, bp, s, *kv_shape)
    slc_p = kvs.reshape(1, bp, s, *kv_shape)
    win_p = kvw[:, s - w_keep:].reshape(1, bp, w_keep, *kv_shape)
    conv_p = u[:, s - (CONV_W - 1):][None]

    rows = bs * t
    us, gbs, qzs, kvcs, kvss, kvws, _, _, gates_s = _in_proj(
        x_sample.reshape(1, rows, d), mods_s[1], mods_s[0], w_cat, rows, F32)
    per_b = lambda a: a.reshape(bs, t, a.shape[-1])
    kvcs, kvss, kvws = per_b(kvcs), per_b(kvss), per_b(kvws)
    xnew = jnp.pad(kvcs, ((0, 0), (0, (-t) % CMP_STRIDE), (0, 0))).reshape(bs, 1, CHUNK_COLS)
    kv_cmpr_s = _compress_paged(_feature_major(cache_cmp_kv[0]), page_table, xnew, *cmp_w)
    o_s = _attn_sample(per_b(qzs), kv_cmpr_s, per_b(gates_s), kvss, _feature_major(cache_slc_kv[0]),
                       _feature_major(cache_win_kv[0]), kvws, page_table)
    st = state_conv[0]
    tpos = jnp.arange(t)[None, :, None]
    st1 = jnp.where(tpos == 0, st[:, 1:2], 0.0).reshape(1, rows, CONV_DIM)
    st2 = jnp.where(tpos == 0, st[:, 0:1], jnp.where(tpos == 1, st[:, 1:2], 0.0)).reshape(1, rows, CONV_DIM)
    y_sample = _post(x_sample.reshape(1, rows, d), o_s.reshape(1, rows, QZ_DIM), us, gbs, st1, st2, mods_s[2:],
                     post_w, rows, t).reshape(bs, t, d)
    cmp_s = kvcs.reshape(1, bs, t, *kv_shape)
    slc_s = kvss.reshape(1, bs, t, *kv_shape)
    win_all = jnp.concatenate([cache_win_kv[0], kvws.reshape(bs, t, *kv_shape)], axis=1)
    win_s = win_all[:, t:][None]
    conv_s = jnp.concatenate([st, per_b(us)], axis=1)[:, t:][None]
    return (y_prompt, y_sample, cmp_p, slc_p, win_p, conv_p, cmp_s, slc_s, win_s, conv_s)
```

```python
import functools

import jax
import jax.numpy as jnp
from jax import lax
from jax.experimental import pallas as pl
from jax.experimental.pallas import tpu as pltpu

HEAD_DIM = 64
N_KV = 2
Q_PER_KV = 6
N_HEADS = N_KV * Q_PER_KV
CONV_DIM = 256
CONV_W = 3
NSA_DIM = N_HEADS * HEAD_DIM
KV_COLS = 2 * N_KV * HEAD_DIM
CMP_BLOCK = 32
CMP_STRIDE = 16
CMP_HID = 2 * HEAD_DIM
SEL_BLOCK = 64
CMP_PER_SEL = SEL_BLOCK // CMP_STRIDE
N_SELECT = 16
WINDOW = 512
Q_BLOCK = 128
DEPTH = 1
ALPHA = (2 * DEPTH) ** 0.25
EPS = 1e-5
SCALE = HEAD_DIM ** -0.5

LANES = 128
QZ_DIM = N_HEADS * LANES
CHUNK_COLS = CMP_STRIDE * KV_COLS
SEL_SHIFT = SEL_BLOCK.bit_length() - 1
HEAD_SHIFT = HEAD_DIM.bit_length() - 1
assert 1 << SEL_SHIFT == SEL_BLOCK and 1 << HEAD_SHIFT == HEAD_DIM
NEG = -1e30
VMEM_LIMIT = 56 * 1024 * 1024
MOD_COL_TILE = 512
IN_PROJ_ROWS = 512
POST_ROWS = 512

F32 = jnp.float32
BF16 = jnp.bfloat16


def _cparams(n_grid):
    return pltpu.CompilerParams(dimension_semantics=("arbitrary",) * n_grid, vmem_limit_bytes=VMEM_LIMIT)


def _const_spec(shape):
    nd = len(shape)
    return pl.BlockSpec(shape, lambda *_: (0,) * nd, pipeline_mode=pl.Buffered(1))


def _dot(a, b):
    return jnp.dot(a, b, preferred_element_type=F32)


def _dot_nt(a, b):
    return lax.dot_general(a, b, (((1,), (1,)), ((), ())), preferred_element_type=F32)


def _iota(shape, dim):
    return lax.broadcasted_iota(jnp.int32, shape, dim)


def _lane_tiles(x):
    return [x[:, w * LANES:(w + 1) * LANES] for w in range(x.shape[1] // LANES)]


def _mod_kernel(c_ref, w_ref, b_ref, o_ref):
    c = c_ref[...]
    a = (c * jax.nn.sigmoid(c)).astype(BF16)
    o_ref[...] = _dot(a, w_ref[...].astype(BF16)) + b_ref[...]


def _modulation(c_all, w_ada, b_ada):
    m, d = c_all.shape
    n = w_ada.shape[1]
    tn = MOD_COL_TILE
    return pl.pallas_call(
        _mod_kernel,
        grid=(n // tn,),
        in_specs=[pl.BlockSpec((m, d), lambda j: (0, 0)),
                  pl.BlockSpec((d, tn), lambda j: (0, j)),
                  pl.BlockSpec((1, tn), lambda j: (0, j))],
        out_specs=pl.BlockSpec((m, tn), lambda j: (0, j)),
        out_shape=jax.ShapeDtypeStruct((m, n), F32),
        compiler_params=_cparams(1),
        name="modulation",
    )(c_all, w_ada, b_ada.reshape(1, n))


_C_HC, _C_GB, _C_GC, _C_Q = 0, CONV_DIM, 2 * CONV_DIM, 3 * CONV_DIM
_C_KVC = _C_Q + QZ_DIM
_C_KVS = _C_KVC + KV_COLS
_C_KVW = _C_KVS + KV_COLS
_C_GL = _C_KVW + KV_COLS
W_IN_COLS = _C_GL + LANES


def _inproj_kernel(x_ref, sc_ref, sh_ref, w_ref, u_ref, gb_ref, qz_ref, kvc_ref, kvs_ref, kvw_ref,
                   kvsb_ref, kvwb_ref, gates_ref):
    h = (x_ref[0] * (1.0 + sc_ref[0]) + sh_ref[0]).astype(BF16)
    hc = _dot(h, w_ref[:, _C_HC:_C_GB])
    gb_ref[0] = _dot(h, w_ref[:, _C_GB:_C_GC])
    gc = _dot(h, w_ref[:, _C_GC:_C_Q])
    u_ref[0] = gc * hc
    qz_ref[0] = _dot(h, w_ref[:, _C_Q:_C_KVC]).astype(qz_ref.dtype)
    kvc_ref[0] = _dot(h, w_ref[:, _C_KVC:_C_KVS])
    kvs = _dot(h, w_ref[:, _C_KVS:_C_KVW])
    kvs_ref[0] = kvs
    kvsb_ref[0] = kvs.astype(BF16)
    kvw = _dot(h, w_ref[:, _C_KVW:_C_GL])
    kvw_ref[0] = kvw
    kvwb_ref[0] = kvw.astype(BF16)
    gates_ref[0] = jax.nn.sigmoid(_dot(h, w_ref[:, _C_GL:W_IN_COLS]))


def _in_proj(x, scale, shift, w_cat, tm, qz_dtype):
    b, t, d = x.shape
    rm = scale.shape[1]
    mod_block = (1, tm, d) if rm == t else (1, 1, d)
    mod_map = (lambda i, j: (i, j, 0)) if rm == t else (lambda i, j: (i, 0, 0))
    row = lambda cols: pl.BlockSpec((1, tm, cols), lambda i, j: (i, j, 0))
    sds = lambda cols, dt: jax.ShapeDtypeStruct((b, t, cols), dt)
    return pl.pallas_call(
        _inproj_kernel,
        grid=(b, t // tm),
        in_specs=[row(d), pl.BlockSpec(mod_block, mod_map), pl.BlockSpec(mod_block, mod_map),
                  _const_spec((d, W_IN_COLS))],
        out_specs=[row(CONV_DIM), row(CONV_DIM), row(QZ_DIM), row(KV_COLS), row(KV_COLS), row(KV_COLS),
                   row(KV_COLS), row(KV_COLS), row(LANES)],
        out_shape=[sds(CONV_DIM, F32), sds(CONV_DIM, F32), sds(QZ_DIM, qz_dtype), sds(KV_COLS, F32),
                   sds(KV_COLS, F32), sds(KV_COLS, F32), sds(KV_COLS, BF16), sds(KV_COLS, BF16),
                   sds(LANES, F32)],
        compiler_params=_cparams(2),
        name="in_proj",
    )(x, scale, shift, w_cat)


def _gelu_tanh(x):
    return jax.nn.gelu(x, approximate=True)


def _cmp_bias_e(pe_ref, w1f_ref, b1_ref, e):
    pe = jnp.broadcast_to(pe_ref[e:e + 1, :], (8, pe_ref.shape[1])).astype(BF16)
    return _dot(pe, w1f_ref[e])[0:1, :] + b1_ref[e:e + 1, :]


def _chunk_partials(row_pairs, wpair_ref, e):
    acc = None
    for jp in range(CMP_STRIDE // 2):
        part = _dot(row_pairs(jp), wpair_ref[e, jp])
        acc = part if acc is None else acc + part
    return acc


def _compress_mlp(top, bot_next, pe_ref, w1f_ref, b1_ref, w2_ref, e):
    bias = jnp.concatenate([_cmp_bias_e(pe_ref, w1f_ref, b1_ref, e)] * N_KV, axis=1)
    return _dot(_gelu_tanh(top + bot_next + bias).astype(BF16), w2_ref[e])


def _compress_dense_kernel(x_ref, wpair_ref, pe_ref, w1f_ref, b1_ref, w2_ref, o_ref, r_sc):
    n = o_ref.shape[1]
    half_cols = N_KV * HEAD_DIM
    hid_cols = N_KV * CMP_HID
    outs = []
    for e in range(2):
        r_sc[e] = x_ref[0, :, e * half_cols:(e + 1) * half_cols]
        acc = _chunk_partials(
            lambda jp: jnp.concatenate([r_sc[e, pl.ds(2 * jp + jj, n, stride=CMP_STRIDE), :] for jj in (0, 1)],
                                       axis=1).astype(BF16), wpair_ref, e)
        bot_next = pltpu.roll(acc[:, hid_cols:], n - 1, 0)
        outs.append(_compress_mlp(acc[:, :hid_cols], bot_next, pe_ref, w1f_ref, b1_ref, w2_ref, e))
    o_ref[0] = jnp.concatenate(outs, axis=1).astype(o_ref.dtype)


def _compress_dense(kvc, wpair, pe2, w1f, b1, w2pair):
    b, s, _ = kvc.shape
    nch = s // CMP_STRIDE
    return pl.pallas_call(
        _compress_dense_kernel,
        grid=(b,),
        in_specs=[pl.BlockSpec((1, s, KV_COLS), lambda i: (i, 0, 0)),
                  _const_spec(wpair.shape), _const_spec(pe2.shape), _const_spec(w1f.shape),
                  _const_spec(b1.shape), _const_spec(w2pair.shape)],
        out_specs=pl.BlockSpec((1, nch, KV_COLS), lambda i: (i, 0, 0)),
        out_shape=jax.ShapeDtypeStruct((b, nch, KV_COLS), BF16),
        scratch_shapes=[pltpu.VMEM((2, s, LANES), F32)],
        compiler_params=_cparams(1),
        name="compress_prompt",
    )(kvc, wpair, pe2, w1f, b1, w2pair)


PAGES_PER_STEP = 32
CHUNK_PITCH = 24


def _page_copy(pool_ref, page_index, buf, sem, slot, p):
    return pltpu.make_async_copy(pool_ref.at[page_index], buf.at[slot, p], sem.at[slot, p])


def _await_pages(pt_ref, pool_ref, buf, sem, n_steps):
    b, kt = pl.program_id(0), pl.program_id(1)

    @pl.when((b == 0) & (kt == 0))
    def _():
        for k in range(n_steps):
            for p in range(PAGES_PER_STEP):
                _page_copy(pool_ref, pt_ref[0, k * PAGES_PER_STEP + p], buf, sem, k, p).start(priority=p % 2)

    for p in range(PAGES_PER_STEP):
        _page_copy(pool_ref, 0, buf, sem, kt, p).wait()
    return kt


def _refill_pages(pt_ref, pool_ref, buf, sem):
    kt = pl.program_id(1)
    nb = jnp.minimum(pl.program_id(0) + 1, pl.num_programs(0) - 1)
    for p in range(PAGES_PER_STEP):
        _page_copy(pool_ref, pt_ref[nb, kt * PAGES_PER_STEP + p], buf, sem, kt, p).start(priority=p % 2)


def _drain_pages(pool_ref, buf, sem, n_steps):
    @pl.when((pl.program_id(0) == pl.num_programs(0) - 1) & (pl.program_id(1) == n_steps - 1))
    def _():
        for k in range(n_steps):
            for p in range(PAGES_PER_STEP):
                _page_copy(pool_ref, 0, buf, sem, k, p).wait()


def _compress_paged_kernel(pt_ref, pool_ref, *refs, n_steps):
    g_pages = PAGES_PER_STEP
    (xnew_ref, wpair_ref, pe_ref, w1f_ref, b1_ref, w2_ref, o_ref, r_sc, tb_sc, buf, sem) = refs
    kt = pl.program_id(1)
    page = buf.shape[3]
    slot = _await_pages(pt_ref, pool_ref, buf, sem, n_steps)
    cpp = page // CMP_STRIDE
    rows_step = g_pages * cpp
    half_cols = N_KV * HEAD_DIM
    hid_cols = N_KV * CMP_HID

    row0 = pl.multiple_of(kt * rows_step, rows_step)
    for e in range(2):
        for p in range(g_pages):
            rows_t = buf[slot, p, e * half_cols:(e + 1) * half_cols, :].T
            for c in range(cpp):
                r_sc[e, pl.ds((p * cpp + c) * CHUNK_PITCH, CMP_STRIDE), :] = rows_t[c * CMP_STRIDE:(c + 1) * CMP_STRIDE]
        tb_sc[e, pl.ds(row0, rows_step), :] = _chunk_partials(
            lambda jp: jnp.concatenate([r_sc[e, pl.ds(2 * jp + jj, rows_step, stride=CHUNK_PITCH), :]
                                        for jj in (0, 1)], axis=1).astype(BF16), wpair_ref, e)
    _refill_pages(pt_ref, pool_ref, buf, sem)

    @pl.when(kt == n_steps - 1)
    def _():
        n = tb_sc.shape[1]
        xn = jnp.broadcast_to(xnew_ref[0], (8, CHUNK_COLS)).astype(BF16)
        last = _iota((n, 1), 0) == n - 1
        outs = []
        for e in range(2):
            new_pair = lambda jp: jnp.concatenate(
                [xn[:, (2 * jp + jj) * KV_COLS + e * half_cols:(2 * jp + jj) * KV_COLS + (e + 1) * half_cols]
                 for jj in (0, 1)], axis=1)
            bot_new = _chunk_partials(new_pair, wpair_ref, e)[0:1, hid_cols:]
            top, bot = tb_sc[e, :, :hid_cols], tb_sc[e, :, hid_cols:]
            bot_next = jnp.where(last, bot_new, pltpu.roll(bot, n - 1, 0))
            outs.append(_compress_mlp(top, bot_next, pe_ref, w1f_ref, b1_ref, w2_ref, e))
        o_ref[0] = jnp.concatenate(outs, axis=1).astype(o_ref.dtype)

    _drain_pages(pool_ref, buf, sem, n_steps)


def _page_stream_scratch(page, n_steps):
    return [pltpu.VMEM((n_steps, PAGES_PER_STEP, KV_COLS, page), F32),
            pltpu.SemaphoreType.DMA((n_steps, PAGES_PER_STEP))]


def _compress_paged(pool_t, page_table, xnew, wpair, pe2, w1f, b1, w2pair):
    bn, n_pages = page_table.shape
    page = pool_t.shape[2]
    g = PAGES_PER_STEP
    n_steps = n_pages // g
    assert n_pages % g == 0 and page % CMP_STRIDE == 0 and page == LANES
    n = n_pages * page // CMP_STRIDE

    def const(shape):
        nd = len(shape)
        return pl.BlockSpec(shape, lambda i, j, pt: (0,) * nd, pipeline_mode=pl.Buffered(1))

    grid_spec = pltpu.PrefetchScalarGridSpec(
        num_scalar_prefetch=1,
        grid=(bn, n_steps),
        in_specs=[pl.BlockSpec(memory_space=pl.ANY),
                  pl.BlockSpec((1, 1, CHUNK_COLS), lambda i, j, pt: (i, 0, 0)), const(wpair.shape),
                  const(pe2.shape), const(w1f.shape), const(b1.shape), const(w2pair.shape)],
        out_specs=pl.BlockSpec((1, n, KV_COLS), lambda i, j, pt: (i, 0, 0)),
        scratch_shapes=[pltpu.VMEM((2, g * (page // CMP_STRIDE) * CHUNK_PITCH, LANES), F32),
                        pltpu.VMEM((2, n, 2 * N_KV * CMP_HID), F32)] + _page_stream_scratch(page, n_steps),
    )
    return pl.pallas_call(
        functools.partial(_compress_paged_kernel, n_steps=n_steps),
        grid_spec=grid_spec,
        out_shape=jax.ShapeDtypeStruct((bn, n, KV_COLS), BF16),
        compiler_params=_cparams(2),
        name="compress_sample",
    )(page_table, pool_t, xnew, wpair, pe2, w1f, b1, w2pair)


def _split3_dot(x, m01):
    h1 = x.astype(BF16)
    r1 = x - h1.astype(F32)
    h2 = r1.astype(BF16)
    h3 = (r1 - h2.astype(F32)).astype(BF16)
    return _dot(h1, m01) + _dot(h2, m01) + _dot(h3, m01)


def _block_scores(imp, n_lanes):
    ncp = imp.shape[1]
    c = _iota((ncp, n_lanes), 0)
    j = _iota((ncp, n_lanes), 1)
    a = jnp.where((c >= CMP_PER_SEL * j - 1) & (c <= CMP_PER_SEL * j + CMP_PER_SEL - 1), 1.0, 0.0).astype(BF16)
    return _split3_dot(imp, a)


def _ranked_scores(score, qpos, n_blocks):
    j = _iota(score.shape, 1)
    cur = qpos >> SEL_SHIFT
    valid = j * SEL_BLOCK <= qpos
    forced = (j == 0) | (j == cur) | (j == cur - 1)
    return jnp.where(forced, -NEG, jnp.where(valid, score, NEG))


def _select_blocks_pairwise(score, qpos, n_blocks, n_select):
    r, l = score.shape
    nb8 = -(-n_blocks // 8) * 8
    ranked = _ranked_scores(score, qpos, n_blocks)
    ranked_t = jnp.concatenate([ranked, jnp.zeros((LANES - r, l), F32)], axis=0).T
    lower_index = jnp.where(_iota((nb8, l), 0) < _iota((nb8, l), 1), 1.0, 0.0)
    in_range = _iota((1, l), 1) < n_blocks
    rows = []
    for v in range(r):
        col = jnp.broadcast_to(ranked_t[:nb8, v:v + 1], (nb8, l))
        row = ranked[v:v + 1, :]
        beats = jnp.where(col > row, 1.0, jnp.where(col == row, lower_index, 0.0))
        rank = jnp.sum(beats, axis=0, keepdims=True)
        rows.append(jnp.where((rank < n_select) & in_range, 1.0, 0.0))
    return jnp.concatenate(rows, axis=0)


def _select_blocks_rank(score, qpos, n_blocks, n_select):
    r, l = score.shape
    nb8 = -(-n_blocks // 8) * 8
    st = _ranked_scores(score, qpos, n_blocks).T[:nb8]
    jrow = _iota((nb8, r), 0)
    rank = jnp.zeros((nb8, r), F32)
    for jp in range(n_blocks):
        row = st[jp:jp + 1, :]
        beats = (row > st) | ((row == st) & (jrow > jp))
        rank = rank + jnp.where(beats, 1.0, 0.0)
    sel_t = jnp.where((rank < n_select) & (jrow < n_blocks), 1.0, 0.0)
    if nb8 < l:
        sel_t = jnp.concatenate([sel_t, jnp.zeros((l - nb8, r), F32)], axis=0)
    return sel_t.T


def _softmax_parts(s, mask):
    s = jnp.where(mask, s, NEG)
    m = jnp.max(s, axis=-1, keepdims=True)
    e = jnp.where(mask, jnp.exp(s - m), 0.0)
    return e, jnp.sum(e, axis=-1, keepdims=True)


SEL_TILE = 512


def _attn_prompt_kernel(qz_ref, kvs_ref, kvw_ref, kvc_ref, gates_ref, exp_ref, o_ref,
                        s_sc, p_sc, pw_sc, m_sc, acc_sc, ocw_sc, *, seq):
    qb = Q_BLOCK
    tk = SEL_TILE
    start = pl.program_id(1) * qb
    qpos = start + _iota((qb, 1), 0)
    ncp = kvc_ref.shape[1]
    n_blocks = -(-seq // SEL_BLOCK)
    rows = Q_PER_KV * qb
    gates = gates_ref[0]
    n_tiles = (start + qb + tk - 1) // tk
    head_rows = [slice(r * qb, (r + 1) * qb) for r in range(Q_PER_KV)]
    kv_heads = range(N_KV)

    def q_rows(g):
        return jnp.concatenate(
            [qz_ref[0, :, (g * Q_PER_KV + r) * LANES:(g * Q_PER_KV + r + 1) * LANES] for r in range(Q_PER_KV)],
            axis=0)


    def values_and_ones(g, vv):
        return jnp.where((_iota((1, LANES), 1) >> HEAD_SHIFT) == g, vv, jnp.ones_like(vv))

    def normalised(acc):
        return acc / jnp.maximum(pltpu.roll(acc, LANES // 2, 1), 1e-30)

    o_cmp, sel = [], []
    c = _iota((qb, ncp), 1)
    cmask = (CMP_STRIDE * c + (CMP_BLOCK - 1) <= qpos) & (c < ncp - 1)
    for g in kv_heads:
        s = _dot_nt(q_rows(g), kvc_ref[0, :, 0:LANES]).reshape(Q_PER_KV, qb, ncp)
        e, l = _softmax_parts(s, cmask)
        p = e / jnp.maximum(l, 1e-30)
        o_cmp.append(_dot(p.reshape(rows, ncp).astype(BF16), kvc_ref[0, :, LANES:2 * LANES]))
        imp = jnp.sum(p, axis=0)
        sel.append(_select_blocks_rank(_block_scores(imp, LANES), qpos, n_blocks,
                                       min(N_SELECT, n_blocks)).astype(BF16))

    wlen = WINDOW + qb
    w0 = pl.multiple_of(jnp.maximum(start - WINDOW, 0), qb)
    kpos = w0 + _iota((1, wlen), 1)
    wmask = (kpos <= qpos) & (kpos > qpos - WINDOW)
    for g in kv_heads:
        s = _dot_nt(q_rows(g), kvw_ref[0, pl.ds(w0, wlen), 0:LANES])
        for rs in head_rows:
            sr = jnp.where(wmask, s[rs], NEG)
            m = jnp.max(functools.reduce(jnp.maximum, _lane_tiles(sr)), axis=-1, keepdims=True)
            pw_sc[g, rs, :] = jnp.exp((sr - m).astype(BF16))
        o_win = normalised(_dot(pw_sc[g], values_and_ones(g, kvw_ref[0, pl.ds(w0, wlen), LANES:2 * LANES])))
        for r, rs in enumerate(head_rows):
            h = g * Q_PER_KV + r
            ocw_sc[g, rs, :] = (gates[:, h:h + 1] * o_cmp[g][rs]
                                + gates[:, 2 * N_HEADS + h:2 * N_HEADS + h + 1] * o_win[rs])

    m_sc[...] = jnp.full(m_sc.shape, NEG, F32)

    def pass1(kt, _):
        k0 = pl.multiple_of(kt * tk, tk)
        causal = (k0 + _iota((1, tk), 1)) <= qpos
        for g in kv_heads:
            s = _dot_nt(q_rows(g), kvs_ref[0, pl.ds(k0, tk), 0:LANES])
            mask = (_dot(sel[g], exp_ref[kt]) > 0.5) & causal
            for rs in head_rows:
                sr = jnp.where(mask, s[rs], NEG)
                s_sc[g, kt, rs, :] = sr
                m_sc[g, rs, :] = jnp.maximum(m_sc[g, rs, :], functools.reduce(jnp.maximum, _lane_tiles(sr)))
        return 0

    lax.fori_loop(0, n_tiles, pass1, 0)
    m_sc[...] = jnp.broadcast_to(jnp.max(m_sc[...], axis=-1, keepdims=True), m_sc.shape)
    acc_sc[...] = jnp.zeros(acc_sc.shape, F32)

    def pass2(kt, _):
        k0 = pl.multiple_of(kt * tk, tk)
        for g in kv_heads:
            for rs in head_rows:
                x = s_sc[g, kt, rs, :] - jnp.concatenate([m_sc[g, rs, :]] * (tk // LANES), axis=1)
                p_sc[g, rs, :] = jnp.exp(x.astype(BF16))
            vv = values_and_ones(g, kvs_ref[0, pl.ds(k0, tk), LANES:2 * LANES])
            acc_sc[g] = acc_sc[g] + _dot(p_sc[g], vv)
        return 0

    lax.fori_loop(0, n_tiles, pass2, 0)

    o_slc = [normalised(acc_sc[g]) for g in kv_heads]
    lane_head = _iota((qb, LANES), 1) >> HEAD_SHIFT
    for r, rs in enumerate(head_rows):
        o = None
        for g in kv_heads:
            h = g * Q_PER_KV + r
            o_g = ocw_sc[g, rs, :] + gates[:, N_HEADS + h:N_HEADS + h + 1] * o_slc[g][rs]
            o = o_g if o is None else jnp.where(lane_head == g, o_g, o)
        o_ref[0, :, r * LANES:(r + 1) * LANES] = o


def _expand_matrix(n_tiles, tile):
    t = jnp.arange(n_tiles)[:, None, None]
    j = jnp.arange(LANES)[None, :, None]
    k = jnp.arange(tile)[None, None, :]
    return (j == (t * tile + k) // SEL_BLOCK).astype(BF16)


def _attn_prompt(qz, kvs_b, kvw_b, kv_cmpr, gates):
    b, s, _ = qz.shape
    assert s % Q_BLOCK == 0 and s % SEL_TILE == 0 and s >= WINDOW + Q_BLOCK and s // SEL_BLOCK <= LANES
    ncp = kv_cmpr.shape[1]
    n_tiles = s // SEL_TILE
    expand = _expand_matrix(n_tiles, SEL_TILE)
    rows = Q_PER_KV * Q_BLOCK
    full = lambda n: pl.BlockSpec((1, n, KV_COLS), lambda i, j: (i, 0, 0))
    return pl.pallas_call(
        functools.partial(_attn_prompt_kernel, seq=s),
        grid=(b, s // Q_BLOCK),
        in_specs=[pl.BlockSpec((1, Q_BLOCK, QZ_DIM), lambda i, j: (i, j, 0)), full(s), full(s), full(ncp),
                  pl.BlockSpec((1, Q_BLOCK, LANES), lambda i, j: (i, j, 0)), _const_spec(expand.shape)],
        out_specs=pl.BlockSpec((1, Q_BLOCK, NSA_DIM), lambda i, j: (i, j, 0)),
        out_shape=jax.ShapeDtypeStruct((b, s, NSA_DIM), F32),
        scratch_shapes=[pltpu.VMEM((N_KV, n_tiles, rows, SEL_TILE), F32), pltpu.VMEM((N_KV, rows, SEL_TILE), BF16),
                        pltpu.VMEM((N_KV, rows, WINDOW + Q_BLOCK), BF16), pltpu.VMEM((N_KV, rows, LANES), F32),
                        pltpu.VMEM((N_KV, rows, LANES), F32), pltpu.VMEM((N_KV, rows, LANES), F32)],
        compiler_params=_cparams(2),
        name="attn_prompt",
    )(qz, kvs_b, kvw_b, kv_cmpr, gates, expand)


def _attn_sample_kernel(pt_ref, pool_ref, *refs, past, n_steps):
    g_pages = PAGES_PER_STEP
    (qz_ref, kvc_ref, gates_ref, kvs_new_ref, win_ref, kvw_new_ref, exp_ref,
     o_ref, q_sc, sel_sc, s_sc, v_sc, snew_sc, mp_sc, ocw_sc, buf, sem) = refs
    kt = pl.program_id(1)
    t = qz_ref.shape[1]
    rows = N_HEADS * t
    page = buf.shape[3]
    tile = g_pages * page
    blocks_per_tile = tile // SEL_BLOCK
    n_past_blocks = past // SEL_BLOCK
    n_blocks = n_past_blocks + -(-t // SEL_BLOCK)
    sel_lanes = sel_sc.shape[0] * LANES
    half_cols = N_KV * HEAD_DIM
    qpos = past + _iota((t, 1), 0)
    qpos_gt = jnp.concatenate([qpos] * N_KV, axis=0)
    own_rows = _iota((N_KV, Q_PER_KV, t, LANES), 0) == (_iota((N_KV, Q_PER_KV, t, LANES), 3) >> HEAD_SHIFT)

    def new_rows_padded():
        kvn = kvs_new_ref[0]
        return jnp.concatenate([kvn, jnp.zeros((LANES - t, KV_COLS), F32)], axis=0).astype(BF16)

    def masked_scores(s, mask):
        k = s.shape[1]
        return jnp.where(mask.reshape(N_KV, 1, t, k), s.reshape(N_KV, Q_PER_KV, t, k), NEG).reshape(rows, k)

    @pl.when(kt == 0)
    def _():
        q = jnp.concatenate([qz_ref[0, :, h * LANES:(h + 1) * LANES] for h in range(N_HEADS)], axis=0)
        q_sc[...] = q.astype(BF16)
        qb = q_sc[...]
        ncp = kvc_ref.shape[1]
        s = _dot_nt(qb, kvc_ref[0, :, 0:LANES]).reshape(N_HEADS, t, ncp)
        c = _iota((t, ncp), 1)
        cmask = CMP_STRIDE * c + (CMP_BLOCK - 1) <= qpos
        e, l = _softmax_parts(s, cmask)
        p = e / jnp.maximum(l, 1e-30)
        o_cmp = _dot(p.reshape(rows, ncp).astype(BF16), kvc_ref[0, :, LANES:2 * LANES])
        imp = jnp.sum(p.reshape(N_KV, Q_PER_KV, t, ncp), axis=1).reshape(N_KV * t, ncp)
        sel = _select_blocks_pairwise(_block_scores(imp, sel_lanes), qpos_gt, n_blocks, min(N_SELECT, n_blocks))
        for w in range(sel_sc.shape[0]):
            sel_sc[w] = sel[:, w * LANES:(w + 1) * LANES]
        wb = win_ref.shape[2]
        win = win_ref[0].astype(BF16)
        kvw_new = kvw_new_ref[0].astype(BF16)
        s_old = _dot(qb, win[0:half_cols]).reshape(N_HEADS, t, wb)
        s_new = _dot_nt(qb, kvw_new[:, 0:half_cols]).reshape(N_HEADS, t, t)
        kpos_old = past - wb + _iota((1, wb), 1)
        kpos_new = past + _iota((1, t), 1)
        mask_old = (kpos_old <= qpos) & (kpos_old > qpos - WINDOW) & (kpos_old >= 0)
        mask_new = (kpos_new <= qpos) & (kpos_new > qpos - WINDOW)
        s_old = jnp.where(mask_old, s_old, NEG)
        s_new = jnp.where(mask_new, s_new, NEG)
        m = jnp.maximum(jnp.max(s_old, axis=-1, keepdims=True), jnp.max(s_new, axis=-1, keepdims=True))
        e_old = jnp.where(mask_old, jnp.exp(s_old - m), 0.0)
        e_new = jnp.where(mask_new, jnp.exp(s_new - m), 0.0)
        l = jnp.sum(e_old, axis=-1, keepdims=True) + jnp.sum(e_new, axis=-1, keepdims=True)
        o_win = (_dot_nt(e_old.reshape(rows, wb).astype(BF16), win[half_cols:])
                 + _dot(e_new.reshape(rows, t).astype(BF16), kvw_new[:, half_cols:]))
        o_win = o_win.reshape(N_HEADS, t, LANES) / jnp.maximum(l, 1e-30)
        gts = gates_ref[0]
        gc = jnp.stack([jnp.broadcast_to(gts[:, h:h + 1], (t, LANES)) for h in range(N_HEADS)])
        gw = jnp.stack([jnp.broadcast_to(gts[:, 2 * N_HEADS + h:2 * N_HEADS + h + 1], (t, LANES))
                        for h in range(N_HEADS)])
        ocw_sc[...] = gc * o_cmp.reshape(N_HEADS, t, LANES) + gw * o_win
        jn = (past + _iota((1, LANES), 1)) >> SEL_SHIFT
        seln = jnp.zeros((N_KV * t, LANES), F32)
        for jb in range(n_past_blocks, n_blocks):
            seln = jnp.where(jn == jb, sel[:, jb:jb + 1], seln)
        npos = past + _iota((1, LANES), 1)
        nmask = (seln > 0.5) & (npos <= qpos_gt) & (npos < past + t)
        s_n = masked_scores(_dot_nt(qb, new_rows_padded()[:, 0:half_cols]), nmask)
        snew_sc[...] = s_n
        mp_sc[...] = s_n

    slot = _await_pages(pt_ref, pool_ref, buf, sem, n_steps)
    k_t = jnp.concatenate([buf[slot, p, 0:half_cols, :] for p in range(g_pages)], axis=1).astype(BF16)
    v_sc[kt] = jnp.concatenate([buf[slot, p, half_cols:, :] for p in range(g_pages)], axis=1).astype(BF16)
    b0 = kt * blocks_per_tile
    selw = sel_sc[b0 // LANES].astype(BF16)
    kpos = kt * tile + _iota((1, tile), 1)
    mask = (_dot(selw, exp_ref[0]) > 0.5) & (kpos <= qpos_gt)
    s = masked_scores(_dot(q_sc[...], k_t), mask)
    s_sc[kt] = s
    mp_sc[...] = jnp.maximum(mp_sc[...], functools.reduce(jnp.maximum, _lane_tiles(s)))
    _refill_pages(pt_ref, pool_ref, buf, sem)

    @pl.when(kt == n_steps - 1)
    def _():
        m = jnp.max(mp_sc[...], axis=-1, keepdims=True)
        e = jnp.exp(snew_sc[...] - m)
        lp = e
        acc = _dot(e.astype(BF16), new_rows_padded()[:, half_cols:])
        for k2 in range(n_steps):
            e = jnp.exp(s_sc[k2] - m)
            lp = lp + functools.reduce(jnp.add, _lane_tiles(e))
            acc = acc + _dot_nt(e.astype(BF16), v_sc[k2])
        l = jnp.sum(lp, axis=-1, keepdims=True)
        gts = gates_ref[0]
        gs = jnp.stack([jnp.broadcast_to(gts[:, N_HEADS + h:N_HEADS + h + 1], (t, LANES))
                        for h in range(N_HEADS)])
        o_slc = (acc / jnp.maximum(l, 1e-30)).reshape(N_HEADS, t, LANES)
        o = jnp.where(own_rows.reshape(N_HEADS, t, LANES), ocw_sc[...] + gs * o_slc, 0.0)
        for r in range(Q_PER_KV):
            o_ref[0, :, r * LANES:(r + 1) * LANES] = functools.reduce(
                jnp.add, [o[g * Q_PER_KV + r] for g in range(N_KV)])

    _drain_pages(pool_ref, buf, sem, n_steps)


def _attn_sample(qz, kv_cmpr, gates, kvs_new, slc_pool_t, win_t, kvw_new, page_table):
    bn, t, _ = qz.shape
    n_pages = page_table.shape[1]
    page = slc_pool_t.shape[2]
    past = n_pages * page
    g = PAGES_PER_STEP
    n_steps = n_pages // g
    tile = g * page
    blocks_per_tile = tile // SEL_BLOCK
    assert n_pages % g == 0 and page % SEL_BLOCK == 0 and LANES % blocks_per_tile == 0 and t % 8 == 0
    n_blocks = past // SEL_BLOCK + -(-t // SEL_BLOCK)
    sel_groups = -(-n_blocks // LANES)
    ncp = kv_cmpr.shape[1]
    wb = win_t.shape[2]
    n_var = LANES // blocks_per_tile
    expand = _expand_matrix(n_var, tile)

    per_b = lambda n, cols: pl.BlockSpec((1, n, cols), lambda i, j, pt: (i, 0, 0))
    grid_spec = pltpu.PrefetchScalarGridSpec(
        num_scalar_prefetch=1,
        grid=(bn, n_steps),
        in_specs=[pl.BlockSpec(memory_space=pl.ANY),
                  per_b(t, QZ_DIM), per_b(ncp, KV_COLS), per_b(t, LANES), per_b(t, KV_COLS), per_b(KV_COLS, wb),
                  per_b(t, KV_COLS), pl.BlockSpec((1, LANES, tile), lambda i, j, pt: (j % n_var, 0, 0))],
        out_specs=per_b(t, NSA_DIM),
        scratch_shapes=[pltpu.VMEM((N_HEADS * t, LANES), BF16),
                        pltpu.VMEM((sel_groups, N_KV * t, LANES), F32),
                        pltpu.VMEM((n_steps, N_HEADS * t, tile), F32), pltpu.VMEM((n_steps, LANES, tile), BF16),
                        pltpu.VMEM((N_HEADS * t, LANES), F32), pltpu.VMEM((N_HEADS * t, LANES), F32),
                        pltpu.VMEM((N_HEADS, t, LANES), F32)] + _page_stream_scratch(page, n_steps),
    )
    return pl.pallas_call(
        functools.partial(_attn_sample_kernel, past=past, n_steps=n_steps),
        grid_spec=grid_spec,
        out_shape=jax.ShapeDtypeStruct((bn, t, NSA_DIM), F32),
        compiler_params=_cparams(2),
        name="attn_sample",
    )(page_table, slc_pool_t, qz, kv_cmpr, gates, kvs_new, win_t, kvw_new, expand)


def _layernorm(x, g, b):
    mu = jnp.mean(x, axis=-1, keepdims=True)
    xc = x - mu
    var = jnp.mean(xc * xc, axis=-1, keepdims=True)
    return xc * lax.rsqrt(var + EPS) * g + b


def _post_kernel(x_ref, o_ref, u_ref, uprev_ref, gb_ref, st1_ref, st2_ref, g1_ref, sh2_ref, sc2_ref, g2_ref,
                 convw_ref, gconv_ref, gnsa_ref, woc_ref, won_ref, ln1g_ref, ln1b_ref, ln2g_ref, ln2b_ref,
                 wg_ref, wu_ref, wd_ref, y_ref, *, seq):
    tm = x_ref.shape[1]
    u = u_ref[0]
    ext = jnp.concatenate([uprev_ref[0], u], axis=0)
    if tm <= seq:
        pos = (pl.program_id(1) * tm) % seq + _iota((tm, 1), 0)
    else:
        pos = lax.rem(_iota((tm, 1), 0), seq)
    p1 = jnp.where(pos >= 1, ext[7:7 + tm], st1_ref[0])
    p2 = jnp.where(pos >= 2, ext[6:6 + tm], st2_ref[0])
    cw = convw_ref[...]
    y_c = gb_ref[0] * (cw[0:1] * p2 + cw[1:2] * p1 + cw[2:3] * u)
    yn = y_c * lax.rsqrt(jnp.mean(y_c * y_c, axis=-1, keepdims=True) + EPS) * gconv_ref[...]
    o = o_ref[0]
    on = o * lax.rsqrt(jnp.sum(o * o, axis=-1, keepdims=True) * (1.0 / NSA_DIM) + EPS) * gnsa_ref[...]
    mix = _dot(yn.astype(BF16), woc_ref[...]) + _dot(on.astype(BF16), won_ref[...])
    x1 = _layernorm(ALPHA * x_ref[0] + g1_ref[0] * mix, ln1g_ref[...], ln1b_ref[...])
    h = (x1 * (1.0 + sc2_ref[0]) + sh2_ref[0]).astype(BF16)
    a = _dot(h, wg_ref[...])
    f = (a * jax.nn.sigmoid(a)) * _dot(h, wu_ref[...])
    f = _dot(f.astype(BF16), wd_ref[...])
    y_ref[0] = _layernorm(ALPHA * x1 + g2_ref[0] * f, ln2g_ref[...], ln2b_ref[...])


def _post(x, o, u, gb, st1, st2, mods, w, tm, seq):
    b, t, d = x.shape
    row = lambda cols: pl.BlockSpec((1, tm, cols), lambda i, j: (i, j, 0))

    def bcast(a):
        if a.shape[1] == t:
            return pl.BlockSpec((1, tm, a.shape[2]), lambda i, j: (i, j, 0))
        return pl.BlockSpec((1, 1, a.shape[2]), lambda i, j: (i, 0, 0))

    prev = pl.BlockSpec((1, 8, CONV_DIM), lambda i, j: (i, jnp.maximum(j * (tm // 8) - 1, 0), 0))
    consts = [w["conv_w"], w["g_conv"], w["g_nsa"], w["wo_c"], w["wo_n"], w["ln1_g"], w["ln1_b"], w["ln2_g"],
              w["ln2_b"], w["w_gate"], w["w_up"], w["w_down"]]
    return pl.pallas_call(
        functools.partial(_post_kernel, seq=seq),
        grid=(b, t // tm),
        in_specs=[row(d), row(NSA_DIM), row(CONV_DIM), prev, row(CONV_DIM), bcast(st1), bcast(st2)]
        + [bcast(m) for m in mods] + [_const_spec(c.shape) for c in consts],
        out_specs=row(d),
        out_shape=jax.ShapeDtypeStruct((b, t, d), F32),
        compiler_params=_cparams(2),
        name="post_block",
    )(x, o, u, u, gb, st1, st2, *mods, *consts)


def _prep_w_in(w_in):
    d = w_in.shape[0]
    c3 = 3 * CONV_DIM
    splits = (CONV_DIM, 2 * CONV_DIM, c3, c3 + NSA_DIM, c3 + NSA_DIM + KV_COLS, c3 + NSA_DIM + 2 * KV_COLS,
              c3 + NSA_DIM + 3 * KV_COLS)
    hc, gb, gc, q, kvc, kvs, kvw, gl = jnp.split(w_in, splits, axis=1)
    q4 = q.reshape(d, N_KV, Q_PER_KV, HEAD_DIM) * SCALE
    qz = jnp.einsum("dgrh,gk->dgrkh", q4, jnp.eye(N_KV, dtype=w_in.dtype)).reshape(d, QZ_DIM)
    glp = jnp.pad(gl, ((0, 0), (0, LANES - gl.shape[1])))
    return jnp.concatenate([hc, gb, gc, qz, kvc, kvs, kvw, glp], axis=1).astype(BF16)


def _prep_cmp(cmp_pe, cmp_w1, cmp_b1, cmp_w2):
    eye = jnp.eye(N_KV, dtype=cmp_w1.dtype)
    w1p = cmp_w1.reshape(2, 2, CMP_STRIDE // 2, 2, HEAD_DIM, CMP_HID)
    wpair = jnp.einsum("etpjdh,gk->epjgdtkh", w1p, eye).reshape(
        2, CMP_STRIDE // 2, 2 * N_KV * HEAD_DIM, 2 * N_KV * CMP_HID).astype(BF16)
    w2pair = jnp.einsum("ehd,gk->eghkd", cmp_w2, eye).reshape(2, N_KV * CMP_HID, N_KV * HEAD_DIM).astype(BF16)
    pe2 = cmp_pe.reshape(2, CMP_BLOCK * HEAD_DIM)
    w1f = cmp_w1.reshape(2, CMP_BLOCK * HEAD_DIM, CMP_HID).astype(BF16)
    return wpair, pe2, w1f, cmp_b1, w2pair


def _prep_post(conv_w, g_conv, g_nsa, w_o, ln1_g, ln1_b, ln2_g, ln2_b, w_gate, w_up, w_down):
    d = w_o.shape[1]
    g_nsa_r = jnp.transpose(g_nsa.reshape(N_KV, Q_PER_KV, HEAD_DIM), (1, 0, 2)).reshape(1, NSA_DIM)
    wo_n = jnp.transpose(w_o[CONV_DIM:].reshape(N_KV, Q_PER_KV, HEAD_DIM, d), (1, 0, 2, 3))
    row = lambda a: a.reshape(1, -1)
    return dict(conv_w=conv_w, g_conv=row(g_conv), g_nsa=g_nsa_r, wo_c=w_o[:CONV_DIM].astype(BF16),
                wo_n=wo_n.reshape(NSA_DIM, d).astype(BF16), ln1_g=row(ln1_g), ln1_b=row(ln1_b),
                ln2_g=row(ln2_g), ln2_b=row(ln2_b), w_gate=w_gate.astype(BF16), w_up=w_up.astype(BF16),
                w_down=w_down.astype(BF16))


def _feature_major(cache):
    n, rows = cache.shape[:2]
    return jnp.transpose(cache, (0, 2, 3, 4, 1)).reshape(n, KV_COLS, rows)


def kernel(x_prompt, x_sample, cache_cmp_kv, cache_slc_kv, cache_win_kv, state_conv, page_table, c_prompt,
           c_sample, w_ada, b_ada, w_in, conv_w, cmp_pe, cmp_w1, cmp_b1, cmp_w2, g_conv_out, g_nsa_out, w_o,
           ln1_g, ln1_b, ln2_g, ln2_b, w_ffn_gate, w_ffn_up, w_ffn_down):
    assert w_ada.shape[0] == DEPTH
    bp, s, d = x_prompt.shape
    bs, t, _ = x_sample.shape
    kv_shape = (2, N_KV, HEAD_DIM)

    w_cat = _prep_w_in(w_in[0])
    cmp_w = _prep_cmp(cmp_pe[0], cmp_w1[0], cmp_b1[0], cmp_w2[0])
    post_w = _prep_post(conv_w[0], g_conv_out[0], g_nsa_out[0], w_o[0], ln1_g[0], ln1_b[0], ln2_g[0], ln2_b[0],
                        w_ffn_gate[0], w_ffn_up[0], w_ffn_down[0])

    mod = _modulation(jnp.concatenate([c_prompt, c_sample], axis=0), w_ada[0], b_ada[0])
    mods_p = [m[:, None, :] for m in jnp.split(mod[:bp], 6, axis=-1)]
    mods_s = [jnp.repeat(m, t, axis=0)[None] for m in jnp.split(mod[bp:], 6, axis=-1)]

    u, gb, qz, kvc, kvs, kvw, kvs_b, kvw_b, gates = _in_proj(x_prompt, mods_p[1], mods_p[0], w_cat, IN_PROJ_ROWS, BF16)
    kv_cmpr = _compress_dense(kvc, *cmp_w)
    o = _attn_prompt(qz, kvs_b, kvw_b, kv_cmpr, gates)
    zero_state = jnp.zeros((bp, 1, CONV_DIM), F32)
    y_prompt = _post(x_prompt, o, u, gb, zero_state, zero_state, mods_p[2:], post_w, POST_ROWS, s)
    w_keep = min(WINDOW, s)
    cmp_p = kvc.reshape(1, bp, s, *kv_shape)
    slc_p = kvs.reshape(1, bp, s, *kv_shape)
    win_p = kvw[:, s - w_keep:].reshape(1, bp, w_keep, *kv_shape)
    conv_p = u[:, s - (CONV_W - 1):][None]

    rows = bs * t
    us, gbs, qzs, kvcs, kvss, kvws, _, _, gates_s = _in_proj(
        x_sample.reshape(1, rows, d), mods_s[1], mods_s[0], w_cat, rows, F32)
    per_b = lambda a: a.reshape(bs, t, a.shape[-1])
    kvcs, kvss, kvws = per_b(kvcs), per_b(kvss), per_b(kvws)
    xnew = jnp.pad(kvcs, ((0, 0), (0, (-t) % CMP_STRIDE), (0, 0))).reshape(bs, 1, CHUNK_COLS)
    kv_cmpr_s = _compress_paged(_feature_major(cache_cmp_kv[0]), page_table, xnew, *cmp_w)
    o_s = _attn_sample(per_b(qzs), kv_cmpr_s, per_b(gates_s), kvss, _feature_major(cache_slc_kv[0]),
                       _feature_major(cache_win_kv[0]), kvws, page_table)
    st = state_conv[0]
    tpos = jnp.arange(t)[None, :, None]
    st1 = jnp.where(tpos == 0, st[:, 1:2], 0.0).reshape(1, rows, CONV_DIM)
    st2 = jnp.where(tpos == 0, st[:, 0:1], jnp.where(tpos == 1, st[:, 1:2], 0.0)).reshape(1, rows, CONV_DIM)
    y_sample = _post(x_sample.reshape(1, rows, d), o_s.reshape(1, rows, NSA_DIM), us, gbs, st1, st2, mods_s[2:],
                     post_w, rows, t).reshape(bs, t, d)
    cmp_s = kvcs.reshape(1, bs, t, *kv_shape)
    slc_s = kvss.reshape(1, bs, t, *kv_shape)
    win_all = jnp.concatenate([cache_win_kv[0], kvws.reshape(bs, t, *kv_shape)], axis=1)
    win_s = win_all[:, t:][None]
    conv_s = jnp.concatenate([st, per_b(us)], axis=1)[:, t:][None]
    return (y_prompt, y_sample, cmp_p, slc_p, win_p, conv_p, cmp_s, slc_s, win_s, conv_s)
```

```python
import functools

import jax
import jax.numpy as jnp
from jax import lax
from jax.experimental import pallas as pl
from jax.experimental.pallas import tpu as pltpu

HEAD_DIM = 64
N_KV = 2
Q_PER_KV = 6
N_HEADS = N_KV * Q_PER_KV
CONV_DIM = 256
CONV_W = 3
NSA_DIM = N_HEADS * HEAD_DIM
KV_COLS = 2 * N_KV * HEAD_DIM
CMP_BLOCK = 32
CMP_STRIDE = 16
CMP_HID = 2 * HEAD_DIM
SEL_BLOCK = 64
CMP_PER_SEL = SEL_BLOCK // CMP_STRIDE
N_SELECT = 16
WINDOW = 512
Q_BLOCK = 128
DEPTH = 1
ALPHA = (2 * DEPTH) ** 0.25
EPS = 1e-5
SCALE = HEAD_DIM ** -0.5

LANES = 128
QZ_DIM = N_HEADS * LANES
CHUNK_COLS = CMP_STRIDE * KV_COLS
SEL_SHIFT = SEL_BLOCK.bit_length() - 1
HEAD_SHIFT = HEAD_DIM.bit_length() - 1
assert 1 << SEL_SHIFT == SEL_BLOCK and 1 << HEAD_SHIFT == HEAD_DIM
NEG = -1e30
VMEM_LIMIT = 56 * 1024 * 1024
MOD_COL_TILE = 512
IN_PROJ_ROWS = 512
POST_ROWS = 512
FFN_CHUNK = 256

F32 = jnp.float32
BF16 = jnp.bfloat16


def _cparams(n_grid):
    return pltpu.CompilerParams(dimension_semantics=("arbitrary",) * n_grid, vmem_limit_bytes=VMEM_LIMIT)


def _const_spec(shape):
    nd = len(shape)
    return pl.BlockSpec(shape, lambda *_: (0,) * nd, pipeline_mode=pl.Buffered(1))


def _dot(a, b):
    return jnp.dot(a, b, preferred_element_type=F32)


def _dot_nt(a, b):
    return lax.dot_general(a, b, (((1,), (1,)), ((), ())), preferred_element_type=F32)


def _iota(shape, dim):
    return lax.broadcasted_iota(jnp.int32, shape, dim)


def _lane_tiles(x):
    return [x[:, w * LANES:(w + 1) * LANES] for w in range(x.shape[1] // LANES)]


def _mod_kernel(c_ref, w_ref, b_ref, o_ref):
    c = c_ref[...]
    a = (c * jax.nn.sigmoid(c)).astype(BF16)
    o_ref[...] = _dot(a, w_ref[...].astype(BF16)) + b_ref[...]


def _modulation(c_all, w_ada, b_ada):
    m, d = c_all.shape
    n = w_ada.shape[1]
    tn = MOD_COL_TILE
    return pl.pallas_call(
        _mod_kernel,
        grid=(n // tn,),
        in_specs=[pl.BlockSpec((m, d), lambda j: (0, 0)),
                  pl.BlockSpec((d, tn), lambda j: (0, j)),
                  pl.BlockSpec((1, tn), lambda j: (0, j))],
        out_specs=pl.BlockSpec((m, tn), lambda j: (0, j)),
        out_shape=jax.ShapeDtypeStruct((m, n), F32),
        compiler_params=_cparams(1),
        name="modulation",
    )(c_all, w_ada, b_ada.reshape(1, n))


_C_HC, _C_GB, _C_GC, _C_Q = 0, CONV_DIM, 2 * CONV_DIM, 3 * CONV_DIM
_C_KVC = _C_Q + QZ_DIM
_C_KVS = _C_KVC + KV_COLS
_C_KVW = _C_KVS + KV_COLS
_C_GL = _C_KVW + KV_COLS
W_IN_COLS = _C_GL + LANES


def _inproj_kernel(x_ref, sc_ref, sh_ref, w_ref, u_ref, gb_ref, qz_ref, kvc_ref, kvs_ref, kvw_ref,
                   kvsb_ref, kvwb_ref, gates_ref):
    h = (x_ref[0] * (1.0 + sc_ref[0]) + sh_ref[0]).astype(BF16)
    hc = _dot(h, w_ref[:, _C_HC:_C_GB])
    gb_ref[0] = _dot(h, w_ref[:, _C_GB:_C_GC])
    gc = _dot(h, w_ref[:, _C_GC:_C_Q])
    u_ref[0] = gc * hc
    qz_ref[0] = _dot(h, w_ref[:, _C_Q:_C_KVC]).astype(qz_ref.dtype)
    kvc_ref[0] = _dot(h, w_ref[:, _C_KVC:_C_KVS])
    kvs = _dot(h, w_ref[:, _C_KVS:_C_KVW])
    kvs_ref[0] = kvs
    kvsb_ref[0] = kvs.astype(BF16)
    kvw = _dot(h, w_ref[:, _C_KVW:_C_GL])
    kvw_ref[0] = kvw
    kvwb_ref[0] = kvw.astype(BF16)
    gates_ref[0] = jax.nn.sigmoid(_dot(h, w_ref[:, _C_GL:W_IN_COLS]))


def _in_proj(x, scale, shift, w_cat, tm, qz_dtype):
    b, t, d = x.shape
    rm = scale.shape[1]
    mod_block = (1, tm, d) if rm == t else (1, 1, d)
    mod_map = (lambda i, j: (i, j, 0)) if rm == t else (lambda i, j: (i, 0, 0))
    row = lambda cols: pl.BlockSpec((1, tm, cols), lambda i, j: (i, j, 0))
    sds = lambda cols, dt: jax.ShapeDtypeStruct((b, t, cols), dt)
    return pl.pallas_call(
        _inproj_kernel,
        grid=(b, t // tm),
        in_specs=[row(d), pl.BlockSpec(mod_block, mod_map), pl.BlockSpec(mod_block, mod_map),
                  _const_spec((d, W_IN_COLS))],
        out_specs=[row(CONV_DIM), row(CONV_DIM), row(QZ_DIM), row(KV_COLS), row(KV_COLS), row(KV_COLS),
                   row(KV_COLS), row(KV_COLS), row(LANES)],
        out_shape=[sds(CONV_DIM, F32), sds(CONV_DIM, F32), sds(QZ_DIM, qz_dtype), sds(KV_COLS, F32),
                   sds(KV_COLS, F32), sds(KV_COLS, F32), sds(KV_COLS, BF16), sds(KV_COLS, BF16),
                   sds(LANES, F32)],
        compiler_params=_cparams(2),
        name="in_proj",
    )(x, scale, shift, w_cat)


def _gelu_tanh(x):
    return jax.nn.gelu(x, approximate=True)


def _cmp_bias_e(pe_ref, w1f_ref, b1_ref, e):
    pe = jnp.broadcast_to(pe_ref[e:e + 1, :], (8, pe_ref.shape[1])).astype(BF16)
    return _dot(pe, w1f_ref[e])[0:1, :] + b1_ref[e:e + 1, :]


def _chunk_partials(row_pairs, wpair_ref, e):
    acc = None
    for jp in range(CMP_STRIDE // 2):
        part = _dot(row_pairs(jp), wpair_ref[e, jp])
        acc = part if acc is None else acc + part
    return acc


def _compress_mlp(top, bot_next, pe_ref, w1f_ref, b1_ref, w2_ref, e):
    bias = jnp.concatenate([_cmp_bias_e(pe_ref, w1f_ref, b1_ref, e)] * N_KV, axis=1)
    return _dot(_gelu_tanh(top + bot_next + bias).astype(BF16), w2_ref[e])


def _compress_dense_kernel(x_ref, wpair_ref, pe_ref, w1f_ref, b1_ref, w2_ref, o_ref, r_sc):
    n = o_ref.shape[1]
    half_cols = N_KV * HEAD_DIM
    hid_cols = N_KV * CMP_HID
    outs = []
    for e in range(2):
        r_sc[e] = x_ref[0, :, e * half_cols:(e + 1) * half_cols]
        acc = _chunk_partials(
            lambda jp: jnp.concatenate([r_sc[e, pl.ds(2 * jp + jj, n, stride=CMP_STRIDE), :] for jj in (0, 1)],
                                       axis=1).astype(BF16), wpair_ref, e)
        bot_next = pltpu.roll(acc[:, hid_cols:], n - 1, 0)
        outs.append(_compress_mlp(acc[:, :hid_cols], bot_next, pe_ref, w1f_ref, b1_ref, w2_ref, e))
    o_ref[0] = jnp.concatenate(outs, axis=1).astype(o_ref.dtype)


def _compress_dense(kvc, wpair, pe2, w1f, b1, w2pair):
    b, s, _ = kvc.shape
    nch = s // CMP_STRIDE
    return pl.pallas_call(
        _compress_dense_kernel,
        grid=(b,),
        in_specs=[pl.BlockSpec((1, s, KV_COLS), lambda i: (i, 0, 0)),
                  _const_spec(wpair.shape), _const_spec(pe2.shape), _const_spec(w1f.shape),
                  _const_spec(b1.shape), _const_spec(w2pair.shape)],
        out_specs=pl.BlockSpec((1, nch, KV_COLS), lambda i: (i, 0, 0)),
        out_shape=jax.ShapeDtypeStruct((b, nch, KV_COLS), BF16),
        scratch_shapes=[pltpu.VMEM((2, s, LANES), F32)],
        compiler_params=_cparams(1),
        name="compress_prompt",
    )(kvc, wpair, pe2, w1f, b1, w2pair)


PAGES_PER_STEP = 32
CHUNK_PITCH = 24


def _page_copy(pool_ref, page_index, buf, sem, slot, p):
    return pltpu.make_async_copy(pool_ref.at[page_index], buf.at[slot, p], sem.at[slot, p])


def _await_pages(pt_ref, pool_ref, buf, sem, n_steps):
    b, kt = pl.program_id(0), pl.program_id(1)

    @pl.when((b == 0) & (kt == 0))
    def _():
        for k in range(n_steps):
            for p in range(PAGES_PER_STEP):
                _page_copy(pool_ref, pt_ref[0, k * PAGES_PER_STEP + p], buf, sem, k, p).start(priority=p % 2)

    for p in range(PAGES_PER_STEP):
        _page_copy(pool_ref, 0, buf, sem, kt, p).wait()
    return kt


def _refill_pages(pt_ref, pool_ref, buf, sem):
    kt = pl.program_id(1)
    nb = jnp.minimum(pl.program_id(0) + 1, pl.num_programs(0) - 1)
    for p in range(PAGES_PER_STEP):
        _page_copy(pool_ref, pt_ref[nb, kt * PAGES_PER_STEP + p], buf, sem, kt, p).start(priority=p % 2)


def _drain_pages(pool_ref, buf, sem, n_steps):
    @pl.when((pl.program_id(0) == pl.num_programs(0) - 1) & (pl.program_id(1) == n_steps - 1))
    def _():
        for k in range(n_steps):
            for p in range(PAGES_PER_STEP):
                _page_copy(pool_ref, 0, buf, sem, k, p).wait()


def _compress_paged_kernel(pt_ref, pool_ref, *refs, n_steps):
    g_pages = PAGES_PER_STEP
    (xnew_ref, wpair_ref, pe_ref, w1f_ref, b1_ref, w2_ref, o_ref, r_sc, tb_sc, buf, sem) = refs
    kt = pl.program_id(1)
    page = buf.shape[3]
    slot = _await_pages(pt_ref, pool_ref, buf, sem, n_steps)
    cpp = page // CMP_STRIDE
    rows_step = g_pages * cpp
    half_cols = N_KV * HEAD_DIM
    hid_cols = N_KV * CMP_HID

    row0 = pl.multiple_of(kt * rows_step, rows_step)
    for e in range(2):
        for p in range(g_pages):
            rows_t = buf[slot, p, e * half_cols:(e + 1) * half_cols, :].T
            for c in range(cpp):
                r_sc[e, pl.ds((p * cpp + c) * CHUNK_PITCH, CMP_STRIDE), :] = rows_t[c * CMP_STRIDE:(c + 1) * CMP_STRIDE]
        tb_sc[e, pl.ds(row0, rows_step), :] = _chunk_partials(
            lambda jp: jnp.concatenate([r_sc[e, pl.ds(2 * jp + jj, rows_step, stride=CHUNK_PITCH), :]
                                        for jj in (0, 1)], axis=1).astype(BF16), wpair_ref, e)
    _refill_pages(pt_ref, pool_ref, buf, sem)

    @pl.when(kt == n_steps - 1)
    def _():
        n = tb_sc.shape[1]
        xn = jnp.broadcast_to(xnew_ref[0], (8, CHUNK_COLS)).astype(BF16)
        last = _iota((n, 1), 0) == n - 1
        outs = []
        for e in range(2):
            new_pair = lambda jp: jnp.concatenate(
                [xn[:, (2 * jp + jj) * KV_COLS + e * half_cols:(2 * jp + jj) * KV_COLS + (e + 1) * half_cols]
                 for jj in (0, 1)], axis=1)
            bot_new = _chunk_partials(new_pair, wpair_ref, e)[0:1, hid_cols:]
            top, bot = tb_sc[e, :, :hid_cols], tb_sc[e, :, hid_cols:]
            bot_next = jnp.where(last, bot_new, pltpu.roll(bot, n - 1, 0))
            outs.append(_compress_mlp(top, bot_next, pe_ref, w1f_ref, b1_ref, w2_ref, e))
        o_ref[0] = jnp.concatenate(outs, axis=1).astype(o_ref.dtype)

    _drain_pages(pool_ref, buf, sem, n_steps)


def _page_stream_scratch(page, n_steps):
    return [pltpu.VMEM((n_steps, PAGES_PER_STEP, KV_COLS, page), F32),
            pltpu.SemaphoreType.DMA((n_steps, PAGES_PER_STEP))]


def _compress_paged(pool_t, page_table, xnew, wpair, pe2, w1f, b1, w2pair):
    bn, n_pages = page_table.shape
    page = pool_t.shape[2]
    g = PAGES_PER_STEP
    n_steps = n_pages // g
    assert n_pages % g == 0 and page % CMP_STRIDE == 0 and page == LANES
    n = n_pages * page // CMP_STRIDE

    def const(shape):
        nd = len(shape)
        return pl.BlockSpec(shape, lambda i, j, pt: (0,) * nd, pipeline_mode=pl.Buffered(1))

    grid_spec = pltpu.PrefetchScalarGridSpec(
        num_scalar_prefetch=1,
        grid=(bn, n_steps),
        in_specs=[pl.BlockSpec(memory_space=pl.ANY),
                  pl.BlockSpec((1, 1, CHUNK_COLS), lambda i, j, pt: (i, 0, 0)), const(wpair.shape),
                  const(pe2.shape), const(w1f.shape), const(b1.shape), const(w2pair.shape)],
        out_specs=pl.BlockSpec((1, n, KV_COLS), lambda i, j, pt: (i, 0, 0)),
        scratch_shapes=[pltpu.VMEM((2, g * (page // CMP_STRIDE) * CHUNK_PITCH, LANES), F32),
                        pltpu.VMEM((2, n, 2 * N_KV * CMP_HID), F32)] + _page_stream_scratch(page, n_steps),
    )
    return pl.pallas_call(
        functools.partial(_compress_paged_kernel, n_steps=n_steps),
        grid_spec=grid_spec,
        out_shape=jax.ShapeDtypeStruct((bn, n, KV_COLS), BF16),
        compiler_params=_cparams(2),
        name="compress_sample",
    )(page_table, pool_t, xnew, wpair, pe2, w1f, b1, w2pair)


def _split3_dot(x, m01):
    h1 = x.astype(BF16)
    r1 = x - h1.astype(F32)
    h2 = r1.astype(BF16)
    h3 = (r1 - h2.astype(F32)).astype(BF16)
    return _dot(h1, m01) + _dot(h2, m01) + _dot(h3, m01)


def _block_scores(imp, n_lanes):
    ncp = imp.shape[1]
    c = _iota((ncp, n_lanes), 0)
    j = _iota((ncp, n_lanes), 1)
    a = jnp.where((c >= CMP_PER_SEL * j - 1) & (c <= CMP_PER_SEL * j + CMP_PER_SEL - 1), 1.0, 0.0).astype(BF16)
    return _split3_dot(imp, a)


def _ranked_scores(score, qpos, n_blocks):
    j = _iota(score.shape, 1)
    cur = qpos >> SEL_SHIFT
    valid = j * SEL_BLOCK <= qpos
    forced = (j == 0) | (j == cur) | (j == cur - 1)
    return jnp.where(forced, -NEG, jnp.where(valid, score, NEG))


def _select_blocks_pairwise(score, qpos, n_blocks, n_select):
    r, l = score.shape
    nb8 = -(-n_blocks // 8) * 8
    ranked = _ranked_scores(score, qpos, n_blocks)
    ranked_t = jnp.concatenate([ranked, jnp.zeros((LANES - r, l), F32)], axis=0).T
    lower_index = jnp.where(_iota((nb8, l), 0) < _iota((nb8, l), 1), 1.0, 0.0)
    in_range = _iota((1, l), 1) < n_blocks
    rows = []
    for v in range(r):
        col = jnp.broadcast_to(ranked_t[:nb8, v:v + 1], (nb8, l))
        row = ranked[v:v + 1, :]
        beats = jnp.where(col > row, 1.0, jnp.where(col == row, lower_index, 0.0))
        rank = jnp.sum(beats, axis=0, keepdims=True)
        rows.append(jnp.where((rank < n_select) & in_range, 1.0, 0.0))
    return jnp.concatenate(rows, axis=0)


def _select_blocks_rank(score, qpos, n_blocks, n_select):
    r, l = score.shape
    nb8 = -(-n_blocks // 8) * 8
    st = _ranked_scores(score, qpos, n_blocks).T[:nb8]
    jrow = _iota((nb8, r), 0)
    rank = jnp.zeros((nb8, r), F32)
    for jp in range(n_blocks):
        row = st[jp:jp + 1, :]
        beats = (row > st) | ((row == st) & (jrow > jp))
        rank = rank + jnp.where(beats, 1.0, 0.0)
    sel_t = jnp.where((rank < n_select) & (jrow < n_blocks), 1.0, 0.0)
    if nb8 < l:
        sel_t = jnp.concatenate([sel_t, jnp.zeros((l - nb8, r), F32)], axis=0)
    return sel_t.T


def _softmax_parts(s, mask):
    s = jnp.where(mask, s, NEG)
    m = jnp.max(s, axis=-1, keepdims=True)
    e = jnp.where(mask, jnp.exp(s - m), 0.0)
    return e, jnp.sum(e, axis=-1, keepdims=True)


SEL_TILE = 512


def _attn_prompt_kernel(qz_ref, kvs_ref, kvw_ref, kvc_ref, gates_ref, exp_ref, o_ref,
                        s_sc, p_sc, pw_sc, m_sc, acc_sc, ocw_sc, *, seq):
    qb = Q_BLOCK
    tk = SEL_TILE
    start = pl.program_id(1) * qb
    qpos = start + _iota((qb, 1), 0)
    ncp = kvc_ref.shape[1]
    n_blocks = -(-seq // SEL_BLOCK)
    rows = Q_PER_KV * qb
    gates = gates_ref[0]
    n_tiles = (start + qb + tk - 1) // tk
    head_rows = [slice(r * qb, (r + 1) * qb) for r in range(Q_PER_KV)]
    kv_heads = range(N_KV)

    def q_rows(g):
        return jnp.concatenate(
            [qz_ref[0, :, (g * Q_PER_KV + r) * LANES:(g * Q_PER_KV + r + 1) * LANES] for r in range(Q_PER_KV)],
            axis=0)


    def values_and_ones(g, vv):
        return jnp.where((_iota((1, LANES), 1) >> HEAD_SHIFT) == g, vv, jnp.ones_like(vv))

    def normalised(acc):
        return acc / jnp.maximum(pltpu.roll(acc, LANES // 2, 1), 1e-30)

    o_cmp, sel = [], []
    c = _iota((qb, ncp), 1)
    cmask = (CMP_STRIDE * c + (CMP_BLOCK - 1) <= qpos) & (c < ncp - 1)
    for g in kv_heads:
        s = _dot_nt(q_rows(g), kvc_ref[0, :, 0:LANES]).reshape(Q_PER_KV, qb, ncp)
        e, l = _softmax_parts(s, cmask)
        p = e / jnp.maximum(l, 1e-30)
        o_cmp.append(_dot(p.reshape(rows, ncp).astype(BF16), kvc_ref[0, :, LANES:2 * LANES]))
        imp = jnp.sum(p, axis=0)
        sel.append(_select_blocks_rank(_block_scores(imp, LANES), qpos, n_blocks,
                                       min(N_SELECT, n_blocks)).astype(BF16))

    wlen = WINDOW + qb
    w0 = pl.multiple_of(jnp.maximum(start - WINDOW, 0), qb)
    kpos = w0 + _iota((1, wlen), 1)
    wmask = (kpos <= qpos) & (kpos > qpos - WINDOW)
    for g in kv_heads:
        s = _dot_nt(q_rows(g), kvw_ref[0, pl.ds(w0, wlen), 0:LANES])
        for rs in head_rows:
            sr = jnp.where(wmask, s[rs], NEG)
            m = jnp.max(functools.reduce(jnp.maximum, _lane_tiles(sr)), axis=-1, keepdims=True)
            pw_sc[g, rs, :] = jnp.exp((sr - m).astype(BF16))
        o_win = normalised(_dot(pw_sc[g], values_and_ones(g, kvw_ref[0, pl.ds(w0, wlen), LANES:2 * LANES])))
        for r, rs in enumerate(head_rows):
            h = g * Q_PER_KV + r
            ocw_sc[g, rs, :] = (gates[:, h:h + 1] * o_cmp[g][rs]
                                + gates[:, 2 * N_HEADS + h:2 * N_HEADS + h + 1] * o_win[rs])

    m_sc[...] = jnp.full(m_sc.shape, NEG, F32)

    def pass1(kt, _):
        k0 = pl.multiple_of(kt * tk, tk)
        causal = (k0 + _iota((1, tk), 1)) <= qpos
        for g in kv_heads:
            s = _dot_nt(q_rows(g), kvs_ref[0, pl.ds(k0, tk), 0:LANES])
            mask = (_dot(sel[g], exp_ref[kt]) > 0.5) & causal
            for rs in head_rows:
                sr = jnp.where(mask, s[rs], NEG)
                s_sc[g, kt, rs, :] = sr
                m_sc[g, rs, :] = jnp.maximum(m_sc[g, rs, :], functools.reduce(jnp.maximum, _lane_tiles(sr)))
        return 0

    lax.fori_loop(0, n_tiles, pass1, 0)
    m_sc[...] = jnp.broadcast_to(jnp.max(m_sc[...], axis=-1, keepdims=True), m_sc.shape)
    acc_sc[...] = jnp.zeros(acc_sc.shape, F32)

    def pass2(kt, _):
        k0 = pl.multiple_of(kt * tk, tk)
        for g in kv_heads:
            for rs in head_rows:
                x = s_sc[g, kt, rs, :] - jnp.concatenate([m_sc[g, rs, :]] * (tk // LANES), axis=1)
                p_sc[g, rs, :] = jnp.exp(x.astype(BF16))
            vv = values_and_ones(g, kvs_ref[0, pl.ds(k0, tk), LANES:2 * LANES])
            acc_sc[g] = acc_sc[g] + _dot(p_sc[g], vv)
        return 0

    lax.fori_loop(0, n_tiles, pass2, 0)

    o_slc = [normalised(acc_sc[g]) for g in kv_heads]
    lane_head = _iota((qb, LANES), 1) >> HEAD_SHIFT
    for r, rs in enumerate(head_rows):
        o = None
        for g in kv_heads:
            h = g * Q_PER_KV + r
            o_g = ocw_sc[g, rs, :] + gates[:, N_HEADS + h:N_HEADS + h + 1] * o_slc[g][rs]
            o = o_g if o is None else jnp.where(lane_head == g, o_g, o)
        o_ref[0, :, r * LANES:(r + 1) * LANES] = o


def _expand_matrix(n_tiles, tile):
    t = jnp.arange(n_tiles)[:, None, None]
    j = jnp.arange(LANES)[None, :, None]
    k = jnp.arange(tile)[None, None, :]
    return (j == (t * tile + k) // SEL_BLOCK).astype(BF16)


def _attn_prompt(qz, kvs_b, kvw_b, kv_cmpr, gates):
    b, s, _ = qz.shape
    assert s % Q_BLOCK == 0 and s % SEL_TILE == 0 and s >= WINDOW + Q_BLOCK and s // SEL_BLOCK <= LANES
    ncp = kv_cmpr.shape[1]
    n_tiles = s // SEL_TILE
    expand = _expand_matrix(n_tiles, SEL_TILE)
    rows = Q_PER_KV * Q_BLOCK
    full = lambda n: pl.BlockSpec((1, n, KV_COLS), lambda i, j: (i, 0, 0))
    return pl.pallas_call(
        functools.partial(_attn_prompt_kernel, seq=s),
        grid=(b, s // Q_BLOCK),
        in_specs=[pl.BlockSpec((1, Q_BLOCK, QZ_DIM), lambda i, j: (i, j, 0)), full(s), full(s), full(ncp),
                  pl.BlockSpec((1, Q_BLOCK, LANES), lambda i, j: (i, j, 0)), _const_spec(expand.shape)],
        out_specs=pl.BlockSpec((1, Q_BLOCK, NSA_DIM), lambda i, j: (i, j, 0)),
        out_shape=jax.ShapeDtypeStruct((b, s, NSA_DIM), F32),
        scratch_shapes=[pltpu.VMEM((N_KV, n_tiles, rows, SEL_TILE), F32), pltpu.VMEM((N_KV, rows, SEL_TILE), BF16),
                        pltpu.VMEM((N_KV, rows, WINDOW + Q_BLOCK), BF16), pltpu.VMEM((N_KV, rows, LANES), F32),
                        pltpu.VMEM((N_KV, rows, LANES), F32), pltpu.VMEM((N_KV, rows, LANES), F32)],
        compiler_params=_cparams(2),
        name="attn_prompt",
    )(qz, kvs_b, kvw_b, kv_cmpr, gates, expand)


def _attn_sample_kernel(pt_ref, pool_ref, *refs, past, n_steps):
    g_pages = PAGES_PER_STEP
    (qz_ref, kvc_ref, gates_ref, kvs_new_ref, win_ref, kvw_new_ref, exp_ref,
     o_ref, q_sc, sel_sc, s_sc, v_sc, snew_sc, mp_sc, ocw_sc, buf, sem) = refs
    kt = pl.program_id(1)
    t = qz_ref.shape[1]
    rows = N_HEADS * t
    page = buf.shape[3]
    tile = g_pages * page
    blocks_per_tile = tile // SEL_BLOCK
    n_past_blocks = past // SEL_BLOCK
    n_blocks = n_past_blocks + -(-t // SEL_BLOCK)
    sel_lanes = sel_sc.shape[0] * LANES
    half_cols = N_KV * HEAD_DIM
    qpos = past + _iota((t, 1), 0)
    qpos_gt = jnp.concatenate([qpos] * N_KV, axis=0)
    own_rows = _iota((N_KV, Q_PER_KV, t, LANES), 0) == (_iota((N_KV, Q_PER_KV, t, LANES), 3) >> HEAD_SHIFT)

    def new_rows_padded():
        kvn = kvs_new_ref[0]
        return jnp.concatenate([kvn, jnp.zeros((LANES - t, KV_COLS), F32)], axis=0).astype(BF16)

    def masked_scores(s, mask):
        k = s.shape[1]
        return jnp.where(mask.reshape(N_KV, 1, t, k), s.reshape(N_KV, Q_PER_KV, t, k), NEG).reshape(rows, k)

    @pl.when(kt == 0)
    def _():
        q = jnp.concatenate([qz_ref[0, :, h * LANES:(h + 1) * LANES] for h in range(N_HEADS)], axis=0)
        q_sc[...] = q.astype(BF16)
        qb = q_sc[...]
        ncp = kvc_ref.shape[1]
        s = _dot_nt(qb, kvc_ref[0, :, 0:LANES]).reshape(N_HEADS, t, ncp)
        c = _iota((t, ncp), 1)
        cmask = CMP_STRIDE * c + (CMP_BLOCK - 1) <= qpos
        e, l = _softmax_parts(s, cmask)
        p = e / jnp.maximum(l, 1e-30)
        o_cmp = _dot(p.reshape(rows, ncp).astype(BF16), kvc_ref[0, :, LANES:2 * LANES])
        imp = jnp.sum(p.reshape(N_KV, Q_PER_KV, t, ncp), axis=1).reshape(N_KV * t, ncp)
        sel = _select_blocks_pairwise(_block_scores(imp, sel_lanes), qpos_gt, n_blocks, min(N_SELECT, n_blocks))
        for w in range(sel_sc.shape[0]):
            sel_sc[w] = sel[:, w * LANES:(w + 1) * LANES]
        wb = win_ref.shape[2]
        win = win_ref[0].astype(BF16)
        kvw_new = kvw_new_ref[0].astype(BF16)
        s_old = _dot(qb, win[0:half_cols]).reshape(N_HEADS, t, wb)
        s_new = _dot_nt(qb, kvw_new[:, 0:half_cols]).reshape(N_HEADS, t, t)
        kpos_old = past - wb + _iota((1, wb), 1)
        kpos_new = past + _iota((1, t), 1)
        mask_old = (kpos_old <= qpos) & (kpos_old > qpos - WINDOW) & (kpos_old >= 0)
        mask_new = (kpos_new <= qpos) & (kpos_new > qpos - WINDOW)
        s_old = jnp.where(mask_old, s_old, NEG)
        s_new = jnp.where(mask_new, s_new, NEG)
        m = jnp.maximum(jnp.max(s_old, axis=-1, keepdims=True), jnp.max(s_new, axis=-1, keepdims=True))
        e_old = jnp.where(mask_old, jnp.exp(s_old - m), 0.0)
        e_new = jnp.where(mask_new, jnp.exp(s_new - m), 0.0)
        l = jnp.sum(e_old, axis=-1, keepdims=True) + jnp.sum(e_new, axis=-1, keepdims=True)
        o_win = (_dot_nt(e_old.reshape(rows, wb).astype(BF16), win[half_cols:])
                 + _dot(e_new.reshape(rows, t).astype(BF16), kvw_new[:, half_cols:]))
        o_win = o_win.reshape(N_HEADS, t, LANES) / jnp.maximum(l, 1e-30)
        gts = gates_ref[0]
        gc = jnp.stack([jnp.broadcast_to(gts[:, h:h + 1], (t, LANES)) for h in range(N_HEADS)])
        gw = jnp.stack([jnp.broadcast_to(gts[:, 2 * N_HEADS + h:2 * N_HEADS + h + 1], (t, LANES))
                        for h in range(N_HEADS)])
        ocw_sc[...] = gc * o_cmp.reshape(N_HEADS, t, LANES) + gw * o_win
        jn = (past + _iota((1, LANES), 1)) >> SEL_SHIFT
        seln = jnp.zeros((N_KV * t, LANES), F32)
        for jb in range(n_past_blocks, n_blocks):
            seln = jnp.where(jn == jb, sel[:, jb:jb + 1], seln)
        npos = past + _iota((1, LANES), 1)
        nmask = (seln > 0.5) & (npos <= qpos_gt) & (npos < past + t)
        s_n = masked_scores(_dot_nt(qb, new_rows_padded()[:, 0:half_cols]), nmask)
        snew_sc[...] = s_n
        mp_sc[...] = s_n

    slot = _await_pages(pt_ref, pool_ref, buf, sem, n_steps)
    k_t = jnp.concatenate([buf[slot, p, 0:half_cols, :] for p in range(g_pages)], axis=1).astype(BF16)
    v_sc[kt] = jnp.concatenate([buf[slot, p, half_cols:, :] for p in range(g_pages)], axis=1).astype(BF16)
    b0 = kt * blocks_per_tile
    selw = sel_sc[b0 // LANES].astype(BF16)
    kpos = kt * tile + _iota((1, tile), 1)
    mask = (_dot(selw, exp_ref[0]) > 0.5) & (kpos <= qpos_gt)
    s = masked_scores(_dot(q_sc[...], k_t), mask)
    s_sc[kt] = s
    mp_sc[...] = jnp.maximum(mp_sc[...], functools.reduce(jnp.maximum, _lane_tiles(s)))
    _refill_pages(pt_ref, pool_ref, buf, sem)

    @pl.when(kt == n_steps - 1)
    def _():
        m = jnp.max(mp_sc[...], axis=-1, keepdims=True)
        e = jnp.exp(snew_sc[...] - m)
        lp = e
        acc = _dot(e.astype(BF16), new_rows_padded()[:, half_cols:])
        for k2 in range(n_steps):
            e = jnp.exp(s_sc[k2] - m)
            lp = lp + functools.reduce(jnp.add, _lane_tiles(e))
            acc = acc + _dot_nt(e.astype(BF16), v_sc[k2])
        l = jnp.sum(lp, axis=-1, keepdims=True)
        gts = gates_ref[0]
        gs = jnp.stack([jnp.broadcast_to(gts[:, N_HEADS + h:N_HEADS + h + 1], (t, LANES))
                        for h in range(N_HEADS)])
        o_slc = (acc / jnp.maximum(l, 1e-30)).reshape(N_HEADS, t, LANES)
        o = jnp.where(own_rows.reshape(N_HEADS, t, LANES), ocw_sc[...] + gs * o_slc, 0.0)
        for r in range(Q_PER_KV):
            o_ref[0, :, r * LANES:(r + 1) * LANES] = functools.reduce(
                jnp.add, [o[g * Q_PER_KV + r] for g in range(N_KV)])

    _drain_pages(pool_ref, buf, sem, n_steps)


def _attn_sample(qz, kv_cmpr, gates, kvs_new, slc_pool_t, win_t, kvw_new, page_table):
    bn, t, _ = qz.shape
    n_pages = page_table.shape[1]
    page = slc_pool_t.shape[2]
    past = n_pages * page
    g = PAGES_PER_STEP
    n_steps = n_pages // g
    tile = g * page
    blocks_per_tile = tile // SEL_BLOCK
    assert n_pages % g == 0 and page % SEL_BLOCK == 0 and LANES % blocks_per_tile == 0 and t % 8 == 0
    n_blocks = past // SEL_BLOCK + -(-t // SEL_BLOCK)
    sel_groups = -(-n_blocks // LANES)
    ncp = kv_cmpr.shape[1]
    wb = win_t.shape[2]
    n_var = LANES // blocks_per_tile
    expand = _expand_matrix(n_var, tile)

    per_b = lambda n, cols: pl.BlockSpec((1, n, cols), lambda i, j, pt: (i, 0, 0))
    grid_spec = pltpu.PrefetchScalarGridSpec(
        num_scalar_prefetch=1,
        grid=(bn, n_steps),
        in_specs=[pl.BlockSpec(memory_space=pl.ANY),
                  per_b(t, QZ_DIM), per_b(ncp, KV_COLS), per_b(t, LANES), per_b(t, KV_COLS), per_b(KV_COLS, wb),
                  per_b(t, KV_COLS), pl.BlockSpec((1, LANES, tile), lambda i, j, pt: (j % n_var, 0, 0))],
        out_specs=per_b(t, NSA_DIM),
        scratch_shapes=[pltpu.VMEM((N_HEADS * t, LANES), BF16),
                        pltpu.VMEM((sel_groups, N_KV * t, LANES), F32),
                        pltpu.VMEM((n_steps, N_HEADS * t, tile), F32), pltpu.VMEM((n_steps, LANES, tile), BF16),
                        pltpu.VMEM((N_HEADS * t, LANES), F32), pltpu.VMEM((N_HEADS * t, LANES), F32),
                        pltpu.VMEM((N_HEADS, t, LANES), F32)] + _page_stream_scratch(page, n_steps),
    )
    return pl.pallas_call(
        functools.partial(_attn_sample_kernel, past=past, n_steps=n_steps),
        grid_spec=grid_spec,
        out_shape=jax.ShapeDtypeStruct((bn, t, NSA_DIM), F32),
        compiler_params=_cparams(2),
        name="attn_sample",
    )(page_table, slc_pool_t, qz, kv_cmpr, gates, kvs_new, win_t, kvw_new, expand)


def _layernorm(x, g, b):
    mu = jnp.mean(x, axis=-1, keepdims=True)
    xc = x - mu
    var = jnp.mean(xc * xc, axis=-1, keepdims=True)
    return xc * lax.rsqrt(var + EPS) * g + b


def _post_kernel(x_ref, o_ref, u_ref, uprev_ref, gb_ref, st1_ref, st2_ref, g1_ref, sh2_ref, sc2_ref, g2_ref,
                 convw_ref, gconv_ref, gnsa_ref, woc_ref, won_ref, ln1g_ref, ln1b_ref, ln2g_ref, ln2b_ref,
                 wg_ref, wu_ref, wd_ref, y_ref, *, seq):
    tm = x_ref.shape[1]
    u = u_ref[0]
    ext = jnp.concatenate([uprev_ref[0], u], axis=0)
    if tm <= seq:
        pos = (pl.program_id(1) * tm) % seq + _iota((tm, 1), 0)
    else:
        pos = lax.rem(_iota((tm, 1), 0), seq)
    p1 = jnp.where(pos >= 1, ext[7:7 + tm], st1_ref[0])
    p2 = jnp.where(pos >= 2, ext[6:6 + tm], st2_ref[0])
    cw = convw_ref[...]
    y_c = gb_ref[0] * (cw[0:1] * p2 + cw[1:2] * p1 + cw[2:3] * u)
    yn = y_c * lax.rsqrt(jnp.mean(y_c * y_c, axis=-1, keepdims=True) + EPS) * gconv_ref[...]
    o = o_ref[0]
    on = o * lax.rsqrt(jnp.sum(o * o, axis=-1, keepdims=True) * (1.0 / NSA_DIM) + EPS) * gnsa_ref[...]
    mix = _dot(yn.astype(BF16), woc_ref[...]) + _dot(on.astype(BF16), won_ref[...])
    x1 = _layernorm(ALPHA * x_ref[0] + g1_ref[0] * mix, ln1g_ref[...], ln1b_ref[...])
    h = (x1 * (1.0 + sc2_ref[0]) + sh2_ref[0]).astype(BF16)
    f = None
    for c0 in range(0, wg_ref.shape[1], FFN_CHUNK):
        a = _dot(h, wg_ref[:, c0:c0 + FFN_CHUNK])
        act = (a * jax.nn.sigmoid(a)) * _dot(h, wu_ref[:, c0:c0 + FFN_CHUNK])
        part = _dot(act.astype(BF16), wd_ref[c0:c0 + FFN_CHUNK, :])
        f = part if f is None else f + part
    y_ref[0] = _layernorm(ALPHA * x1 + g2_ref[0] * f, ln2g_ref[...], ln2b_ref[...])


def _post(x, o, u, gb, st1, st2, mods, w, tm, seq):
    b, t, d = x.shape
    row = lambda cols: pl.BlockSpec((1, tm, cols), lambda i, j: (i, j, 0))

    def bcast(a):
        if a.shape[1] == t:
            return pl.BlockSpec((1, tm, a.shape[2]), lambda i, j: (i, j, 0))
        return pl.BlockSpec((1, 1, a.shape[2]), lambda i, j: (i, 0, 0))

    prev = pl.BlockSpec((1, 8, CONV_DIM), lambda i, j: (i, jnp.maximum(j * (tm // 8) - 1, 0), 0))
    consts = [w["conv_w"], w["g_conv"], w["g_nsa"], w["wo_c"], w["wo_n"], w["ln1_g"], w["ln1_b"], w["ln2_g"],
              w["ln2_b"], w["w_gate"], w["w_up"], w["w_down"]]
    return pl.pallas_call(
        functools.partial(_post_kernel, seq=seq),
        grid=(b, t // tm),
        in_specs=[row(d), row(NSA_DIM), row(CONV_DIM), prev, row(CONV_DIM), bcast(st1), bcast(st2)]
        + [bcast(m) for m in mods] + [_const_spec(c.shape) for c in consts],
        out_specs=row(d),
        out_shape=jax.ShapeDtypeStruct((b, t, d), F32),
        compiler_params=_cparams(2),
        name="post_block",
    )(x, o, u, u, gb, st1, st2, *mods, *consts)


def _prep_w_in(w_in):
    d = w_in.shape[0]
    c3 = 3 * CONV_DIM
    splits = (CONV_DIM, 2 * CONV_DIM, c3, c3 + NSA_DIM, c3 + NSA_DIM + KV_COLS, c3 + NSA_DIM + 2 * KV_COLS,
              c3 + NSA_DIM + 3 * KV_COLS)
    hc, gb, gc, q, kvc, kvs, kvw, gl = jnp.split(w_in, splits, axis=1)
    q4 = q.reshape(d, N_KV, Q_PER_KV, HEAD_DIM) * SCALE
    qz = jnp.einsum("dgrh,gk->dgrkh", q4, jnp.eye(N_KV, dtype=w_in.dtype)).reshape(d, QZ_DIM)
    glp = jnp.pad(gl, ((0, 0), (0, LANES - gl.shape[1])))
    return jnp.concatenate([hc, gb, gc, qz, kvc, kvs, kvw, glp], axis=1).astype(BF16)


def _prep_cmp(cmp_pe, cmp_w1, cmp_b1, cmp_w2):
    eye = jnp.eye(N_KV, dtype=cmp_w1.dtype)
    w1p = cmp_w1.reshape(2, 2, CMP_STRIDE // 2, 2, HEAD_DIM, CMP_HID)
    wpair = jnp.einsum("etpjdh,gk->epjgdtkh", w1p, eye).reshape(
        2, CMP_STRIDE // 2, 2 * N_KV * HEAD_DIM, 2 * N_KV * CMP_HID).astype(BF16)
    w2pair = jnp.einsum("ehd,gk->eghkd", cmp_w2, eye).reshape(2, N_KV * CMP_HID, N_KV * HEAD_DIM).astype(BF16)
    pe2 = cmp_pe.reshape(2, CMP_BLOCK * HEAD_DIM)
    w1f = cmp_w1.reshape(2, CMP_BLOCK * HEAD_DIM, CMP_HID).astype(BF16)
    return wpair, pe2, w1f, cmp_b1, w2pair


def _prep_post(conv_w, g_conv, g_nsa, w_o, ln1_g, ln1_b, ln2_g, ln2_b, w_gate, w_up, w_down):
    d = w_o.shape[1]
    g_nsa_r = jnp.transpose(g_nsa.reshape(N_KV, Q_PER_KV, HEAD_DIM), (1, 0, 2)).reshape(1, NSA_DIM)
    wo_n = jnp.transpose(w_o[CONV_DIM:].reshape(N_KV, Q_PER_KV, HEAD_DIM, d), (1, 0, 2, 3))
    row = lambda a: a.reshape(1, -1)
    return dict(conv_w=conv_w, g_conv=row(g_conv), g_nsa=g_nsa_r, wo_c=w_o[:CONV_DIM].astype(BF16),
                wo_n=wo_n.reshape(NSA_DIM, d).astype(BF16), ln1_g=row(ln1_g), ln1_b=row(ln1_b),
                ln2_g=row(ln2_g), ln2_b=row(ln2_b), w_gate=w_gate.astype(BF16), w_up=w_up.astype(BF16),
                w_down=w_down.astype(BF16))


def _feature_major(cache):
    n, rows = cache.shape[:2]
    return jnp.transpose(cache, (0, 2, 3, 4, 1)).reshape(n, KV_COLS, rows)


def kernel(x_prompt, x_sample, cache_cmp_kv, cache_slc_kv, cache_win_kv, state_conv, page_table, c_prompt,
           c_sample, w_ada, b_ada, w_in, conv_w, cmp_pe, cmp_w1, cmp_b1, cmp_w2, g_conv_out, g_nsa_out, w_o,
           ln1_g, ln1_b, ln2_g, ln2_b, w_ffn_gate, w_ffn_up, w_ffn_down):
    assert w_ada.shape[0] == DEPTH
    bp, s, d = x_prompt.shape
    bs, t, _ = x_sample.shape
    kv_shape = (2, N_KV, HEAD_DIM)

    w_cat = _prep_w_in(w_in[0])
    cmp_w = _prep_cmp(cmp_pe[0], cmp_w1[0], cmp_b1[0], cmp_w2[0])
    post_w = _prep_post(conv_w[0], g_conv_out[0], g_nsa_out[0], w_o[0], ln1_g[0], ln1_b[0], ln2_g[0], ln2_b[0],
                        w_ffn_gate[0], w_ffn_up[0], w_ffn_down[0])

    mod = _modulation(jnp.concatenate([c_prompt, c_sample], axis=0), w_ada[0], b_ada[0])
    mods_p = [m[:, None, :] for m in jnp.split(mod[:bp], 6, axis=-1)]
    mods_s = [jnp.repeat(m, t, axis=0)[None] for m in jnp.split(mod[bp:], 6, axis=-1)]

    u, gb, qz, kvc, kvs, kvw, kvs_b, kvw_b, gates = _in_proj(x_prompt, mods_p[1], mods_p[0], w_cat, IN_PROJ_ROWS, BF16)
    kv_cmpr = _compress_dense(kvc, *cmp_w)
    o = _attn_prompt(qz, kvs_b, kvw_b, kv_cmpr, gates)
    zero_state = jnp.zeros((bp, 1, CONV_DIM), F32)
    y_prompt = _post(x_prompt, o, u, gb, zero_state, zero_state, mods_p[2:], post_w, POST_ROWS, s)
    w_keep = min(WINDOW, s)
    cmp_p = kvc.reshape(1, bp, s, *kv_shape)
    slc_p = kvs.reshape(1, bp, s, *kv_shape)
    win_p = kvw[:, s - w_keep:].reshape(1, bp, w_keep, *kv_shape)
    conv_p = u[:, s - (CONV_W - 1):][None]

    rows = bs * t
    us, gbs, qzs, kvcs, kvss, kvws, _, _, gates_s = _in_proj(
        x_sample.reshape(1, rows, d), mods_s[1], mods_s[0], w_cat, rows, F32)
    per_b = lambda a: a.reshape(bs, t, a.shape[-1])
    kvcs, kvss, kvws = per_b(kvcs), per_b(kvss), per_b(kvws)
    xnew = jnp.pad(kvcs, ((0, 0), (0, (-t) % CMP_STRIDE), (0, 0))).reshape(bs, 1, CHUNK_COLS)
    kv_cmpr_s = _compress_paged(_feature_major(cache_cmp_kv[0]), page_table, xnew, *cmp_w)
    o_s = _attn_sample(per_b(qzs), kv_cmpr_s, per_b(gates_s), kvss, _feature_major(cache_slc_kv[0]),
                       _feature_major(cache_win_kv[0]), kvws, page_table)
    st = state_conv[0]
    tpos = jnp.arange(t)[None, :, None]
    st1 = jnp.where(tpos == 0, st[:, 1:2], 0.0).reshape(1, rows, CONV_DIM)
    st2 = jnp.where(tpos == 0, st[:, 0:1], jnp.where(tpos == 1, st[:, 1:2], 0.0)).reshape(1, rows, CONV_DIM)
    y_sample = _post(x_sample.reshape(1, rows, d), o_s.reshape(1, rows, NSA_DIM), us, gbs, st1, st2, mods_s[2:],
                     post_w, rows, t).reshape(bs, t, d)
    cmp_s = kvcs.reshape(1, bs, t, *kv_shape)
    slc_s = kvss.reshape(1, bs, t, *kv_shape)
    win_all = jnp.concatenate([cache_win_kv[0], kvws.reshape(bs, t, *kv_shape)], axis=1)
    win_s = win_all[:, t:][None]
    conv_s = jnp.concatenate([st, per_b(us)], axis=1)[:, t:][None]
    return (y_prompt, y_sample, cmp_p, slc_p, win_p, conv_p, cmp_s, slc_s, win_s, conv_s)
```
